```python
import jax, jax.numpy as jnp
from jax import lax
import numpy as np

D_MODEL = 1024
BATCH = 32
SEQ = 256
DEPTH = 1
DEC_BATCH = 4
DEC_SEQ = 1024
PAST_LEN = 512

GRID_W = 64
N_HEADS = 4
HEAD_K = 128
HEAD_V = 256
KEY_DIM = N_HEADS * HEAD_K
VAL_DIM = N_HEADS * HEAD_V
GATE_RANK = 16
GATE_NORMALIZER = 16.0
CHUNK = 64
CONV_DIM = D_MODEL
D_FF = 2816
N_MOD = 6
EPS = 1e-6
IN_SPLITS = (KEY_DIM, KEY_DIM, VAL_DIM, VAL_DIM, GATE_RANK, GATE_RANK,
             CONV_DIM, CONV_DIM, CONV_DIM, VAL_DIM, CONV_DIM)
IN_DIM = 2 * KEY_DIM + 3 * VAL_DIM + 2 * GATE_RANK + 4 * CONV_DIM

kernel_name = "bidir_gla_shortconv_convffn_prefix_dit"


def rmsnorm(x, g):
    xf = x.astype(jnp.float32)
    y = xf * lax.rsqrt(jnp.mean(xf * xf, axis=-1, keepdims=True) + EPS)
    return (y * g.astype(jnp.float32)).astype(x.dtype)


def split_cols(h):
    outs, off = [], 0
    for w in IN_SPLITS:
        outs.append(h[..., off:off + w])
        off += w
    return outs


def to_heads(t, hd):
    b, l, _ = t.shape
    return t.reshape(b, l, N_HEADS, hd).transpose(0, 2, 1, 3)


def short_conv1d(x, w):
    l = x.shape[1]
    xp = jnp.pad(x, ((0, 0), (1, 1), (0, 0)))
    return xp[:, :l] * w[0] + xp[:, 1:l + 1] * w[1] + xp[:, 2:] * w[2]


def dwconv3x3(x, w, b, rows, width):
    bn, l, f = x.shape
    img = x.reshape(bn, rows, width, f)
    out = lax.conv_general_dilated(img, w[:, :, None, :].astype(img.dtype), (1, 1), ((1, 1), (1, 1)),
                                   dimension_numbers=('NHWC', 'HWIO', 'NHWC'), feature_group_count=f)
    return out.reshape(bn, l, f) + b


def gla_chunked(q, k, v, g, s0):
    bn, h, l, dk = q.shape
    dv = v.shape[-1]
    n = l // CHUNK
    qc = q.reshape(bn, h, n, CHUNK, dk)
    kc = k.reshape(bn, h, n, CHUNK, dk)
    vc = v.reshape(bn, h, n, CHUNK, dv)
    bcum = jnp.cumsum(g.astype(jnp.float32).reshape(bn, h, n, CHUNK, dk), axis=3)
    b_last = bcum[:, :, :, -1:, :]
    b_ref = bcum[:, :, :, CHUNK // 2:CHUNK // 2 + 1, :]
    a = jnp.einsum('bhncd,bhnsd->bhncs', qc * jnp.exp(bcum - b_ref), kc * jnp.exp(b_ref - bcum))
    tril = jnp.tril(jnp.ones((CHUNK, CHUNK), dtype=bool))
    a = jnp.where(tril, a, 0.0)
    o_intra = jnp.einsum('bhncs,bhnsv->bhncv', a, vc)
    u = jnp.einsum('bhncd,bhncv->bhndv', kc * jnp.exp(b_last - bcum), vc).astype(jnp.float32)
    decay = jnp.exp(b_last[:, :, :, 0, :])

    def step(s, inp):
        dec, uu = inp
        return dec[..., None] * s + uu, s

    s_final, s_starts = lax.scan(step, s0.astype(jnp.float32),
                                 (jnp.moveaxis(decay, 2, 0), jnp.moveaxis(u, 2, 0)))
    s_starts = jnp.moveaxis(s_starts, 0, 2)
    o_inter = jnp.einsum('bhncd,bhndv->bhncv', qc * jnp.exp(bcum), s_starts)
    o = (o_inter + o_intra).reshape(bn, h, l, dv)
    return o.astype(v.dtype), s_final.astype(v.dtype)


def gla_bidir(q, k, v, g_f, g_b, s0_f, s0_b):
    o_f, s_f = gla_chunked(q, k, v, g_f, s0_f)
    flip = lambda t: t[:, :, ::-1]
    o_b, s_b = gla_chunked(flip(q), flip(k), flip(v), flip(g_b), s0_b)
    return o_f + flip(o_b), s_f, s_b


def block(x, cvec, s0_f, s0_b, rows, width,
          w_ada, b_ada, norm1_g, w_in, w_gk_f, b_gk_f, w_gk_b, b_gk_b, gla_norm_g,
          conv_mix_w, w_out, norm2_g, ffn_w_up, ffn_w_gate, ffn_conv_w, ffn_conv_b, ffn_w_down):
    bn, l, _ = x.shape
    mod = (jax.nn.silu(cvec) @ w_ada + b_ada).reshape(cvec.shape[0], 1, N_MOD, D_MODEL)
    sh1, sc1, ga1, sh2, sc2, ga2 = [mod[:, :, i] for i in range(N_MOD)]
    xn = rmsnorm(x, norm1_g) * (1.0 + sc1) + sh1
    q, k, v, g_out, code_f, code_b, c_b, c_c, c_x, gate_a, gate_b = split_cols(xn @ w_in)
    g_f = jax.nn.log_sigmoid((code_f @ w_gk_f + b_gk_f).astype(jnp.float32)) / GATE_NORMALIZER
    g_b = jax.nn.log_sigmoid((code_b @ w_gk_b + b_gk_b).astype(jnp.float32)) / GATE_NORMALIZER
    o, s_f, s_b = gla_bidir(to_heads(q, HEAD_K) * (HEAD_K ** -0.5), to_heads(k, HEAD_K), to_heads(v, HEAD_V),
                            to_heads(g_f, HEAD_K), to_heads(g_b, HEAD_K), s0_f, s0_b)
    o = rmsnorm(o, gla_norm_g).transpose(0, 2, 1, 3).reshape(bn, l, VAL_DIM) * jax.nn.silu(g_out)
    o_conv = c_b * short_conv1d(c_c * c_x, conv_mix_w)
    mix = (jax.nn.sigmoid(gate_a) * o + jax.nn.sigmoid(gate_b) * o_conv) @ w_out
    x = x + ga1 * mix
    xn2 = rmsnorm(x, norm2_g) * (1.0 + sc2) + sh2
    hup = dwconv3x3(xn2 @ ffn_w_up, ffn_conv_w, ffn_conv_b, rows, width)
    y = (jax.nn.silu(hup) * (xn2 @ ffn_w_gate)) @ ffn_w_down
    x = x + ga2 * y
    return x, s_f, s_b


def setup_inputs(seed: int = 0) -> dict:
    key = jax.random.key(seed)
    ks = jax.random.split(key, 26)
    nrm = lambda i, shape, s: jax.random.normal(ks[i], shape, jnp.float32) * s
    d = D_MODEL
    st_shape = (DEC_BATCH, DEPTH, N_HEADS, HEAD_K, HEAD_V)
    return {
        'x_prompt': nrm(0, (BATCH, SEQ, d), 1.0),
        'x_sample': nrm(1, (DEC_BATCH, DEC_SEQ, d), 1.0),
        'c': nrm(2, (DEC_BATCH, d), 1.0),
        'state_gla_fwd': nrm(3, st_shape, 1.0),
        'state_gla_bwd': nrm(4, st_shape, 1.0),
        'c_ctx': nrm(5, (d,), 1.0),
        'w_ada': nrm(6, (DEPTH, d, N_MOD * d), 0.5 * d ** -0.5),
        'b_ada': nrm(7, (DEPTH, N_MOD * d), 0.02),
        'norm1_g': 1.0 + nrm(8, (DEPTH, d), 0.02),
        'w_in': nrm(9, (DEPTH, d, IN_DIM), d ** -0.5),
        'w_gk_f': nrm(10, (DEPTH, GATE_RANK, KEY_DIM), GATE_RANK ** -0.5),
        'b_gk_f': nrm(11, (DEPTH, KEY_DIM), 0.1),
        'w_gk_b': nrm(12, (DEPTH, GATE_RANK, KEY_DIM), GATE_RANK ** -0.5),
        'b_gk_b': nrm(13, (DEPTH, KEY_DIM), 0.1),
        'gla_norm_g': 1.0 + nrm(14, (DEPTH, HEAD_V), 0.02),
        'conv_mix_w': nrm(15, (DEPTH, 3, CONV_DIM), 3 ** -0.5),
        'w_out': nrm(16, (DEPTH, VAL_DIM, d), VAL_DIM ** -0.5),
        'norm2_g': 1.0 + nrm(17, (DEPTH, d), 0.02),
        'ffn_w_up': nrm(18, (DEPTH, d, D_FF), d ** -0.5),
        'ffn_w_gate': nrm(19, (DEPTH, d, D_FF), d ** -0.5),
        'ffn_conv_w': nrm(20, (DEPTH, 3, 3, D_FF), 1.0 / 3.0),
        'ffn_conv_b': nrm(21, (DEPTH, D_FF), 0.02),
        'ffn_w_down': nrm(22, (DEPTH, D_FF, d), D_FF ** -0.5),
        'normf_g': 1.0 + nrm(23, (d,), 0.02),
    }


def reference(x_prompt, x_sample, c, state_gla_fwd, state_gla_bwd, c_ctx, w_ada, b_ada, norm1_g, w_in,
              w_gk_f, b_gk_f, w_gk_b, b_gk_b, gla_norm_g, conv_mix_w, w_out, norm2_g,
              ffn_w_up, ffn_w_gate, ffn_conv_w, ffn_conv_b, ffn_w_down, normf_g):
    bp, lp, _ = x_prompt.shape
    ls = x_sample.shape[1]
    rows = ls // GRID_W
    zero_state = jnp.zeros((bp, N_HEADS, HEAD_K, HEAD_V), x_prompt.dtype)
    xp, xs = x_prompt, x_sample
    new_f, new_b = [], []
    for l in range(DEPTH):
        p = (w_ada[l], b_ada[l], norm1_g[l], w_in[l], w_gk_f[l], b_gk_f[l], w_gk_b[l], b_gk_b[l],
             gla_norm_g[l], conv_mix_w[l], w_out[l], norm2_g[l], ffn_w_up[l], ffn_w_gate[l],
             ffn_conv_w[l], ffn_conv_b[l], ffn_w_down[l])
        xp, s_f, s_b = block(xp, c_ctx[None, :], zero_state, zero_state, 1, lp, *p)
        new_f.append(s_f)
        new_b.append(s_b)
        xs, _, _ = block(xs, c, state_gla_fwd[:, l], state_gla_bwd[:, l], rows, GRID_W, *p)
    y_prompt = rmsnorm(xp, normf_g)
    y_sample = rmsnorm(xs, normf_g)
    return (y_prompt, y_sample, jnp.stack(new_f, axis=1), jnp.stack(new_b, axis=1))
```

```python
import functools

import jax
import jax.numpy as jnp
from jax import lax
from jax.experimental import pallas as pl
from jax.experimental.pallas import tpu as pltpu

D_MODEL = 1024
N_HEADS = 4
HEAD_K = 128
HEAD_V = 256
KEY_DIM = N_HEADS * HEAD_K
VAL_DIM = N_HEADS * HEAD_V
GATE_RANK = 16
GATE_NORMALIZER = 16.0
CONV_DIM = D_MODEL
D_FF = 2816
N_MOD = 6
EPS = 1e-6
GRID_W = 64
IN_SPLITS = (KEY_DIM, KEY_DIM, VAL_DIM, VAL_DIM, GATE_RANK, GATE_RANK,
             CONV_DIM, CONV_DIM, CONV_DIM, VAL_DIM, CONV_DIM)

DIAG_BLOCK = 64
SUPER_BLOCK = 256
MXU_LANES_V7X = 128
FFN_TILE = 256
FFN_ROWS = 1024
INPROJ_ROWS = 512
ADA_TILE = 1024
MOD_ROWS = 8
VMEM_LIMIT_BYTES = 56 * 1024 * 1024

F32 = jnp.float32
BF16 = jnp.bfloat16


def _dot(a, b):
    return jnp.dot(a, b, preferred_element_type=F32)


def _dot_nt(a, b):
    return lax.dot_general(a, b, (((1,), (1,)), ((), ())), preferred_element_type=F32)


def _dot_tn(a, b):
    return lax.dot_general(a, b, (((0,), (0,)), ((), ())), preferred_element_type=F32)


def _sigmoid(x):
    return 1.0 / (1.0 + jnp.exp(-x))


def _rms(x, g):
    return x * lax.rsqrt(jnp.mean(x * x, axis=-1, keepdims=True) + EPS) * g


def _const_spec(shape):
    nd = len(shape)
    return pl.BlockSpec(shape, lambda *_: (0,) * nd, pipeline_mode=pl.Buffered(1))


def _params(*sem):
    return pltpu.CompilerParams(dimension_semantics=sem, vmem_limit_bytes=VMEM_LIMIT_BYTES)


def _ada_kernel(c_ref, w_ref, b_ref, o_ref):
    c = c_ref[...]
    s = (c * _sigmoid(c)).astype(BF16)
    o_ref[...] = _dot(s, w_ref[...].astype(BF16)) + b_ref[...]


def _ada(cc, w_ada, b_ada):
    n = w_ada.shape[1]
    return pl.pallas_call(
        _ada_kernel,
        out_shape=jax.ShapeDtypeStruct((MOD_ROWS, n), F32),
        grid=(n // ADA_TILE,),
        in_specs=[pl.BlockSpec((MOD_ROWS, D_MODEL), lambda j: (0, 0)),
                  pl.BlockSpec((D_MODEL, ADA_TILE), lambda j: (0, j)),
                  pl.BlockSpec((1, ADA_TILE), lambda j: (0, j))],
        out_specs=pl.BlockSpec((MOD_ROWS, ADA_TILE), lambda j: (0, j)),
        compiler_params=_params("arbitrary"),
        name="ada",
    )(cc, w_ada, b_ada)


def _inproj_kernel(x_ref, mod_ref, n1_ref, wq_ref, wk_ref, wv_ref, wgo_ref, wcode_ref, wcb_ref, wcc_ref,
                   wcx_ref, wga_ref, wgb_ref, wgk_ref, bgk_ref,
                   q_ref, k_ref, v_ref, og_ref, p_ref, cbg_ref, g_ref):
    mod = mod_ref[...]
    sh1 = mod[:, 0:D_MODEL]
    sc1 = mod[:, D_MODEL:2 * D_MODEL]
    xb = (_rms(x_ref[...], n1_ref[...]) * (1.0 + sc1) + sh1).astype(BF16)

    q_ref[...] = (_dot(xb, wq_ref[...]) * (HEAD_K ** -0.5)).astype(BF16)
    k_ref[...] = _dot(xb, wk_ref[...]).astype(BF16)
    v_ref[...] = _dot(xb, wv_ref[...]).astype(BF16)
    g_out = _dot(xb, wgo_ref[...])
    og_ref[...] = (g_out * _sigmoid(g_out) * _sigmoid(_dot(xb, wga_ref[...]))).astype(BF16)
    p_ref[...] = (_dot(xb, wcc_ref[...]) * _dot(xb, wcx_ref[...])).astype(BF16)
    cbg_ref[...] = (_sigmoid(_dot(xb, wgb_ref[...])) * _dot(xb, wcb_ref[...])).astype(BF16)
    z = _dot(_dot(xb, wcode_ref[...]).astype(BF16), wgk_ref[...]) + bgk_ref[...]
    g_ref[...] = (jnp.minimum(z, 0.0) - jnp.log(1.0 + jnp.exp(-jnp.abs(z)))) * (1.0 / GATE_NORMALIZER)


def _inproj(x, mod3, n1g, ws, wgk, bgk, *, tm, mod_row):
    n = x.shape[0]
    row = lambda i: (i, 0)
    wide = lambda w, dt: (jax.ShapeDtypeStruct((n, w), dt), pl.BlockSpec((tm, w), row))
    outs = [wide(KEY_DIM, BF16), wide(KEY_DIM, BF16), wide(VAL_DIM, BF16), wide(VAL_DIM, BF16),
            wide(CONV_DIM, BF16), wide(CONV_DIM, BF16), wide(2 * KEY_DIM, F32)]
    return pl.pallas_call(
        _inproj_kernel,
        out_shape=[o[0] for o in outs],
        grid=(n // tm,),
        in_specs=[pl.BlockSpec((tm, D_MODEL), row),
                  pl.BlockSpec((None, 1, N_MOD * D_MODEL), lambda i: (mod_row(i), 0, 0)),
                  _const_spec(n1g.shape)]
                 + [_const_spec(w.shape) for w in ws]
                 + [_const_spec(wgk.shape), _const_spec(bgk.shape)],
        out_specs=[o[1] for o in outs],
        compiler_params=_params("arbitrary"),
        name="in_proj",
    )(x, mod3, n1g, *ws, wgk, bgk)


def _shift_down(x, rows):
    return jnp.where(rows >= 1, pltpu.roll(x, 1, axis=0), 0.0)


def _shift_up(x, rows):
    n = x.shape[0]
    return jnp.where(rows < n - 1, pltpu.roll(x, n - 1, axis=0), 0.0)


def _prefix_sum(x, rows):
    n, s = x.shape[0], 1
    while s < n:
        x = x + jnp.where(rows >= s, pltpu.roll(x, s, axis=0), 0.0)
        s *= 2
    return x


def _suffix_sum(x, rows):
    n, s = x.shape[0], 1
    while s < n:
        x = x + jnp.where(rows < n - s, pltpu.roll(x, n - s, axis=0), 0.0)
        s *= 2
    return x


def _block_rows(a, blk, off):
    parts = [jnp.broadcast_to(a[c * blk + off:c * blk + off + 1, :], (blk, a.shape[1]))
             for c in range(a.shape[0] // blk)]
    return parts[0] if len(parts) == 1 else jnp.concatenate(parts, axis=0)


def _mixer_kernel(*refs, seq, heads, has_state_in, has_state_out):
    (q_ref, k_ref, v_ref, gf_ref, gb_ref, og_ref, p_ref, cbg_ref, cw_ref, gn_ref, wo_ref, x_ref,
     mod_ref) = refs[:13]
    refs = refs[13:]
    if has_state_in:
        s0f_ref, s0b_ref = refs[:2]
        refs = refs[2:]
    x1_ref = refs[0]
    refs = refs[1:]
    if has_state_out:
        sf_ref, sb_ref = refs[:2]
        refs = refs[2:]
    a_ref, acc_ref = refs
    hstep = pl.program_id(1)
    n_super = seq // SUPER_BLOCK

    rows_k = lax.broadcasted_iota(jnp.int32, (seq, heads * HEAD_K), 0)
    bf_all = _prefix_sum(gf_ref[...], rows_k)
    sb_all = _suffix_sum(gb_ref[...], rows_k)

    rows1 = rows_k[:, :HEAD_K]
    rows_v = lax.broadcasted_iota(jnp.int32, (seq, HEAD_V), 0)
    r2 = lax.broadcasted_iota(jnp.int32, (SUPER_BLOCK, SUPER_BLOCK), 0)
    c2 = lax.broadcasted_iota(jnp.int32, (SUPER_BLOCK, SUPER_BLOCK), 1)
    same64 = (r2 & -DIAG_BLOCK) == (c2 & -DIAG_BLOCK)
    same128 = (r2 & -2 * DIAG_BLOCK) == (c2 & -2 * DIAG_BLOCK)

    contrib = None
    for h in range(heads):
        ks = slice(h * HEAD_K, (h + 1) * HEAD_K)
        vs = slice(h * HEAD_V, (h + 1) * HEAD_V)
        q = q_ref[:, ks].astype(F32)
        k = k_ref[:, ks].astype(F32)
        v = v_ref[:, vs]
        bf = bf_all[:, ks]
        sb = sb_all[:, ks]

        rf0 = _block_rows(bf, DIAG_BLOCK, DIAG_BLOCK // 2)
        rb0 = _block_rows(sb, DIAG_BLOCK, DIAG_BLOCK // 2 - 1)
        q0f = (q * jnp.exp(bf - rf0)).astype(BF16)
        k0f = (k * jnp.exp(rf0 - bf)).astype(BF16)
        q0b = (q * jnp.exp(sb - rb0)).astype(BF16)
        k0b = (k * jnp.exp(rb0 - sb)).astype(BF16)

        lhs_levels, rhs_levels = [], []
        s = DIAG_BLOCK
        while s < seq:
            rf = _block_rows(bf, 2 * s, s - 1)
            rb = _block_rows(sb, 2 * s, s)
            odd = (rows1 & (2 * s - 1)) >= s
            ef = bf - rf
            eb = sb - rb
            lhs_levels.append((q * jnp.exp(jnp.where(odd, ef, eb))).astype(BF16))
            rhs_levels.append((k * jnp.exp(jnp.where(odd, -eb, -ef))).astype(BF16))
            s *= 2

        for i in range(n_super):
            rs = slice(i * SUPER_BLOCK, (i + 1) * SUPER_BLOCK)
            m0f = _dot_nt(q0f[rs], k0f[rs])
            m0b = _dot_nt(q0b[rs], k0b[rs])
            m1 = _dot_nt(lhs_levels[0][rs], rhs_levels[0][rs])
            m2 = _dot_nt(lhs_levels[1][rs], rhs_levels[1][rs])
            diag = jnp.where(c2 <= r2, m0f, 0.0) + jnp.where(c2 >= r2, m0b, 0.0)
            a_ref[rs, rs] = jnp.where(same64, diag, jnp.where(same128, m1, m2)).astype(BF16)
        s = SUPER_BLOCK
        lev = 2
        while s < seq:
            lhs, rhs = lhs_levels[lev], rhs_levels[lev]
            for pair in range(seq // (2 * s)):
                ev = slice(pair * 2 * s, pair * 2 * s + s)
                od = slice(pair * 2 * s + s, (pair + 1) * 2 * s)
                a_ref[od, ev] = _dot_nt(lhs[od], rhs[ev]).astype(BF16)
                a_ref[ev, od] = _dot_nt(lhs[ev], rhs[od]).astype(BF16)
            s *= 2
            lev += 1

        o = _dot(a_ref[...], v)
        if has_state_in:
            qi = jnp.concatenate([(q * jnp.exp(bf)).astype(BF16), (q * jnp.exp(sb)).astype(BF16)], axis=1)
            s0 = jnp.concatenate([s0f_ref[h].astype(BF16), s0b_ref[h].astype(BF16)], axis=0)
            o = o + _dot(qi, s0)
        if has_state_out:
            sf_ref[h] = _dot_tn((k * jnp.exp(bf[seq - 1:seq, :] - bf)).astype(BF16), v)
            sb_ref[h] = _dot_tn((k * jnp.exp(sb[0:1, :] - sb)).astype(BF16), v)

        o = _rms(o, gn_ref[...])
        p = p_ref[:, vs].astype(F32)
        conv = (_shift_down(p, rows_v) * cw_ref[0:1, vs] + p * cw_ref[1:2, vs]
                + _shift_up(p, rows_v) * cw_ref[2:3, vs])
        mix = og_ref[:, vs].astype(F32) * o + cbg_ref[:, vs].astype(F32) * conv
        part = _dot(mix.astype(BF16), wo_ref[vs, :])
        contrib = part if contrib is None else contrib + part

    @pl.when(hstep == 0)
    def _():
        acc_ref[...] = contrib

    @pl.when(hstep != 0)
    def _():
        acc_ref[...] += contrib

    @pl.when(hstep == pl.num_programs(1) - 1)
    def _():
        ga1 = mod_ref[:, 2 * D_MODEL:3 * D_MODEL]
        x1_ref[...] = x_ref[...] + ga1 * acc_ref[...]


def _mixer(x, mod3, proj, conv_w, gn, wo, states, *, seq, heads, mod_row, has_state_out):
    q, k, v, og, p, cbg, g = proj
    n = x.shape[0]
    nb = n // seq
    hsteps = N_HEADS // heads
    has_state_in = states is not None
    col = lambda b, h: (b, h)
    st_block = (None, None, heads, HEAD_K, HEAD_V)
    st_map = lambda b, h: (b, 0, h, 0, 0)
    in_specs = [pl.BlockSpec((seq, heads * HEAD_K), col),
                pl.BlockSpec((seq, heads * HEAD_K), col),
                pl.BlockSpec((seq, heads * HEAD_V), col),
                pl.BlockSpec((seq, heads * HEAD_K), col),
                pl.BlockSpec((seq, heads * HEAD_K), lambda b, h: (b, hsteps + h)),
                pl.BlockSpec((seq, heads * HEAD_V), col),
                pl.BlockSpec((seq, heads * HEAD_V), col),
                pl.BlockSpec((seq, heads * HEAD_V), col),
                pl.BlockSpec((3, heads * HEAD_V), lambda b, h: (0, h)),
                pl.BlockSpec((1, HEAD_V), lambda b, h: (0, 0)),
                pl.BlockSpec((heads * HEAD_V, D_MODEL), lambda b, h: (h, 0)),
                pl.BlockSpec((seq, D_MODEL), lambda b, h: (b, 0)),
                pl.BlockSpec((None, 1, N_MOD * D_MODEL), lambda b, h: (mod_row(b), 0, 0))]
    args = [q, k, v, g, g, og, p, cbg, conv_w, gn, wo, x, mod3]
    if has_state_in:
        in_specs += [pl.BlockSpec(st_block, st_map)] * 2
        args += list(states)
    out_shape = [jax.ShapeDtypeStruct((n, D_MODEL), F32)]
    out_specs = [pl.BlockSpec((seq, D_MODEL), lambda b, h: (b, 0))]
    if has_state_out:
        out_shape += [jax.ShapeDtypeStruct((nb, 1, N_HEADS, HEAD_K, HEAD_V), F32)] * 2
        out_specs += [pl.BlockSpec(st_block, st_map)] * 2
    return pl.pallas_call(
        functools.partial(_mixer_kernel, seq=seq, heads=heads, has_state_in=has_state_in,
                          has_state_out=has_state_out),
        out_shape=out_shape,
        grid=(nb, hsteps),
        in_specs=in_specs,
        out_specs=out_specs,
        scratch_shapes=[pltpu.VMEM((seq, seq), BF16), pltpu.VMEM((seq, D_MODEL), F32)],
        compiler_params=_params("arbitrary", "arbitrary"),
        name="mixer",
    )(*args)


def _ffn_kernel(x1_ref, mod_ref, n2_ref, wu_ref, wg_ref, cw_ref, cb_ref, wd_ref, nf_ref, o_ref, acc_ref,
                *, seq, width):
    tm = x1_ref.shape[0]
    mod = mod_ref[...]
    sh2 = mod[:, 3 * D_MODEL:4 * D_MODEL]
    sc2 = mod[:, 4 * D_MODEL:5 * D_MODEL]
    ga2 = mod[:, 5 * D_MODEL:6 * D_MODEL]
    x1 = x1_ref[...]
    xb = (_rms(x1, n2_ref[...]) * (1.0 + sc2) + sh2).astype(BF16)

    rows = lax.broadcasted_iota(jnp.int32, (tm, FFN_TILE), 0)
    col_in_row = rows & (width - 1)
    has_left = col_in_row != 0
    has_right = col_in_row != width - 1
    n_rows = seq // width
    zrow = jnp.zeros((width, FFN_TILE), F32)

    def up_rows(z):
        return jnp.concatenate([zrow, z[:tm - width]], axis=0)

    def down_rows(z):
        return jnp.concatenate([z[width:], zrow], axis=0)

    for f in range(D_FF // FFN_TILE):
        fs = slice(f * FFN_TILE, (f + 1) * FFN_TILE)
        up = _dot(xb, wu_ref[:, fs])
        left = jnp.where(has_left, pltpu.roll(up, 1, axis=0), 0.0)
        right = jnp.where(has_right, pltpu.roll(up, tm - 1, axis=0), 0.0)
        w = lambda i, j: cw_ref[3 * i + j:3 * i + j + 1, fs]
        cv = left * w(1, 0) + up * w(1, 1) + right * w(1, 2) + cb_ref[:, fs]
        if n_rows > 1:
            cv = cv + (up_rows(left) * w(0, 0) + up_rows(up) * w(0, 1) + up_rows(right) * w(0, 2)
                       + down_rows(left) * w(2, 0) + down_rows(up) * w(2, 1) + down_rows(right) * w(2, 2))
        act = (cv * _sigmoid(cv) * _dot(xb, wg_ref[:, fs])).astype(BF16)
        part = _dot(act, wd_ref[fs, :])
        if f == 0:
            acc_ref[...] = part
        else:
            acc_ref[...] += part

    o_ref[...] = _rms(x1 + ga2 * acc_ref[...], nf_ref[...])


def _ffn(x1, mod3, n2g, wu, wg, cw, cb, wd, nfg, *, tm, seq, width, mod_row):
    n = x1.shape[0]
    row = lambda i: (i, 0)
    assert width & (width - 1) == 0 and tm % seq == 0 and (seq == width or tm == seq)
    return pl.pallas_call(
        functools.partial(_ffn_kernel, seq=seq, width=width),
        out_shape=jax.ShapeDtypeStruct((n, D_MODEL), F32),
        grid=(n // tm,),
        in_specs=[pl.BlockSpec((tm, D_MODEL), row),
                  pl.BlockSpec((None, 1, N_MOD * D_MODEL), lambda i: (mod_row(i), 0, 0)),
                  _const_spec(n2g.shape), _const_spec(wu.shape), _const_spec(wg.shape),
                  _const_spec(cw.shape), _const_spec(cb.shape), _const_spec(wd.shape),
                  _const_spec(nfg.shape)],
        out_specs=pl.BlockSpec((tm, D_MODEL), row),
        scratch_shapes=[pltpu.VMEM((tm, D_MODEL), F32)],
        compiler_params=_params("arbitrary"),
        name="ffn",
    )(x1, mod3, n2g, wu, wg, cw, cb, wd, nfg)


def kernel(x_prompt, x_sample, c, state_gla_fwd, state_gla_bwd, c_ctx, w_ada, b_ada, norm1_g, w_in, w_gk_f,
           b_gk_f, w_gk_b, b_gk_b, gla_norm_g, conv_mix_w, w_out, norm2_g, ffn_w_up, ffn_w_gate,
           ffn_conv_w, ffn_conv_b, ffn_w_down, normf_g):
    bp, lp, d = x_prompt.shape
    bs, ls, _ = x_sample.shape
    assert w_ada.shape[0] == 1 and d == D_MODEL and bs + 1 <= MOD_ROWS

    cc = jnp.zeros((MOD_ROWS, d), F32).at[0].set(c_ctx).at[1:1 + bs].set(c)
    mod3 = _ada(cc, w_ada[0], b_ada[0][None, :]).reshape(MOD_ROWS, 1, N_MOD * d)

    cols, off = [], 0
    for wdt in IN_SPLITS:
        cols.append(w_in[0][:, off:off + wdt])
        off += wdt
    wq, wk, wv, wgo, wcf, wcb_, wcb, wcc, wcx, wga, wgb = cols
    wcode = jnp.zeros((d, MXU_LANES_V7X), F32).at[:, :GATE_RANK].set(wcf).at[:, GATE_RANK:2 * GATE_RANK].set(wcb_)
    wgk = (jnp.zeros((MXU_LANES_V7X, 2 * KEY_DIM), F32)
           .at[:GATE_RANK, :KEY_DIM].set(w_gk_f[0]).at[GATE_RANK:2 * GATE_RANK, KEY_DIM:].set(w_gk_b[0]))
    bgk = jnp.concatenate([b_gk_f[0], b_gk_b[0]])[None, :]
    ws = [w.astype(BF16) for w in (wq, wk, wv, wgo, wcode, wcb, wcc, wcx, wga, wgb)]
    wgk = wgk.astype(BF16)
    wo = w_out[0].astype(BF16)
    wu, wg, wd = ffn_w_up[0].astype(BF16), ffn_w_gate[0].astype(BF16), ffn_w_down[0].astype(BF16)
    cw9 = ffn_conv_w[0].reshape(9, D_FF)
    cb = ffn_conv_b[0][None, :]
    n1g, n2g, nfg, gn = norm1_g[0][None, :], norm2_g[0][None, :], normf_g[None, :], gla_norm_g[0][None, :]

    def group(x3, *, seq, width, heads, states, mod_of_seq):
        x = x3.reshape(-1, d)
        proj = _inproj(x, mod3, n1g, ws, wgk, bgk, tm=INPROJ_ROWS,
                       mod_row=lambda i: mod_of_seq((i * INPROJ_ROWS) // seq))
        outs = _mixer(x, mod3, proj, conv_mix_w[0], gn, wo, states, seq=seq, heads=heads,
                      mod_row=mod_of_seq, has_state_out=states is None)
        y = _ffn(outs[0], mod3, n2g, wu, wg, cw9, cb, wd, nfg, tm=FFN_ROWS, seq=seq, width=width,
                 mod_row=lambda i: mod_of_seq((i * FFN_ROWS) // seq))
        return y.reshape(x3.shape), outs[1:]

    y_prompt, (new_f, new_b) = group(x_prompt, seq=lp, width=lp, heads=N_HEADS, states=None,
                                     mod_of_seq=lambda b: 0)
    y_sample, _ = group(x_sample, seq=ls, width=GRID_W, heads=1,
                        states=(state_gla_fwd, state_gla_bwd), mod_of_seq=lambda b: 1 + b)
    return y_prompt, y_sample, new_f, new_b
```

```python
import functools

import jax
import jax.numpy as jnp
from jax import lax
from jax.experimental import pallas as pl
from jax.experimental.pallas import tpu as pltpu

D_MODEL = 1024
N_HEADS = 4
HEAD_K = 128
HEAD_V = 256
KEY_DIM = N_HEADS * HEAD_K
VAL_DIM = N_HEADS * HEAD_V
GATE_RANK = 16
GATE_NORMALIZER = 16.0
LOG2_E = 1.4426950408889634
CONV_DIM = D_MODEL
D_FF = 2816
N_MOD = 6
EPS = 1e-6
GRID_W = 64
IN_SPLITS = (KEY_DIM, KEY_DIM, VAL_DIM, VAL_DIM, GATE_RANK, GATE_RANK,
             CONV_DIM, CONV_DIM, CONV_DIM, VAL_DIM, CONV_DIM)

DIAG_BLOCK = 64
SUPER_BLOCK = 256
MXU_LANES_V7X = 128
FFN_TILE = 256
FFN_ROWS = 1024
INPROJ_ROWS = 512
ADA_TILE = 1024
MOD_ROWS = 8
VMEM_LIMIT_BYTES = 56 * 1024 * 1024

F32 = jnp.float32
BF16 = jnp.bfloat16


def _dot(a, b):
    return jnp.dot(a, b, preferred_element_type=F32)


def _dot_nt(a, b):
    return lax.dot_general(a, b, (((1,), (1,)), ((), ())), preferred_element_type=F32)


def _dot_tn(a, b):
    return lax.dot_general(a, b, (((0,), (0,)), ((), ())), preferred_element_type=F32)


def _sigmoid(x):
    return 1.0 / (1.0 + jnp.exp(-x))


def _rms(x, g):
    return x * lax.rsqrt(jnp.mean(x * x, axis=-1, keepdims=True) + EPS) * g


def _const_spec(shape):
    nd = len(shape)
    return pl.BlockSpec(shape, lambda *_: (0,) * nd, pipeline_mode=pl.Buffered(1))


def _params(*sem):
    return pltpu.CompilerParams(dimension_semantics=sem, vmem_limit_bytes=VMEM_LIMIT_BYTES)


def _ada_kernel(c_ref, w_ref, b_ref, o_ref):
    c = c_ref[...]
    s = (c * _sigmoid(c)).astype(BF16)
    o_ref[...] = _dot(s, w_ref[...].astype(BF16)) + b_ref[...]


def _ada(cc, w_ada, b_ada):
    n = w_ada.shape[1]
    return pl.pallas_call(
        _ada_kernel,
        out_shape=jax.ShapeDtypeStruct((MOD_ROWS, n), F32),
        grid=(n // ADA_TILE,),
        in_specs=[pl.BlockSpec((MOD_ROWS, D_MODEL), lambda j: (0, 0)),
                  pl.BlockSpec((D_MODEL, ADA_TILE), lambda j: (0, j)),
                  pl.BlockSpec((1, ADA_TILE), lambda j: (0, j))],
        out_specs=pl.BlockSpec((MOD_ROWS, ADA_TILE), lambda j: (0, j)),
        compiler_params=_params("arbitrary"),
        name="ada",
    )(cc, w_ada, b_ada)


def _prefix_sum(x, pos, seg):
    s = 1
    while s < seg:
        x = x + jnp.where(pos >= s, pltpu.roll(x, s, axis=0), 0.0)
        s *= 2
    return x


def _suffix_sum(x, pos, seg):
    n, s = x.shape[0], 1
    while s < seg:
        x = x + jnp.where(pos < seg - s, pltpu.roll(x, n - s, axis=0), 0.0)
        s *= 2
    return x


def _inproj_kernel(x_ref, mod_ref, n1_ref, wq_ref, wk_ref, wv_ref, wgo_ref, wcode_ref, wcb_ref, wcc_ref,
                   wcx_ref, wga_ref, wgb_ref, wgk_ref, bgk_ref,
                   q_ref, k_ref, v_ref, og_ref, p_ref, cbg_ref, g_ref, *, seg):
    mod = mod_ref[...]
    sh1 = mod[:, 0:D_MODEL]
    sc1 = mod[:, D_MODEL:2 * D_MODEL]
    xb = (_rms(x_ref[...], n1_ref[...]) * (1.0 + sc1) + sh1).astype(BF16)

    q_ref[...] = (_dot(xb, wq_ref[...]) * (HEAD_K ** -0.5)).astype(BF16)
    k_ref[...] = _dot(xb, wk_ref[...]).astype(BF16)
    v_ref[...] = _dot(xb, wv_ref[...]).astype(BF16)
    g_out = _dot(xb, wgo_ref[...])
    og_ref[...] = (g_out * _sigmoid(g_out) * _sigmoid(_dot(xb, wga_ref[...]))).astype(BF16)
    p_ref[...] = (_dot(xb, wcc_ref[...]) * _dot(xb, wcx_ref[...])).astype(BF16)
    cbg_ref[...] = (_sigmoid(_dot(xb, wgb_ref[...])) * _dot(xb, wcb_ref[...])).astype(BF16)
    z = _dot(_dot(xb, wcode_ref[...]).astype(BF16), wgk_ref[...]) + bgk_ref[...]
    g = (jnp.minimum(z, 0.0) - jnp.log(1.0 + jnp.exp(-jnp.abs(z)))) * (LOG2_E / GATE_NORMALIZER)
    pos = lax.broadcasted_iota(jnp.int32, (g.shape[0], KEY_DIM), 0) & (seg - 1)
    g_ref[:, :KEY_DIM] = _prefix_sum(g[:, :KEY_DIM], pos, seg)
    g_ref[:, KEY_DIM:] = _suffix_sum(g[:, KEY_DIM:], pos, seg)


def _inproj(x, mod3, n1g, ws, wgk, bgk, *, tm, seg, mod_row):
    n = x.shape[0]
    assert seg & (seg - 1) == 0 and tm % seg == 0
    row = lambda i: (i, 0)
    wide = lambda w, dt: (jax.ShapeDtypeStruct((n, w), dt), pl.BlockSpec((tm, w), row))
    outs = [wide(KEY_DIM, BF16), wide(KEY_DIM, BF16), wide(VAL_DIM, BF16), wide(VAL_DIM, BF16),
            wide(CONV_DIM, BF16), wide(CONV_DIM, BF16), wide(2 * KEY_DIM, F32)]
    return pl.pallas_call(
        functools.partial(_inproj_kernel, seg=seg),
        out_shape=[o[0] for o in outs],
        grid=(n // tm,),
        in_specs=[pl.BlockSpec((tm, D_MODEL), row),
                  pl.BlockSpec((None, 1, N_MOD * D_MODEL), lambda i: (mod_row(i), 0, 0)),
                  _const_spec(n1g.shape)]
                 + [_const_spec(w.shape) for w in ws]
                 + [_const_spec(wgk.shape), _const_spec(bgk.shape)],
        out_specs=[o[1] for o in outs],
        compiler_params=_params("arbitrary"),
        name="in_proj",
    )(x, mod3, n1g, *ws, wgk, bgk)


def _cat(parts, axis=0):
    return parts[0] if len(parts) == 1 else jnp.concatenate(parts, axis=axis)


def _join_prefix(b, seg):
    parts = [b[:seg]]
    for c in range(1, b.shape[0] // seg):
        parts.append(b[c * seg:(c + 1) * seg] + parts[-1][seg - 1:seg, :])
    return _cat(parts)


def _join_suffix(b, seg):
    n = b.shape[0] // seg
    parts = [b[(n - 1) * seg:]]
    for c in range(n - 2, -1, -1):
        parts.insert(0, b[c * seg:(c + 1) * seg] + parts[0][0:1, :])
    return _cat(parts)


def _diag_args(b, off):
    return _cat([b[c:c + DIAG_BLOCK] - b[c + off:c + off + 1, :] for c in range(0, b.shape[0], DIAG_BLOCK)])


def _pair_args(bf, sb, s):
    lhs, rhs = [], []
    for e0 in range(0, bf.shape[0], 2 * s):
        o0 = e0 + s
        rf = bf[o0 - 1:o0, :]
        rb = sb[o0:o0 + 1, :]
        lhs += [sb[e0:o0] - rb, bf[o0:o0 + s] - rf]
        rhs += [rf - bf[e0:o0], rb - sb[o0:o0 + s]]
    return _cat(lhs), _cat(rhs)


def _mixer_kernel(*refs, seq, heads, seg, has_state_in, has_state_out):
    (q_ref, k_ref, v_ref, gf_ref, gb_ref, og_ref, p_ref, cbg_ref, cw_ref, gn_ref, wo_ref, x_ref,
     mod_ref) = refs[:13]
    refs = refs[13:]
    if has_state_in:
        s0f_ref, s0b_ref = refs[:2]
        refs = refs[2:]
    x1_ref = refs[0]
    refs = refs[1:]
    if has_state_out:
        sf_ref, sb_ref = refs[:2]
        refs = refs[2:]
    a_ref = refs[0]
    hsteps = N_HEADS // heads
    half = SUPER_BLOCK // 2

    bf_all = _join_prefix(gf_ref[...], seg)
    sb_all = _join_suffix(gb_ref[...], seg)

    rows_v = lax.broadcasted_iota(jnp.int32, (seq, HEAD_V), 0)
    r_loc = lax.broadcasted_iota(jnp.int32, (DIAG_BLOCK, half), 0)
    lane = lax.broadcasted_iota(jnp.int32, (DIAG_BLOCK, half), 1)
    ones = jnp.ones((HEAD_V, half), BF16)
    scale = lambda t, e: t * e.astype(BF16)

    mixes = []
    for h in range(heads):
        ks = slice(h * HEAD_K, (h + 1) * HEAD_K)
        vs = slice(h * HEAD_V, (h + 1) * HEAD_V)
        q = q_ref[:, ks]
        k = k_ref[:, ks]
        v = v_ref[:, vs]
        bf = bf_all[:, ks]
        sb = sb_all[:, ks]

        e0f = jnp.exp2(_diag_args(bf, DIAG_BLOCK // 2))
        e0b = jnp.exp2(_diag_args(sb, DIAG_BLOCK // 2 - 1))
        q0f, k0f = scale(q, e0f), scale(k, 1.0 / e0f)
        q0b, k0b = scale(q, e0b), scale(k, 1.0 / e0b)

        lhs_levels, rhs_levels = [], []
        s = DIAG_BLOCK
        while s < seq:
            la, ra = _pair_args(bf, sb, s)
            lhs_levels.append(scale(q, jnp.exp2(la)))
            rhs_levels.append(scale(k, jnp.exp2(ra)))
            s *= 2

        for base in range(0, seq, SUPER_BLOCK):
            rs = slice(base, base + SUPER_BLOCK)
            m0f = _dot_nt(q0f[rs], k0f[rs])
            m0b = _dot_nt(q0b[rs], k0b[rs])
            m1 = _dot_nt(lhs_levels[0][rs], rhs_levels[0][rs])
            m2 = _dot_nt(lhs_levels[1][rs], rhs_levels[1][rs])
            for bi in range(SUPER_BLOCK // DIAG_BLOCK):
                rr = slice(bi * DIAG_BLOCK, (bi + 1) * DIAG_BLOCK)
                for lt in range(2):
                    cc = slice(lt * half, (lt + 1) * half)
                    if lt == bi // 2:
                        c_loc = lane - DIAG_BLOCK * (bi % 2)
                        in_diag = (lane >= DIAG_BLOCK) if bi % 2 else (lane < DIAG_BLOCK)
                        diag = (jnp.where(c_loc <= r_loc, m0f[rr, cc], 0.0)
                                + jnp.where(c_loc >= r_loc, m0b[rr, cc], 0.0))
                        piece = jnp.where(in_diag, diag, m1[rr, cc])
                    else:
                        piece = m2[rr, cc]
                    a_ref[base + bi * DIAG_BLOCK:base + (bi + 1) * DIAG_BLOCK,
                          base + lt * half:base + (lt + 1) * half] = piece.astype(BF16)
        s = SUPER_BLOCK
        lev = 2
        while s < seq:
            lhs, rhs = lhs_levels[lev], rhs_levels[lev]
            for e0 in range(0, seq, 2 * s):
                ev = slice(e0, e0 + s)
                od = slice(e0 + s, e0 + 2 * s)
                a_ref[od, ev] = _dot_nt(lhs[od], rhs[ev]).astype(BF16)
                a_ref[ev, od] = _dot_nt(lhs[ev], rhs[od]).astype(BF16)
            s *= 2
            lev += 1

        o = _dot(a_ref[...], v)
        if has_state_in:
            qi = jnp.concatenate([scale(q, jnp.exp2(bf)), scale(q, jnp.exp2(sb))], axis=1)
            s0 = jnp.concatenate([s0f_ref[h].astype(BF16), s0b_ref[h].astype(BF16)], axis=0)
            o = o + _dot(qi, s0)
        if has_state_out:
            sf_ref[h] = _dot_tn(scale(k, jnp.exp2(bf[seq - 1:seq, :] - bf)), v)
            sb_ref[h] = _dot_tn(scale(k, jnp.exp2(sb[0:1, :] - sb)), v)

        inv = lax.rsqrt(_dot((o * o).astype(BF16), ones) * (1.0 / HEAD_V) + EPS)
        o = jnp.concatenate([o[:, :half] * inv, o[:, half:] * inv], axis=1) * gn_ref[...]
        p = p_ref[:, vs].astype(F32)
        conv = (jnp.where(rows_v >= 1, pltpu.roll(p, 1, axis=0), 0.0) * cw_ref[0:1, vs] + p * cw_ref[1:2, vs]
                + jnp.where(rows_v < seq - 1, pltpu.roll(p, seq - 1, axis=0), 0.0) * cw_ref[2:3, vs])
        mixes.append(og_ref[:, vs] * o.astype(BF16) + cbg_ref[:, vs] * conv.astype(BF16))

    contrib = _dot(_cat(mixes, axis=1), wo_ref[...])
    ga1 = mod_ref[:, 2 * D_MODEL:3 * D_MODEL]
    if hsteps == 1:
        x1_ref[...] = x_ref[...] + ga1 * contrib
    else:
        hstep = pl.program_id(1)

        @pl.when(hstep == 0)
        def _():
            x1_ref[...] = x_ref[...] + ga1 * contrib

        @pl.when(hstep != 0)
        def _():
            x1_ref[...] += ga1 * contrib


def _mixer(x, mod3, proj, conv_w, gn, wo, states, *, seq, heads, seg, mod_row, has_state_out):
    q, k, v, og, p, cbg, g = proj
    n = x.shape[0]
    nb = n // seq
    hsteps = N_HEADS // heads
    has_state_in = states is not None
    assert seq % SUPER_BLOCK == 0 and seq % seg == 0
    col = lambda b, h: (b, h)
    st_block = (None, None, heads, HEAD_K, HEAD_V)
    st_map = lambda b, h: (b, 0, h, 0, 0)
    in_specs = [pl.BlockSpec((seq, heads * HEAD_K), col),
                pl.BlockSpec((seq, heads * HEAD_K), col),
                pl.BlockSpec((seq, heads * HEAD_V), col),
                pl.BlockSpec((seq, heads * HEAD_K), col),
                pl.BlockSpec((seq, heads * HEAD_K), lambda b, h: (b, hsteps + h)),
                pl.BlockSpec((seq, heads * HEAD_V), col),
                pl.BlockSpec((seq, heads * HEAD_V), col),
                pl.BlockSpec((seq, heads * HEAD_V), col),
                pl.BlockSpec((3, heads * HEAD_V), lambda b, h: (0, h)),
                pl.BlockSpec((1, HEAD_V), lambda b, h: (0, 0)),
                pl.BlockSpec((heads * HEAD_V, D_MODEL), lambda b, h: (h, 0)),
                pl.BlockSpec((seq, D_MODEL), lambda b, h: (b, 0)),
                pl.BlockSpec((None, 1, N_MOD * D_MODEL), lambda b, h: (mod_row(b), 0, 0))]
    args = [q, k, v, g, g, og, p, cbg, conv_w, gn, wo, x, mod3]
    if has_state_in:
        in_specs += [pl.BlockSpec(st_block, st_map)] * 2
        args += list(states)
    out_shape = [jax.ShapeDtypeStruct((n, D_MODEL), F32)]
    out_specs = [pl.BlockSpec((seq, D_MODEL), lambda b, h: (b, 0))]
    if has_state_out:
        out_shape += [jax.ShapeDtypeStruct((nb, 1, N_HEADS, HEAD_K, HEAD_V), F32)] * 2
        out_specs += [pl.BlockSpec(st_block, st_map)] * 2
    return pl.pallas_call(
        functools.partial(_mixer_kernel, seq=seq, heads=heads, seg=seg, has_state_in=has_state_in,
                          has_state_out=has_state_out),
        out_shape=out_shape,
        grid=(nb, hsteps),
        in_specs=in_specs,
        out_specs=out_specs,
        scratch_shapes=[pltpu.VMEM((seq, seq), BF16)],
        compiler_params=_params("arbitrary", "arbitrary"),
        name="mixer",
    )(*args)


def _ffn_kernel(x1_ref, mod_ref, n2_ref, wu_ref, wg_ref, cw_ref, cb_ref, wd_ref, nf_ref, o_ref, acc_ref,
                *, seq, width):
    tm = x1_ref.shape[0]
    mod = mod_ref[...]
    sh2 = mod[:, 3 * D_MODEL:4 * D_MODEL]
    sc2 = mod[:, 4 * D_MODEL:5 * D_MODEL]
    ga2 = mod[:, 5 * D_MODEL:6 * D_MODEL]
    x1 = x1_ref[...]
    xb = (_rms(x1, n2_ref[...]) * (1.0 + sc2) + sh2).astype(BF16)

    rows = lax.broadcasted_iota(jnp.int32, (tm, FFN_TILE), 0)
    col_in_row = rows & (width - 1)
    has_left = col_in_row != 0
    has_right = col_in_row != width - 1
    n_rows = seq // width
    zrow = jnp.zeros((width, FFN_TILE), F32)

    def up_rows(z):
        return jnp.concatenate([zrow, z[:tm - width]], axis=0)

    def down_rows(z):
        return jnp.concatenate([z[width:], zrow], axis=0)

    for f in range(D_FF // FFN_TILE):
        fs = slice(f * FFN_TILE, (f + 1) * FFN_TILE)
        up = _dot(xb, wu_ref[:, fs])
        left = jnp.where(has_left, pltpu.roll(up, 1, axis=0), 0.0)
        right = jnp.where(has_right, pltpu.roll(up, tm - 1, axis=0), 0.0)
        w = lambda i, j: cw_ref[3 * i + j:3 * i + j + 1, fs]
        cv = left * w(1, 0) + up * w(1, 1) + right * w(1, 2) + cb_ref[:, fs]
        if n_rows > 1:
            cv = cv + (up_rows(left) * w(0, 0) + up_rows(up) * w(0, 1) + up_rows(right) * w(0, 2)
                       + down_rows(left) * w(2, 0) + down_rows(up) * w(2, 1) + down_rows(right) * w(2, 2))
        act = (cv * _sigmoid(cv) * _dot(xb, wg_ref[:, fs])).astype(BF16)
        part = _dot(act, wd_ref[fs, :])
        if f == 0:
            acc_ref[...] = part
        else:
            acc_ref[...] += part

    o_ref[...] = _rms(x1 + ga2 * acc_ref[...], nf_ref[...])


def _ffn(x1, mod3, n2g, wu, wg, cw, cb, wd, nfg, *, tm, seq, width, mod_row):
    n = x1.shape[0]
    row = lambda i: (i, 0)
    assert width & (width - 1) == 0 and tm % seq == 0 and (seq == width or tm == seq)
    return pl.pallas_call(
        functools.partial(_ffn_kernel, seq=seq, width=width),
        out_shape=jax.ShapeDtypeStruct((n, D_MODEL), F32),
        grid=(n // tm,),
        in_specs=[pl.BlockSpec((tm, D_MODEL), row),
                  pl.BlockSpec((None, 1, N_MOD * D_MODEL), lambda i: (mod_row(i), 0, 0)),
                  _const_spec(n2g.shape), _const_spec(wu.shape), _const_spec(wg.shape),
                  _const_spec(cw.shape), _const_spec(cb.shape), _const_spec(wd.shape),
                  _const_spec(nfg.shape)],
        out_specs=pl.BlockSpec((tm, D_MODEL), row),
        scratch_shapes=[pltpu.VMEM((tm, D_MODEL), F32)],
        compiler_params=_params("arbitrary"),
        name="ffn",
    )(x1, mod3, n2g, wu, wg, cw, cb, wd, nfg)


def kernel(x_prompt, x_sample, c, state_gla_fwd, state_gla_bwd, c_ctx, w_ada, b_ada, norm1_g, w_in, w_gk_f,
           b_gk_f, w_gk_b, b_gk_b, gla_norm_g, conv_mix_w, w_out, norm2_g, ffn_w_up, ffn_w_gate,
           ffn_conv_w, ffn_conv_b, ffn_w_down, normf_g):
    bp, lp, d = x_prompt.shape
    bs, ls, _ = x_sample.shape
    assert w_ada.shape[0] == 1 and d == D_MODEL and bs + 1 <= MOD_ROWS

    cc = jnp.zeros((MOD_ROWS, d), F32).at[0].set(c_ctx).at[1:1 + bs].set(c)
    mod3 = _ada(cc, w_ada[0], b_ada[0][None, :]).reshape(MOD_ROWS, 1, N_MOD * d)

    cols, off = [], 0
    for wdt in IN_SPLITS:
        cols.append(w_in[0][:, off:off + wdt])
        off += wdt
    wq, wk, wv, wgo, wcf, wcb_, wcb, wcc, wcx, wga, wgb = cols
    wcode = jnp.zeros((d, MXU_LANES_V7X), F32).at[:, :GATE_RANK].set(wcf).at[:, GATE_RANK:2 * GATE_RANK].set(wcb_)
    wgk = (jnp.zeros((MXU_LANES_V7X, 2 * KEY_DIM), F32)
           .at[:GATE_RANK, :KEY_DIM].set(w_gk_f[0]).at[GATE_RANK:2 * GATE_RANK, KEY_DIM:].set(w_gk_b[0]))
    bgk = jnp.concatenate([b_gk_f[0], b_gk_b[0]])[None, :]
    ws = [w.astype(BF16) for w in (wq, wk, wv, wgo, wcode, wcb, wcc, wcx, wga, wgb)]
    wgk = wgk.astype(BF16)
    wo = w_out[0].astype(BF16)
    wu, wg, wd = ffn_w_up[0].astype(BF16), ffn_w_gate[0].astype(BF16), ffn_w_down[0].astype(BF16)
    cw9 = ffn_conv_w[0].reshape(9, D_FF)
    cb = ffn_conv_b[0][None, :]
    n1g, n2g, nfg, gn = norm1_g[0][None, :], norm2_g[0][None, :], normf_g[None, :], gla_norm_g[0][None, :]

    def group(x3, *, seq, width, heads, states, mod_of_seq):
        x = x3.reshape(-1, d)
        seg = min(seq, INPROJ_ROWS)
        proj = _inproj(x, mod3, n1g, ws, wgk, bgk, tm=INPROJ_ROWS, seg=seg,
                       mod_row=lambda i: mod_of_seq((i * INPROJ_ROWS) // seq))
        outs = _mixer(x, mod3, proj, conv_mix_w[0], gn, wo, states, seq=seq, heads=heads, seg=seg,
                      mod_row=mod_of_seq, has_state_out=states is None)
        y = _ffn(outs[0], mod3, n2g, wu, wg, cw9, cb, wd, nfg, tm=FFN_ROWS, seq=seq, width=width,
                 mod_row=lambda i: mod_of_seq((i * FFN_ROWS) // seq))
        return y.reshape(x3.shape), outs[1:]

    y_prompt, (new_f, new_b) = group(x_prompt, seq=lp, width=lp, heads=N_HEADS, states=None,
                                     mod_of_seq=lambda b: 0)
    y_sample, _ = group(x_sample, seq=ls, width=GRID_W, heads=1,
                        states=(state_gla_fwd, state_gla_bwd), mod_of_seq=lambda b: 1 + b)
    return y_prompt, y_sample, new_f, new_b
```

```python
import functools

import jax
import jax.numpy as jnp
from jax import lax
from jax.experimental import pallas as pl
from jax.experimental.pallas import tpu as pltpu

D_MODEL = 1024
N_HEADS = 4
HEAD_K = 128
HEAD_V = 256
KEY_DIM = N_HEADS * HEAD_K
VAL_DIM = N_HEADS * HEAD_V
GATE_RANK = 16
GATE_NORMALIZER = 16.0
LOG2_E = 1.4426950408889634
CONV_DIM = D_MODEL
D_FF = 2816
N_MOD = 6
EPS = 1e-6
GRID_W = 64
IN_SPLITS = (KEY_DIM, KEY_DIM, VAL_DIM, VAL_DIM, GATE_RANK, GATE_RANK,
             CONV_DIM, CONV_DIM, CONV_DIM, VAL_DIM, CONV_DIM)

DIAG_BLOCK = 64
SUPER_BLOCK = 256
MXU_LANES_V7X = 128
FFN_TILE = 256
FFN_ROWS = 1024
INPROJ_ROWS = 512
ADA_TILE = 1024
MOD_ROWS = 8
VMEM_LIMIT_BYTES = 56 * 1024 * 1024

F32 = jnp.float32
BF16 = jnp.bfloat16


def _dot(a, b):
    return jnp.dot(a, b, preferred_element_type=F32)


def _dot_nt(a, b):
    return lax.dot_general(a, b, (((1,), (1,)), ((), ())), preferred_element_type=F32)


def _dot_tn(a, b):
    return lax.dot_general(a, b, (((0,), (0,)), ((), ())), preferred_element_type=F32)


def _sigmoid(x):
    return 1.0 / (1.0 + jnp.exp(-x))


def _rms(x, g):
    return x * lax.rsqrt(jnp.mean(x * x, axis=-1, keepdims=True) + EPS) * g


def _const_spec(shape):
    nd = len(shape)
    return pl.BlockSpec(shape, lambda *_: (0,) * nd, pipeline_mode=pl.Buffered(1))


def _params(*sem):
    return pltpu.CompilerParams(dimension_semantics=sem, vmem_limit_bytes=VMEM_LIMIT_BYTES)


def _ada_kernel(c_ref, w_ref, b_ref, o_ref):
    c = c_ref[...]
    s = (c * _sigmoid(c)).astype(BF16)
    o_ref[...] = _dot(s, w_ref[...].astype(BF16)) + b_ref[...]


def _ada(cc, w_ada, b_ada):
    n = w_ada.shape[1]
    return pl.pallas_call(
        _ada_kernel,
        out_shape=jax.ShapeDtypeStruct((MOD_ROWS, n), F32),
        grid=(n // ADA_TILE,),
        in_specs=[pl.BlockSpec((MOD_ROWS, D_MODEL), lambda j: (0, 0)),
                  pl.BlockSpec((D_MODEL, ADA_TILE), lambda j: (0, j)),
                  pl.BlockSpec((1, ADA_TILE), lambda j: (0, j))],
        out_specs=pl.BlockSpec((MOD_ROWS, ADA_TILE), lambda j: (0, j)),
        compiler_params=_params("arbitrary"),
        name="ada",
    )(cc, w_ada, b_ada)


def _prefix_sum(x, pos, seg):
    s = 1
    while s < seg:
        x = x + jnp.where(pos >= s, pltpu.roll(x, s, axis=0), 0.0)
        s *= 2
    return x


def _suffix_sum(x, pos, seg):
    n, s = x.shape[0], 1
    while s < seg:
        x = x + jnp.where(pos < seg - s, pltpu.roll(x, n - s, axis=0), 0.0)
        s *= 2
    return x


def _inproj_kernel(x_ref, mod_ref, n1_ref, wh_ref, wt_ref, wcode_ref, wgk_ref, bgk_ref,
                   q_ref, k_ref, v_ref, og_ref, p_ref, cbg_ref, g_ref, *, seg):
    mod = mod_ref[...]
    sh1 = mod[:, 0:D_MODEL]
    sc1 = mod[:, D_MODEL:2 * D_MODEL]
    xb = (_rms(x_ref[...], n1_ref[...]) * (1.0 + sc1) + sh1).astype(BF16)

    head = lambda off, width: _dot(xb, wh_ref[:, off:off + width])
    tail = lambda i: _dot(xb, wt_ref[:, i * CONV_DIM:(i + 1) * CONV_DIM])
    q_ref[...] = (head(0, KEY_DIM) * (HEAD_K ** -0.5)).astype(BF16)
    k_ref[...] = head(KEY_DIM, KEY_DIM).astype(BF16)
    v_ref[...] = head(2 * KEY_DIM, VAL_DIM).astype(BF16)
    g_out = head(2 * KEY_DIM + VAL_DIM, VAL_DIM)
    og_ref[...] = (g_out * _sigmoid(g_out) * _sigmoid(tail(3))).astype(BF16)
    p_ref[...] = (tail(1) * tail(2)).astype(BF16)
    cbg_ref[...] = (_sigmoid(tail(4)) * tail(0)).astype(BF16)
    z = _dot(_dot(xb, wcode_ref[...]).astype(BF16), wgk_ref[...]) + bgk_ref[...]
    g = (jnp.minimum(z, 0.0) - jnp.log(1.0 + jnp.exp(-jnp.abs(z)))) * (LOG2_E / GATE_NORMALIZER)
    pos = lax.broadcasted_iota(jnp.int32, (g.shape[0], KEY_DIM), 0) & (seg - 1)
    g_ref[:, :KEY_DIM] = _prefix_sum(g[:, :KEY_DIM], pos, seg)
    g_ref[:, KEY_DIM:] = _suffix_sum(g[:, KEY_DIM:], pos, seg)


def _inproj(x, mod3, n1g, ws, wgk, bgk, *, tm, seg, mod_row):
    n = x.shape[0]
    assert seg & (seg - 1) == 0 and tm % seg == 0
    row = lambda i: (i, 0)
    wide = lambda w, dt: (jax.ShapeDtypeStruct((n, w), dt), pl.BlockSpec((tm, w), row))
    outs = [wide(KEY_DIM, BF16), wide(KEY_DIM, BF16), wide(VAL_DIM, BF16), wide(VAL_DIM, BF16),
            wide(CONV_DIM, BF16), wide(CONV_DIM, BF16), wide(2 * KEY_DIM, F32)]
    return pl.pallas_call(
        functools.partial(_inproj_kernel, seg=seg),
        out_shape=[o[0] for o in outs],
        grid=(n // tm,),
        in_specs=[pl.BlockSpec((tm, D_MODEL), row),
                  pl.BlockSpec((None, 1, N_MOD * D_MODEL), lambda i: (mod_row(i), 0, 0)),
                  _const_spec(n1g.shape)]
                 + [_const_spec(w.shape) for w in ws]
                 + [_const_spec(wgk.shape), _const_spec(bgk.shape)],
        out_specs=[o[1] for o in outs],
        compiler_params=_params("arbitrary"),
        name="in_proj",
    )(x, mod3, n1g, *ws, wgk, bgk)


def _cat(parts, axis=0):
    return parts[0] if len(parts) == 1 else jnp.concatenate(parts, axis=axis)


def _join_prefix(b, seg):
    parts = [b[:seg]]
    for c in range(1, b.shape[0] // seg):
        parts.append(b[c * seg:(c + 1) * seg] + parts[-1][seg - 1:seg, :])
    return _cat(parts)


def _join_suffix(b, seg):
    n = b.shape[0] // seg
    parts = [b[(n - 1) * seg:]]
    for c in range(n - 2, -1, -1):
        parts.insert(0, b[c * seg:(c + 1) * seg] + parts[0][0:1, :])
    return _cat(parts)


def _diag_args(b, off):
    return _cat([b[c:c + DIAG_BLOCK] - b[c + off:c + off + 1, :] for c in range(0, b.shape[0], DIAG_BLOCK)])


def _pair_args(bf, sb, s):
    lhs, rhs = [], []
    for e0 in range(0, bf.shape[0], 2 * s):
        o0 = e0 + s
        rf = bf[o0 - 1:o0, :]
        rb = sb[o0:o0 + 1, :]
        lhs += [sb[e0:o0] - rb, bf[o0:o0 + s] - rf]
        rhs += [rf - bf[e0:o0], rb - sb[o0:o0 + s]]
    return _cat(lhs), _cat(rhs)


def _mixer_kernel(*refs, seq, heads, seg, has_state_in, has_state_out):
    (q_ref, k_ref, v_ref, gf_ref, gb_ref, og_ref, p_ref, cbg_ref, cw_ref, gn_ref, wo_ref, x_ref,
     mod_ref) = refs[:13]
    refs = refs[13:]
    if has_state_in:
        s0f_ref, s0b_ref = refs[:2]
        refs = refs[2:]
    x1_ref = refs[0]
    refs = refs[1:]
    if has_state_out:
        sf_ref, sb_ref = refs[:2]
        refs = refs[2:]
    a_ref = refs[0]
    hsteps = N_HEADS // heads
    half = SUPER_BLOCK // 2

    bf_all = _join_prefix(gf_ref[...], seg)
    sb_all = _join_suffix(gb_ref[...], seg)

    rows_v = lax.broadcasted_iota(jnp.int32, (seq, HEAD_V), 0)
    r_loc = lax.broadcasted_iota(jnp.int32, (DIAG_BLOCK, half), 0)
    lane = lax.broadcasted_iota(jnp.int32, (DIAG_BLOCK, half), 1)
    ones = jnp.ones((HEAD_V, half), BF16)
    scale = lambda t, e: t * e.astype(BF16)

    mixes = []
    for h in range(heads):
        ks = slice(h * HEAD_K, (h + 1) * HEAD_K)
        vs = slice(h * HEAD_V, (h + 1) * HEAD_V)
        q = q_ref[:, ks]
        k = k_ref[:, ks]
        v = v_ref[:, vs]
        bf = bf_all[:, ks]
        sb = sb_all[:, ks]

        e0f = jnp.exp2(_diag_args(bf, DIAG_BLOCK // 2))
        e0b = jnp.exp2(_diag_args(sb, DIAG_BLOCK // 2 - 1))
        q0f, k0f = scale(q, e0f), scale(k, 1.0 / e0f)
        q0b, k0b = scale(q, e0b), scale(k, 1.0 / e0b)

        lhs_levels, rhs_levels = [], []
        s = DIAG_BLOCK
        while s < seq:
            la, ra = _pair_args(bf, sb, s)
            lhs_levels.append(scale(q, jnp.exp2(la)))
            rhs_levels.append(scale(k, jnp.exp2(ra)))
            s *= 2

        for base in range(0, seq, SUPER_BLOCK):
            rs = slice(base, base + SUPER_BLOCK)
            m0f = _dot_nt(q0f[rs], k0f[rs])
            m0b = _dot_nt(q0b[rs], k0b[rs])
            m1 = _dot_nt(lhs_levels[0][rs], rhs_levels[0][rs])
            m2 = _dot_nt(lhs_levels[1][rs], rhs_levels[1][rs])
            for bi in range(SUPER_BLOCK // DIAG_BLOCK):
                rr = slice(bi * DIAG_BLOCK, (bi + 1) * DIAG_BLOCK)
                for lt in range(2):
                    cc = slice(lt * half, (lt + 1) * half)
                    if lt == bi // 2:
                        c_loc = lane - DIAG_BLOCK * (bi % 2)
                        in_diag = (lane >= DIAG_BLOCK) if bi % 2 else (lane < DIAG_BLOCK)
                        diag = (jnp.where(c_loc <= r_loc, m0f[rr, cc], 0.0)
                                + jnp.where(c_loc >= r_loc, m0b[rr, cc], 0.0))
                        piece = jnp.where(in_diag, diag, m1[rr, cc])
                    else:
                        piece = m2[rr, cc]
                    a_ref[base + bi * DIAG_BLOCK:base + (bi + 1) * DIAG_BLOCK,
                          base + lt * half:base + (lt + 1) * half] = piece.astype(BF16)
        s = SUPER_BLOCK
        lev = 2
        while s < seq:
            lhs, rhs = lhs_levels[lev], rhs_levels[lev]
            for e0 in range(0, seq, 2 * s):
                ev = slice(e0, e0 + s)
                od = slice(e0 + s, e0 + 2 * s)
                a_ref[od, ev] = _dot_nt(lhs[od], rhs[ev]).astype(BF16)
                a_ref[ev, od] = _dot_nt(lhs[ev], rhs[od]).astype(BF16)
            s *= 2
            lev += 1

        o = _dot(a_ref[...], v)
        if has_state_in:
            qi = jnp.concatenate([scale(q, jnp.exp2(bf)), scale(q, jnp.exp2(sb))], axis=1)
            s0 = jnp.concatenate([s0f_ref[h].astype(BF16), s0b_ref[h].astype(BF16)], axis=0)
            o = o + _dot(qi, s0)
        if has_state_out:
            sf_ref[h] = _dot_tn(scale(k, jnp.exp2(bf[seq - 1:seq, :] - bf)), v)
            sb_ref[h] = _dot_tn(scale(k, jnp.exp2(sb[0:1, :] - sb)), v)

        inv = lax.rsqrt(_dot((o * o).astype(BF16), ones) * (1.0 / HEAD_V) + EPS)
        o = jnp.concatenate([o[:, :half] * inv, o[:, half:] * inv], axis=1) * gn_ref[...]
        p = p_ref[:, vs].astype(F32)
        conv = (jnp.where(rows_v >= 1, pltpu.roll(p, 1, axis=0), 0.0) * cw_ref[0:1, vs] + p * cw_ref[1:2, vs]
                + jnp.where(rows_v < seq - 1, pltpu.roll(p, seq - 1, axis=0), 0.0) * cw_ref[2:3, vs])
        mixes.append(og_ref[:, vs] * o.astype(BF16) + cbg_ref[:, vs] * conv.astype(BF16))

    contrib = _dot(_cat(mixes, axis=1), wo_ref[...])
    ga1 = mod_ref[:, 2 * D_MODEL:3 * D_MODEL]
    if hsteps == 1:
        x1_ref[...] = x_ref[...] + ga1 * contrib
    else:
        hstep = pl.program_id(1)

        @pl.when(hstep == 0)
        def _():
            x1_ref[...] = x_ref[...] + ga1 * contrib

        @pl.when(hstep != 0)
        def _():
            x1_ref[...] += ga1 * contrib


def _mixer(x, mod3, proj, conv_w, gn, wo, states, *, seq, heads, seg, mod_row, has_state_out):
    q, k, v, og, p, cbg, g = proj
    n = x.shape[0]
    nb = n // seq
    hsteps = N_HEADS // heads
    has_state_in = states is not None
    assert seq % SUPER_BLOCK == 0 and seq % seg == 0
    col = lambda b, h: (b, h)
    st_block = (None, None, heads, HEAD_K, HEAD_V)
    st_map = lambda b, h: (b, 0, h, 0, 0)
    in_specs = [pl.BlockSpec((seq, heads * HEAD_K), col),
                pl.BlockSpec((seq, heads * HEAD_K), col),
                pl.BlockSpec((seq, heads * HEAD_V), col),
                pl.BlockSpec((seq, heads * HEAD_K), col),
                pl.BlockSpec((seq, heads * HEAD_K), lambda b, h: (b, hsteps + h)),
                pl.BlockSpec((seq, heads * HEAD_V), col),
                pl.BlockSpec((seq, heads * HEAD_V), col),
                pl.BlockSpec((seq, heads * HEAD_V), col),
                pl.BlockSpec((3, heads * HEAD_V), lambda b, h: (0, h)),
                pl.BlockSpec((1, HEAD_V), lambda b, h: (0, 0)),
                pl.BlockSpec((heads * HEAD_V, D_MODEL), lambda b, h: (h, 0)),
                pl.BlockSpec((seq, D_MODEL), lambda b, h: (b, 0)),
                pl.BlockSpec((None, 1, N_MOD * D_MODEL), lambda b, h: (mod_row(b), 0, 0))]
    args = [q, k, v, g, g, og, p, cbg, conv_w, gn, wo, x, mod3]
    if has_state_in:
        in_specs += [pl.BlockSpec(st_block, st_map)] * 2
        args += list(states)
    out_shape = [jax.ShapeDtypeStruct((n, D_MODEL), F32)]
    out_specs = [pl.BlockSpec((seq, D_MODEL), lambda b, h: (b, 0))]
    if has_state_out:
        out_shape += [jax.ShapeDtypeStruct((nb, 1, N_HEADS, HEAD_K, HEAD_V), F32)] * 2
        out_specs += [pl.BlockSpec(st_block, st_map)] * 2
    return pl.pallas_call(
        functools.partial(_mixer_kernel, seq=seq, heads=heads, seg=seg, has_state_in=has_state_in,
                          has_state_out=has_state_out),
        out_shape=out_shape,
        grid=(nb, hsteps),
        in_specs=in_specs,
        out_specs=out_specs,
        scratch_shapes=[pltpu.VMEM((seq, seq), BF16)],
        compiler_params=_params("arbitrary", "arbitrary"),
        name="mixer",
    )(*args)


def _ffn_kernel(x1_ref, mod_ref, n2_ref, wu_ref, wg_ref, cw_ref, cb_ref, wd_ref, nf_ref, o_ref, acc_ref,
                *, seq, width):
    tm = x1_ref.shape[0]
    mod = mod_ref[...]
    sh2 = mod[:, 3 * D_MODEL:4 * D_MODEL]
    sc2 = mod[:, 4 * D_MODEL:5 * D_MODEL]
    ga2 = mod[:, 5 * D_MODEL:6 * D_MODEL]
    x1 = x1_ref[...]
    xb = (_rms(x1, n2_ref[...]) * (1.0 + sc2) + sh2).astype(BF16)

    rows = lax.broadcasted_iota(jnp.int32, (tm, FFN_TILE), 0)
    col_in_row = rows & (width - 1)
    has_left = col_in_row != 0
    has_right = col_in_row != width - 1
    n_rows = seq // width
    zrow = jnp.zeros((width, FFN_TILE), F32)

    def up_rows(z):
        return jnp.concatenate([zrow, z[:tm - width]], axis=0)

    def down_rows(z):
        return jnp.concatenate([z[width:], zrow], axis=0)

    n_tiles = D_FF // FFN_TILE

    def up_and_gate(f):
        fs = slice(f * FFN_TILE, (f + 1) * FFN_TILE)
        return _dot(xb, wu_ref[:, fs]), _dot(xb, wg_ref[:, fs])

    ahead = up_and_gate(0)
    for f in range(n_tiles):
        fs = slice(f * FFN_TILE, (f + 1) * FFN_TILE)
        up, gate = ahead
        if f + 1 < n_tiles:
            ahead = up_and_gate(f + 1)
        left = jnp.where(has_left, pltpu.roll(up, 1, axis=0), 0.0)
        right = jnp.where(has_right, pltpu.roll(up, tm - 1, axis=0), 0.0)
        w = lambda i, j: cw_ref[3 * i + j:3 * i + j + 1, fs]
        cv = left * w(1, 0) + up * w(1, 1) + right * w(1, 2) + cb_ref[:, fs]
        if n_rows > 1:
            cv = cv + (up_rows(left) * w(0, 0) + up_rows(up) * w(0, 1) + up_rows(right) * w(0, 2)
                       + down_rows(left) * w(2, 0) + down_rows(up) * w(2, 1) + down_rows(right) * w(2, 2))
        act = (cv * _sigmoid(cv) * gate).astype(BF16)
        part = _dot(act, wd_ref[fs, :])
        if f == 0:
            acc_ref[...] = part
        else:
            acc_ref[...] += part

    o_ref[...] = _rms(x1 + ga2 * acc_ref[...], nf_ref[...])


def _ffn(x1, mod3, n2g, wu, wg, cw, cb, wd, nfg, *, tm, seq, width, mod_row):
    n = x1.shape[0]
    row = lambda i: (i, 0)
    assert width & (width - 1) == 0 and tm % seq == 0 and (seq == width or tm == seq)
    return pl.pallas_call(
        functools.partial(_ffn_kernel, seq=seq, width=width),
        out_shape=jax.ShapeDtypeStruct((n, D_MODEL), F32),
        grid=(n // tm,),
        in_specs=[pl.BlockSpec((tm, D_MODEL), row),
                  pl.BlockSpec((None, 1, N_MOD * D_MODEL), lambda i: (mod_row(i), 0, 0)),
                  _const_spec(n2g.shape), _const_spec(wu.shape), _const_spec(wg.shape),
                  _const_spec(cw.shape), _const_spec(cb.shape), _const_spec(wd.shape),
                  _const_spec(nfg.shape)],
        out_specs=pl.BlockSpec((tm, D_MODEL), row),
        scratch_shapes=[pltpu.VMEM((tm, D_MODEL), F32)],
        compiler_params=_params("arbitrary"),
        name="ffn",
    )(x1, mod3, n2g, wu, wg, cw, cb, wd, nfg)


def kernel(x_prompt, x_sample, c, state_gla_fwd, state_gla_bwd, c_ctx, w_ada, b_ada, norm1_g, w_in, w_gk_f,
           b_gk_f, w_gk_b, b_gk_b, gla_norm_g, conv_mix_w, w_out, norm2_g, ffn_w_up, ffn_w_gate,
           ffn_conv_w, ffn_conv_b, ffn_w_down, normf_g):
    bp, lp, d = x_prompt.shape
    bs, ls, _ = x_sample.shape
    assert w_ada.shape[0] == 1 and d == D_MODEL and bs + 1 <= MOD_ROWS

    cc = jnp.zeros((MOD_ROWS, d), F32).at[0].set(c_ctx).at[1:1 + bs].set(c)
    mod3 = _ada(cc, w_ada[0], b_ada[0][None, :]).reshape(MOD_ROWS, 1, N_MOD * d)

    assert IN_SPLITS[4:6] == (GATE_RANK, GATE_RANK)
    code_off = sum(IN_SPLITS[:4])
    tail_off = code_off + 2 * GATE_RANK
    wcode = jnp.zeros((d, MXU_LANES_V7X), F32).at[:, :2 * GATE_RANK].set(w_in[0][:, code_off:tail_off])
    wgk = (jnp.zeros((MXU_LANES_V7X, 2 * KEY_DIM), F32)
           .at[:GATE_RANK, :KEY_DIM].set(w_gk_f[0]).at[GATE_RANK:2 * GATE_RANK, KEY_DIM:].set(w_gk_b[0]))
    bgk = jnp.concatenate([b_gk_f[0], b_gk_b[0]])[None, :]
    ws = [w_in[0][:, :code_off].astype(BF16), w_in[0][:, tail_off:].astype(BF16), wcode.astype(BF16)]
    wgk = wgk.astype(BF16)
    wo = w_out[0].astype(BF16)
    wu, wg, wd = ffn_w_up[0].astype(BF16), ffn_w_gate[0].astype(BF16), ffn_w_down[0].astype(BF16)
    cw9 = ffn_conv_w[0].reshape(9, D_FF)
    cb = ffn_conv_b[0][None, :]
    n1g, n2g, nfg, gn = norm1_g[0][None, :], norm2_g[0][None, :], normf_g[None, :], gla_norm_g[0][None, :]

    def group(x3, *, seq, width, heads, states, mod_of_seq):
        x = x3.reshape(-1, d)
        seg = min(seq, INPROJ_ROWS)
        proj = _inproj(x, mod3, n1g, ws, wgk, bgk, tm=INPROJ_ROWS, seg=seg,
                       mod_row=lambda i: mod_of_seq((i * INPROJ_ROWS) // seq))
        outs = _mixer(x, mod3, proj, conv_mix_w[0], gn, wo, states, seq=seq, heads=heads, seg=seg,
                      mod_row=mod_of_seq, has_state_out=states is None)
        y = _ffn(outs[0], mod3, n2g, wu, wg, cw9, cb, wd, nfg, tm=FFN_ROWS, seq=seq, width=width,
                 mod_row=lambda i: mod_of_seq((i * FFN_ROWS) // seq))
        return y.reshape(x3.shape), outs[1:]

    y_prompt, (new_f, new_b) = group(x_prompt, seq=lp, width=lp, heads=N_HEADS, states=None,
                                     mod_of_seq=lambda b: 0)
    y_sample, _ = group(x_sample, seq=ls, width=GRID_W, heads=1,
                        states=(state_gla_fwd, state_gla_bwd), mod_of_seq=lambda b: 1 + b)
    return y_prompt, y_sample, new_f, new_b
```

```python
import functools

import jax
import jax.numpy as jnp
from jax import lax
from jax.experimental import pallas as pl
from jax.experimental.pallas import tpu as pltpu

D_MODEL = 1024
N_HEADS = 4
HEAD_K = 128
HEAD_V = 256
KEY_DIM = N_HEADS * HEAD_K
VAL_DIM = N_HEADS * HEAD_V
GATE_RANK = 16
GATE_NORMALIZER = 16.0
LOG2_E = 1.4426950408889634
CONV_DIM = D_MODEL
D_FF = 2816
N_MOD = 6
EPS = 1e-6
GRID_W = 64
IN_SPLITS = (KEY_DIM, KEY_DIM, VAL_DIM, VAL_DIM, GATE_RANK, GATE_RANK,
             CONV_DIM, CONV_DIM, CONV_DIM, VAL_DIM, CONV_DIM)

DIAG_BLOCK = 64
SUPER_BLOCK = 256
MXU_LANES_V7X = 128
FFN_TILE = 256
FFN_ROWS = 1024
INPROJ_ROWS = 512
ADA_TILE = 1024
MOD_ROWS = 8
VMEM_LIMIT_BYTES = 56 * 1024 * 1024

F32 = jnp.float32
BF16 = jnp.bfloat16


def _dot(a, b):
    return jnp.dot(a, b, preferred_element_type=F32)


def _dot_nt(a, b):
    return lax.dot_general(a, b, (((1,), (1,)), ((), ())), preferred_element_type=F32)


def _dot_tn(a, b):
    return lax.dot_general(a, b, (((0,), (0,)), ((), ())), preferred_element_type=F32)


def _sigmoid(x):
    return 1.0 / (1.0 + jnp.exp(-x))


def _rms(x, g):
    return x * lax.rsqrt(jnp.mean(x * x, axis=-1, keepdims=True) + EPS) * g


def _const_spec(shape):
    nd = len(shape)
    return pl.BlockSpec(shape, lambda *_: (0,) * nd, pipeline_mode=pl.Buffered(1))


def _params(*sem):
    return pltpu.CompilerParams(dimension_semantics=sem, vmem_limit_bytes=VMEM_LIMIT_BYTES)


def _ada_kernel(c_ref, w_ref, b_ref, o_ref):
    c = c_ref[...]
    s = (c * _sigmoid(c)).astype(BF16)
    o_ref[...] = _dot(s, w_ref[...].astype(BF16)) + b_ref[...]


def _ada(cc, w_ada, b_ada):
    n = w_ada.shape[1]
    return pl.pallas_call(
        _ada_kernel,
        out_shape=jax.ShapeDtypeStruct((MOD_ROWS, n), F32),
        grid=(n // ADA_TILE,),
        in_specs=[pl.BlockSpec((MOD_ROWS, D_MODEL), lambda j: (0, 0)),
                  pl.BlockSpec((D_MODEL, ADA_TILE), lambda j: (0, j)),
                  pl.BlockSpec((1, ADA_TILE), lambda j: (0, j))],
        out_specs=pl.BlockSpec((MOD_ROWS, ADA_TILE), lambda j: (0, j)),
        compiler_params=_params("arbitrary"),
        name="ada",
    )(cc, w_ada, b_ada)


def _prefix_sum(x, pos, seg):
    s = 1
    while s < seg:
        x = x + jnp.where(pos >= s, pltpu.roll(x, s, axis=0), 0.0)
        s *= 2
    return x


def _suffix_sum(x, pos, seg):
    n, s = x.shape[0], 1
    while s < seg:
        x = x + jnp.where(pos < seg - s, pltpu.roll(x, n - s, axis=0), 0.0)
        s *= 2
    return x


W_BLOCK = 512
HEAD_COLS = 2 * KEY_DIM + 2 * VAL_DIM
TAIL_OFF = HEAD_COLS + 2 * GATE_RANK
TAIL_SHIFT = TAIL_OFF % MXU_LANES_V7X
N_HEAD_BLOCKS = HEAD_COLS // W_BLOCK
N_W_BLOCKS = N_HEAD_BLOCKS + 5 * CONV_DIM // W_BLOCK


def _inproj_kernel(xa_ref, xb_ref, mod_ref, n1_ref, wa_ref, wb_ref, wcode_ref, wgk_ref, bgk_ref,
                   q_ref, k_ref, v_ref, og_ref, p_ref, cbg_ref, g_ref, w_scr, *, tiles_a, seg):
    step = pl.program_id(0)

    @pl.when(step < N_HEAD_BLOCKS)
    def _():
        w_scr[step] = wa_ref[...].astype(BF16)

    @pl.when(jnp.logical_and(step >= N_HEAD_BLOCKS, step < N_W_BLOCKS))
    def _():
        wide = jnp.concatenate([wa_ref[...], wb_ref[...]], axis=1)
        w_scr[step] = pltpu.roll(wide, wide.shape[1] - TAIL_SHIFT, axis=1)[:, :W_BLOCK].astype(BF16)

    @pl.when(step >= N_W_BLOCKS)
    def _():
        first_group = step - N_W_BLOCKS < tiles_a
        mod = mod_ref[...]
        sh1 = mod[:, 0:D_MODEL]
        sc1 = mod[:, D_MODEL:2 * D_MODEL]
        x = jnp.where(first_group, xa_ref[...], xb_ref[...])
        xn = (_rms(x, n1_ref[...]) * (1.0 + sc1) + sh1).astype(BF16)
        halves = range(CONV_DIM // W_BLOCK)
        head = lambda blk: _dot(xn, w_scr[blk])
        tail = lambda grp, j: _dot(xn, w_scr[N_HEAD_BLOCKS + grp * len(halves) + j])
        cols = lambda j: slice(j * W_BLOCK, (j + 1) * W_BLOCK)

        q_ref[...] = (head(0) * (HEAD_K ** -0.5)).astype(BF16)
        k_ref[...] = head(1).astype(BF16)
        for j in halves:
            v_ref[:, cols(j)] = head(2 + j).astype(BF16)
            g_out = head(2 + len(halves) + j)
            og_ref[:, cols(j)] = (g_out * _sigmoid(g_out) * _sigmoid(tail(3, j))).astype(BF16)
            p_ref[:, cols(j)] = (tail(1, j) * tail(2, j)).astype(BF16)
            cbg_ref[:, cols(j)] = (_sigmoid(tail(4, j)) * tail(0, j)).astype(BF16)
        z = _dot(_dot(xn, wcode_ref[...]).astype(BF16), wgk_ref[...]) + bgk_ref[...]
        g = (jnp.minimum(z, 0.0) - jnp.log(1.0 + jnp.exp(-jnp.abs(z)))) * (LOG2_E / GATE_NORMALIZER)
        pos = lax.broadcasted_iota(jnp.int32, (g.shape[0], KEY_DIM), 0) & (seg - 1)
        g_ref[:, :KEY_DIM] = _prefix_sum(g[:, :KEY_DIM], pos, seg)
        g_ref[:, KEY_DIM:] = _suffix_sum(g[:, KEY_DIM:], pos, seg)


def _inproj(xa, xb, mod3, n1g, w_in, wcode, wgk, bgk, *, tm, seg, mod_row):
    tiles_a, tiles_b = xa.shape[0] // tm, xb.shape[0] // tm
    n = xa.shape[0] + xb.shape[0]
    assert seg & (seg - 1) == 0 and tm % seg == 0
    assert KEY_DIM == W_BLOCK and HEAD_COLS % W_BLOCK == 0 and W_BLOCK % MXU_LANES_V7X == 0
    tile = lambda i: jnp.maximum(i - N_W_BLOCKS, 0)
    row = lambda i: (tile(i), 0)
    lanes_per_block = W_BLOCK // MXU_LANES_V7X
    wide = lambda w, dt: (jax.ShapeDtypeStruct((n, w), dt), pl.BlockSpec((tm, w), row))
    outs = [wide(KEY_DIM, BF16), wide(KEY_DIM, BF16), wide(VAL_DIM, BF16), wide(VAL_DIM, BF16),
            wide(CONV_DIM, BF16), wide(CONV_DIM, BF16), wide(2 * KEY_DIM, F32)]
    return pl.pallas_call(
        functools.partial(_inproj_kernel, tiles_a=tiles_a, seg=seg),
        out_shape=[o[0] for o in outs],
        grid=(N_W_BLOCKS + tiles_a + tiles_b,),
        in_specs=[pl.BlockSpec((tm, D_MODEL), lambda i: (jnp.minimum(tile(i), tiles_a - 1), 0)),
                  pl.BlockSpec((tm, D_MODEL), lambda i: (jnp.maximum(tile(i) - tiles_a, 0), 0)),
                  pl.BlockSpec((None, 1, N_MOD * D_MODEL), lambda i: (mod_row(tile(i)), 0, 0)),
                  _const_spec(n1g.shape),
                  pl.BlockSpec((D_MODEL, W_BLOCK), lambda i: (0, jnp.minimum(i, N_W_BLOCKS - 1))),
                  pl.BlockSpec((D_MODEL, MXU_LANES_V7X),
                               lambda i: (0, (jnp.clip(i, N_HEAD_BLOCKS, N_W_BLOCKS - 1) + 1) * lanes_per_block)),
                  _const_spec(wcode.shape), _const_spec(wgk.shape), _const_spec(bgk.shape)],
        out_specs=[o[1] for o in outs],
        scratch_shapes=[pltpu.VMEM((N_W_BLOCKS, D_MODEL, W_BLOCK), BF16)],
        compiler_params=_params("arbitrary"),
        name="in_proj",
    )(xa, xb, mod3, n1g, w_in, w_in, wcode, wgk, bgk)


def _cat(parts, axis=0):
    return parts[0] if len(parts) == 1 else jnp.concatenate(parts, axis=axis)


def _join_prefix(b, seg):
    parts = [b[:seg]]
    for c in range(1, b.shape[0] // seg):
        parts.append(b[c * seg:(c + 1) * seg] + parts[-1][seg - 1:seg, :])
    return _cat(parts)


def _join_suffix(b, seg):
    n = b.shape[0] // seg
    parts = [b[(n - 1) * seg:]]
    for c in range(n - 2, -1, -1):
        parts.insert(0, b[c * seg:(c + 1) * seg] + parts[0][0:1, :])
    return _cat(parts)


def _diag_args(b, off):
    return _cat([b[c:c + DIAG_BLOCK] - b[c + off:c + off + 1, :] for c in range(0, b.shape[0], DIAG_BLOCK)])


def _pair_args(bf, sb, s):
    lhs, rhs = [], []
    for e0 in range(0, bf.shape[0], 2 * s):
        o0 = e0 + s
        rf = bf[o0 - 1:o0, :]
        rb = sb[o0:o0 + 1, :]
        lhs += [sb[e0:o0] - rb, bf[o0:o0 + s] - rf]
        rhs += [rf - bf[e0:o0], rb - sb[o0:o0 + s]]
    return _cat(lhs), _cat(rhs)


def _mixer_kernel(*refs, seq, heads, seg, has_state_in, has_state_out, has_x1_in):
    (q_ref, k_ref, v_ref, gf_ref, gb_ref, og_ref, p_ref, cbg_ref, cw_ref, gn_ref, wo_ref, x_ref,
     mod_ref) = refs[:13]
    refs = refs[13:]
    if has_state_in:
        s0f_ref, s0b_ref = refs[:2]
        refs = refs[2:]
    if has_x1_in:
        refs = refs[1:]
    x1_ref = refs[0]
    refs = refs[1:]
    if has_state_out:
        sf_ref, sb_ref = refs[:2]
        refs = refs[2:]
    a_ref = refs[0]
    hsteps = N_HEADS // heads
    half = SUPER_BLOCK // 2

    bf_all = _join_prefix(gf_ref[...], seg)
    sb_all = _join_suffix(gb_ref[...], seg)

    rows_v = lax.broadcasted_iota(jnp.int32, (seq, HEAD_V), 0)
    r_loc = lax.broadcasted_iota(jnp.int32, (DIAG_BLOCK, half), 0)
    lane = lax.broadcasted_iota(jnp.int32, (DIAG_BLOCK, half), 1)
    ones = jnp.ones((HEAD_V, half), BF16)
    scale = lambda t, e: t * e.astype(BF16)

    mixes = []
    for h in range(heads):
        ks = slice(h * HEAD_K, (h + 1) * HEAD_K)
        vs = slice(h * HEAD_V, (h + 1) * HEAD_V)
        q = q_ref[:, ks]
        k = k_ref[:, ks]
        v = v_ref[:, vs]
        bf = bf_all[:, ks]
        sb = sb_all[:, ks]

        e0f = jnp.exp2(_diag_args(bf, DIAG_BLOCK // 2))
        e0b = jnp.exp2(_diag_args(sb, DIAG_BLOCK // 2 - 1))
        q0f, k0f = scale(q, e0f), scale(k, 1.0 / e0f)
        q0b, k0b = scale(q, e0b), scale(k, 1.0 / e0b)

        lhs_levels, rhs_levels = [], []
        s = DIAG_BLOCK
        while s < seq:
            la, ra = _pair_args(bf, sb, s)
            lhs_levels.append(scale(q, jnp.exp2(la)))
            rhs_levels.append(scale(k, jnp.exp2(ra)))
            s *= 2

        for base in range(0, seq, SUPER_BLOCK):
            rs = slice(base, base + SUPER_BLOCK)
            m0f = _dot_nt(q0f[rs], k0f[rs])
            m0b = _dot_nt(q0b[rs], k0b[rs])
            m1 = _dot_nt(lhs_levels[0][rs], rhs_levels[0][rs])
            m2 = _dot_nt(lhs_levels[1][rs], rhs_levels[1][rs])
            for bi in range(SUPER_BLOCK // DIAG_BLOCK):
                rr = slice(bi * DIAG_BLOCK, (bi + 1) * DIAG_BLOCK)
                for lt in range(2):
                    cc = slice(lt * half, (lt + 1) * half)
                    if lt == bi // 2:
                        c_loc = lane - DIAG_BLOCK * (bi % 2)
                        in_diag = (lane >= DIAG_BLOCK) if bi % 2 else (lane < DIAG_BLOCK)
                        diag = (jnp.where(c_loc <= r_loc, m0f[rr, cc], 0.0)
                                + jnp.where(c_loc >= r_loc, m0b[rr, cc], 0.0))
                        piece = jnp.where(in_diag, diag, m1[rr, cc])
                    else:
                        piece = m2[rr, cc]
                    a_ref[base + bi * DIAG_BLOCK:base + (bi + 1) * DIAG_BLOCK,
                          base + lt * half:base + (lt + 1) * half] = piece.astype(BF16)
        s = SUPER_BLOCK
        lev = 2
        while s < seq:
            lhs, rhs = lhs_levels[lev], rhs_levels[lev]
            for e0 in range(0, seq, 2 * s):
                ev = slice(e0, e0 + s)
                od = slice(e0 + s, e0 + 2 * s)
                a_ref[od, ev] = _dot_nt(lhs[od], rhs[ev]).astype(BF16)
                a_ref[ev, od] = _dot_nt(lhs[ev], rhs[od]).astype(BF16)
            s *= 2
            lev += 1

        o = _dot(a_ref[...], v)
        if has_state_in:
            qi = jnp.concatenate([scale(q, jnp.exp2(bf)), scale(q, jnp.exp2(sb))], axis=1)
            s0 = jnp.concatenate([s0f_ref[h].astype(BF16), s0b_ref[h].astype(BF16)], axis=0)
            o = o + _dot(qi, s0)
        if has_state_out:
            sf_ref[h] = _dot_tn(scale(k, jnp.exp2(bf[seq - 1:seq, :] - bf)), v)
            sb_ref[h] = _dot_tn(scale(k, jnp.exp2(sb[0:1, :] - sb)), v)

        inv = lax.rsqrt(_dot((o * o).astype(BF16), ones) * (1.0 / HEAD_V) + EPS)
        o = jnp.concatenate([o[:, :half] * inv, o[:, half:] * inv], axis=1) * gn_ref[...]
        p = p_ref[:, vs].astype(F32)
        conv = (jnp.where(rows_v >= 1, pltpu.roll(p, 1, axis=0), 0.0) * cw_ref[0:1, vs] + p * cw_ref[1:2, vs]
                + jnp.where(rows_v < seq - 1, pltpu.roll(p, seq - 1, axis=0), 0.0) * cw_ref[2:3, vs])
        mixes.append(og_ref[:, vs] * o.astype(BF16) + cbg_ref[:, vs] * conv.astype(BF16))

    contrib = _dot(_cat(mixes, axis=1), wo_ref[...])
    ga1 = mod_ref[:, 2 * D_MODEL:3 * D_MODEL]
    if hsteps == 1:
        x1_ref[...] = x_ref[...] + ga1 * contrib
    else:
        hstep = pl.program_id(1)

        @pl.when(hstep == 0)
        def _():
            x1_ref[...] = x_ref[...] + ga1 * contrib

        @pl.when(hstep != 0)
        def _():
            x1_ref[...] += ga1 * contrib


def _mixer(x, mod3, proj, conv_w, gn, wo, states, x1_prev, *, seq, heads, seg, first_row, mod_row,
           has_state_out):
    q, k, v, og, p, cbg, g = proj
    n = x.shape[0]
    nb = n // seq
    hsteps = N_HEADS // heads
    has_state_in = states is not None
    assert seq % SUPER_BLOCK == 0 and seq % seg == 0 and first_row % seq == 0
    b0 = first_row // seq
    col = lambda b, h: (b0 + b, h)
    st_block = (None, None, heads, HEAD_K, HEAD_V)
    st_map = lambda b, h: (b, 0, h, 0, 0)
    in_specs = [pl.BlockSpec((seq, heads * HEAD_K), col),
                pl.BlockSpec((seq, heads * HEAD_K), col),
                pl.BlockSpec((seq, heads * HEAD_V), col),
                pl.BlockSpec((seq, heads * HEAD_K), col),
                pl.BlockSpec((seq, heads * HEAD_K), lambda b, h: (b0 + b, hsteps + h)),
                pl.BlockSpec((seq, heads * HEAD_V), col),
                pl.BlockSpec((seq, heads * HEAD_V), col),
                pl.BlockSpec((seq, heads * HEAD_V), col),
                pl.BlockSpec((3, heads * HEAD_V), lambda b, h: (0, h)),
                pl.BlockSpec((1, HEAD_V), lambda b, h: (0, 0)),
                pl.BlockSpec((heads * HEAD_V, D_MODEL), lambda b, h: (h, 0)),
                pl.BlockSpec((seq, D_MODEL), lambda b, h: (b, 0)),
                pl.BlockSpec((None, 1, N_MOD * D_MODEL), lambda b, h: (mod_row(b), 0, 0))]
    args = [q, k, v, g, g, og, p, cbg, conv_w, gn, wo, x, mod3]
    if has_state_in:
        in_specs += [pl.BlockSpec(st_block, st_map)] * 2
        args += list(states)
    aliases = {}
    if x1_prev is not None:
        aliases = {len(args): 0}
        in_specs.append(pl.BlockSpec(memory_space=pl.ANY))
        args.append(x1_prev)
    out_shape = [jax.ShapeDtypeStruct((q.shape[0], D_MODEL), F32)]
    out_specs = [pl.BlockSpec((seq, D_MODEL), lambda b, h: (b0 + b, 0))]
    if has_state_out:
        out_shape += [jax.ShapeDtypeStruct((nb, 1, N_HEADS, HEAD_K, HEAD_V), F32)] * 2
        out_specs += [pl.BlockSpec(st_block, st_map)] * 2
    return pl.pallas_call(
        functools.partial(_mixer_kernel, seq=seq, heads=heads, seg=seg, has_state_in=has_state_in,
                          has_state_out=has_state_out, has_x1_in=x1_prev is not None),
        out_shape=out_shape,
        grid=(nb, hsteps),
        in_specs=in_specs,
        out_specs=out_specs,
        scratch_shapes=[pltpu.VMEM((seq, seq), BF16)],
        input_output_aliases=aliases,
        compiler_params=_params("arbitrary", "arbitrary"),
        name="mixer",
    )(*args)


N_FFN_TILES = D_FF // FFN_TILE


def _ffn_kernel(x1_ref, mod_ref, n2_ref, wu_ref, wg_ref, wd_ref, cw_ref, cb_ref, nf_ref,
                oa_ref, ob_ref, wu_scr, wg_scr, wd_scr, *, tiles_a, geom_a, geom_b):
    step = pl.program_id(0)
    tile = step - N_FFN_TILES

    @pl.when(step < N_FFN_TILES)
    def _():
        wu_scr[step] = wu_ref[...].astype(BF16)
        wg_scr[step] = wg_ref[...].astype(BF16)
        wd_scr[step] = wd_ref[...].astype(BF16)

    weights = (wu_scr, wg_scr, wd_scr, cw_ref, cb_ref)

    @pl.when(jnp.logical_and(tile >= 0, tile < tiles_a))
    def _():
        _ffn_tile(x1_ref, oa_ref, mod_ref, n2_ref, nf_ref, weights, *geom_a)

    @pl.when(tile >= tiles_a)
    def _():
        _ffn_tile(x1_ref, ob_ref, mod_ref, n2_ref, nf_ref, weights, *geom_b)


def _ffn_tile(x1_ref, o_ref, mod_ref, n2_ref, nf_ref, weights, seq, width):
    wu_scr, wg_scr, wd_scr, cw_ref, cb_ref = weights
    tm = x1_ref.shape[0]
    mod = mod_ref[...]
    sh2 = mod[:, 3 * D_MODEL:4 * D_MODEL]
    sc2 = mod[:, 4 * D_MODEL:5 * D_MODEL]
    ga2 = mod[:, 5 * D_MODEL:6 * D_MODEL]
    x1 = x1_ref[...]
    xb = (_rms(x1, n2_ref[...]) * (1.0 + sc2) + sh2).astype(BF16)

    rows = lax.broadcasted_iota(jnp.int32, (tm, FFN_TILE), 0)
    col_in_row = rows & (width - 1)
    has_left = col_in_row != 0
    has_right = col_in_row != width - 1
    n_rows = seq // width
    zrow = jnp.zeros((width, FFN_TILE), F32)

    def up_rows(z):
        return jnp.concatenate([zrow, z[:tm - width]], axis=0)

    def down_rows(z):
        return jnp.concatenate([z[width:], zrow], axis=0)

    def up_and_gate(f):
        return _dot(xb, wu_scr[f]), _dot(xb, wg_scr[f])

    ahead = up_and_gate(0)
    for f in range(N_FFN_TILES):
        fs = slice(f * FFN_TILE, (f + 1) * FFN_TILE)
        up, gate = ahead
        if f + 1 < N_FFN_TILES:
            ahead = up_and_gate(f + 1)
        left = jnp.where(has_left, pltpu.roll(up, 1, axis=0), 0.0)
        right = jnp.where(has_right, pltpu.roll(up, tm - 1, axis=0), 0.0)
        w = lambda i, j: cw_ref[3 * i + j:3 * i + j + 1, fs]
        cv = left * w(1, 0) + up * w(1, 1) + right * w(1, 2) + cb_ref[:, fs]
        if n_rows > 1:
            cv = cv + (up_rows(left) * w(0, 0) + up_rows(up) * w(0, 1) + up_rows(right) * w(0, 2)
                       + down_rows(left) * w(2, 0) + down_rows(up) * w(2, 1) + down_rows(right) * w(2, 2))
        act = (cv * _sigmoid(cv) * gate).astype(BF16)
        part = _dot(act, wd_scr[f])
        if f == 0:
            o_ref[...] = part
        else:
            o_ref[...] += part

    o_ref[...] = _rms(x1 + ga2 * o_ref[...], nf_ref[...])


def _ffn(x1, rows_a, mod3, n2g, wu, wg, wd, cw, cb, nfg, *, tm, geom_a, geom_b, mod_row):
    rows_b = x1.shape[0] - rows_a
    tiles_a, tiles_b = rows_a // tm, rows_b // tm
    for seq, width in (geom_a, geom_b):
        assert width & (width - 1) == 0 and tm % seq == 0 and (seq == width or tm == seq)
    tile = lambda i: jnp.maximum(i - N_FFN_TILES, 0)
    blk_a = lambda i: (jnp.minimum(tile(i), tiles_a - 1), 0)
    blk_b = lambda i: (jnp.maximum(tile(i) - tiles_a, 0), 0)
    w_step = lambda i: jnp.minimum(i, N_FFN_TILES - 1)
    return pl.pallas_call(
        functools.partial(_ffn_kernel, tiles_a=tiles_a, geom_a=geom_a, geom_b=geom_b),
        out_shape=[jax.ShapeDtypeStruct((rows_a, D_MODEL), F32), jax.ShapeDtypeStruct((rows_b, D_MODEL), F32)],
        grid=(N_FFN_TILES + tiles_a + tiles_b,),
        in_specs=[pl.BlockSpec((tm, D_MODEL), lambda i: (tile(i), 0)),
                  pl.BlockSpec((None, 1, N_MOD * D_MODEL), lambda i: (mod_row(tile(i)), 0, 0)),
                  _const_spec(n2g.shape),
                  pl.BlockSpec((D_MODEL, FFN_TILE), lambda i: (0, w_step(i))),
                  pl.BlockSpec((D_MODEL, FFN_TILE), lambda i: (0, w_step(i))),
                  pl.BlockSpec((FFN_TILE, D_MODEL), lambda i: (w_step(i), 0)),
                  _const_spec(cw.shape), _const_spec(cb.shape), _const_spec(nfg.shape)],
        out_specs=[pl.BlockSpec((tm, D_MODEL), blk_a), pl.BlockSpec((tm, D_MODEL), blk_b)],
        scratch_shapes=[pltpu.VMEM((N_FFN_TILES, D_MODEL, FFN_TILE), BF16),
                        pltpu.VMEM((N_FFN_TILES, D_MODEL, FFN_TILE), BF16),
                        pltpu.VMEM((N_FFN_TILES, FFN_TILE, D_MODEL), BF16)],
        compiler_params=_params("arbitrary"),
        name="ffn",
    )(x1, mod3, n2g, wu, wg, wd, cw, cb, nfg)


def kernel(x_prompt, x_sample, c, state_gla_fwd, state_gla_bwd, c_ctx, w_ada, b_ada, norm1_g, w_in, w_gk_f,
           b_gk_f, w_gk_b, b_gk_b, gla_norm_g, conv_mix_w, w_out, norm2_g, ffn_w_up, ffn_w_gate,
           ffn_conv_w, ffn_conv_b, ffn_w_down, normf_g):
    bp, lp, d = x_prompt.shape
    bs, ls, _ = x_sample.shape
    assert w_ada.shape[0] == 1 and d == D_MODEL and bs + 1 <= MOD_ROWS

    cc = jnp.zeros((MOD_ROWS, d), F32).at[0].set(c_ctx).at[1:1 + bs].set(c)
    mod3 = _ada(cc, w_ada[0], b_ada[0][None, :]).reshape(MOD_ROWS, 1, N_MOD * d)

    assert IN_SPLITS == (KEY_DIM, KEY_DIM, VAL_DIM, VAL_DIM, GATE_RANK, GATE_RANK) + (CONV_DIM,) * 5
    wcode = (jnp.zeros((d, MXU_LANES_V7X), F32).at[:, :2 * GATE_RANK].set(w_in[0][:, HEAD_COLS:TAIL_OFF])
             .astype(BF16))
    wgk = (jnp.zeros((MXU_LANES_V7X, 2 * KEY_DIM), F32)
           .at[:GATE_RANK, :KEY_DIM].set(w_gk_f[0]).at[GATE_RANK:2 * GATE_RANK, KEY_DIM:].set(w_gk_b[0])
           .astype(BF16))
    bgk = jnp.concatenate([b_gk_f[0], b_gk_b[0]])[None, :]
    wo = w_out[0].astype(BF16)
    cw9 = ffn_conv_w[0].reshape(9, D_FF)
    cb = ffn_conv_b[0][None, :]
    n1g, n2g, nfg, gn = norm1_g[0][None, :], norm2_g[0][None, :], normf_g[None, :], gla_norm_g[0][None, :]

    xp, xs = x_prompt.reshape(-1, d), x_sample.reshape(-1, d)
    tiles_p = xp.shape[0] // INPROJ_ROWS
    seg = min(lp, ls, INPROJ_ROWS)
    proj = _inproj(xp, xs, mod3, n1g, w_in[0], wcode, wgk, bgk, tm=INPROJ_ROWS, seg=seg,
                   mod_row=lambda t: jnp.where(t < tiles_p, 0, 1 + ((t - tiles_p) * INPROJ_ROWS) // ls))

    mix = functools.partial(_mixer, mod3=mod3, proj=proj, conv_w=conv_mix_w[0], gn=gn, wo=wo, seg=seg)
    x1, new_f, new_b = mix(xp, states=None, x1_prev=None, seq=lp, heads=N_HEADS, first_row=0,
                           mod_row=lambda b: 0, has_state_out=True)
    x1, = mix(xs, states=(state_gla_fwd, state_gla_bwd), x1_prev=x1, seq=ls, heads=1, first_row=xp.shape[0],
              mod_row=lambda b: 1 + b, has_state_out=False)

    ffn_tiles_p = xp.shape[0] // FFN_ROWS
    y_prompt, y_sample = _ffn(
        x1, xp.shape[0], mod3, n2g, ffn_w_up[0], ffn_w_gate[0], ffn_w_down[0], cw9, cb, nfg, tm=FFN_ROWS,
        geom_a=(lp, lp), geom_b=(ls, GRID_W),
        mod_row=lambda t: jnp.where(t < ffn_tiles_p, 0, 1 + ((t - ffn_tiles_p) * FFN_ROWS) // ls))
    return y_prompt.reshape(x_prompt.shape), y_sample.reshape(x_sample.shape), new_f, new_b
```

```python
import functools

import jax
import jax.numpy as jnp
from jax import lax
from jax.experimental import pallas as pl
from jax.experimental.pallas import tpu as pltpu

D_MODEL = 1024
N_HEADS = 4
HEAD_K = 128
HEAD_V = 256
KEY_DIM = N_HEADS * HEAD_K
VAL_DIM = N_HEADS * HEAD_V
GATE_RANK = 16
GATE_NORMALIZER = 16.0
LOG2_E = 1.4426950408889634
CONV_DIM = D_MODEL
D_FF = 2816
N_MOD = 6
EPS = 1e-6
GRID_W = 64
IN_SPLITS = (KEY_DIM, KEY_DIM, VAL_DIM, VAL_DIM, GATE_RANK, GATE_RANK,
             CONV_DIM, CONV_DIM, CONV_DIM, VAL_DIM, CONV_DIM)

DIAG_BLOCK = 64
SUPER_BLOCK = 256
MXU_LANES_V7X = 128
FFN_TILE = 256
FFN_ROWS = 1024
INPROJ_ROWS = 512
ADA_TILE = 1024
MOD_ROWS = 8
VMEM_LIMIT_BYTES = 56 * 1024 * 1024

F32 = jnp.float32
BF16 = jnp.bfloat16


def _dot(a, b):
    return jnp.dot(a, b, preferred_element_type=F32)


def _dot_nt(a, b):
    return lax.dot_general(a, b, (((1,), (1,)), ((), ())), preferred_element_type=F32)


def _dot_tn(a, b):
    return lax.dot_general(a, b, (((0,), (0,)), ((), ())), preferred_element_type=F32)


def _sigmoid(x):
    return 1.0 / (1.0 + jnp.exp(-x))


def _rms(x, g):
    return x * lax.rsqrt(jnp.mean(x * x, axis=-1, keepdims=True) + EPS) * g


def _const_spec(shape):
    nd = len(shape)
    return pl.BlockSpec(shape, lambda *_: (0,) * nd, pipeline_mode=pl.Buffered(1))


def _params(*sem):
    return pltpu.CompilerParams(dimension_semantics=sem, vmem_limit_bytes=VMEM_LIMIT_BYTES)


def _ada_kernel(c_ref, w_ref, b_ref, o_ref):
    c = c_ref[...]
    s = (c * _sigmoid(c)).astype(BF16)
    o_ref[...] = _dot(s, w_ref[...].astype(BF16)) + b_ref[...]


def _ada(cc, w_ada, b_ada):
    n = w_ada.shape[1]
    return pl.pallas_call(
        _ada_kernel,
        out_shape=jax.ShapeDtypeStruct((MOD_ROWS, n), F32),
        grid=(n // ADA_TILE,),
        in_specs=[pl.BlockSpec((MOD_ROWS, D_MODEL), lambda j: (0, 0)),
                  pl.BlockSpec((D_MODEL, ADA_TILE), lambda j: (0, j)),
                  pl.BlockSpec((1, ADA_TILE), lambda j: (0, j))],
        out_specs=pl.BlockSpec((MOD_ROWS, ADA_TILE), lambda j: (0, j)),
        compiler_params=_params("arbitrary"),
        name="ada",
    )(cc, w_ada, b_ada)


def _prefix_sum(x, pos, seg):
    s = 1
    while s < seg:
        x = x + jnp.where(pos >= s, pltpu.roll(x, s, axis=0), 0.0)
        s *= 2
    return x


def _suffix_sum(x, pos, seg):
    n, s = x.shape[0], 1
    while s < seg:
        x = x + jnp.where(pos < seg - s, pltpu.roll(x, n - s, axis=0), 0.0)
        s *= 2
    return x


W_BLOCK = 512
HEAD_COLS = 2 * KEY_DIM + 2 * VAL_DIM
TAIL_OFF = HEAD_COLS + 2 * GATE_RANK
TAIL_SHIFT = TAIL_OFF % MXU_LANES_V7X
N_HEAD_BLOCKS = HEAD_COLS // W_BLOCK
N_W_BLOCKS = N_HEAD_BLOCKS + 5 * CONV_DIM // W_BLOCK


def _inproj_kernel(xa_ref, xb_ref, mod_ref, n1_ref, wa_ref, wb_ref, wgk_ref, bgk_ref,
                   q_ref, k_ref, v_ref, og_ref, p_ref, cbg_ref, g_ref, w_scr, wcode_scr, *, tiles_a, seg):
    step = pl.program_id(0)

    @pl.when(step < N_HEAD_BLOCKS)
    def _():
        w_scr[step] = wa_ref[...].T.astype(BF16)
        lane = lax.broadcasted_iota(jnp.int32, (D_MODEL, MXU_LANES_V7X), 1)
        wcode_scr[...] = jnp.where(lane < 2 * GATE_RANK, wb_ref[...].T, 0.0).astype(BF16)

    @pl.when(jnp.logical_and(step >= N_HEAD_BLOCKS, step < N_W_BLOCKS))
    def _():
        rows = jnp.concatenate([wa_ref[TAIL_SHIFT:, :], wb_ref[:TAIL_SHIFT, :]], axis=0)
        w_scr[step] = rows.T.astype(BF16)

    @pl.when(step >= N_W_BLOCKS)
    def _():
        first_group = step - N_W_BLOCKS < tiles_a
        mod = mod_ref[...]
        sh1 = mod[:, 0:D_MODEL]
        sc1 = mod[:, D_MODEL:2 * D_MODEL]
        x = jnp.where(first_group, xa_ref[...], xb_ref[...])
        xn = (_rms(x, n1_ref[...]) * (1.0 + sc1) + sh1).astype(BF16)
        halves = range(CONV_DIM // W_BLOCK)
        head = lambda blk: _dot(xn, w_scr[blk])
        tail = lambda grp, j: _dot(xn, w_scr[N_HEAD_BLOCKS + grp * len(halves) + j])
        cols = lambda j: slice(j * W_BLOCK, (j + 1) * W_BLOCK)

        q_ref[...] = (head(0) * (HEAD_K ** -0.5)).astype(BF16)
        k_ref[...] = head(1).astype(BF16)
        for j in halves:
            v_ref[:, cols(j)] = head(2 + j).astype(BF16)
            g_out = head(2 + len(halves) + j)
            og_ref[:, cols(j)] = (g_out * _sigmoid(g_out) * _sigmoid(tail(3, j))).astype(BF16)
            p_ref[:, cols(j)] = (tail(1, j) * tail(2, j)).astype(BF16)
            cbg_ref[:, cols(j)] = (_sigmoid(tail(4, j)) * tail(0, j)).astype(BF16)
        z = _dot(_dot(xn, wcode_scr[...]).astype(BF16), wgk_ref[...]) + bgk_ref[...]
        g = (jnp.minimum(z, 0.0) - jnp.log(1.0 + jnp.exp(-jnp.abs(z)))) * (LOG2_E / GATE_NORMALIZER)
        pos = lax.broadcasted_iota(jnp.int32, (g.shape[0], KEY_DIM), 0) & (seg - 1)
        g_ref[:, :KEY_DIM] = _prefix_sum(g[:, :KEY_DIM], pos, seg)
        g_ref[:, KEY_DIM:] = _suffix_sum(g[:, KEY_DIM:], pos, seg)


def _inproj(xa, xb, mod3, n1g, w_in_t, wgk, bgk, *, tm, seg, mod_row):
    tiles_a, tiles_b = xa.shape[0] // tm, xb.shape[0] // tm
    n = xa.shape[0] + xb.shape[0]
    assert seg & (seg - 1) == 0 and tm % seg == 0
    assert KEY_DIM == W_BLOCK and HEAD_COLS % W_BLOCK == 0 and W_BLOCK % MXU_LANES_V7X == 0
    tile = lambda i: jnp.maximum(i - N_W_BLOCKS, 0)
    row = lambda i: (tile(i), 0)
    lanes_per_block = W_BLOCK // MXU_LANES_V7X
    wide = lambda w, dt: (jax.ShapeDtypeStruct((n, w), dt), pl.BlockSpec((tm, w), row))
    outs = [wide(KEY_DIM, BF16), wide(KEY_DIM, BF16), wide(VAL_DIM, BF16), wide(VAL_DIM, BF16),
            wide(CONV_DIM, BF16), wide(CONV_DIM, BF16), wide(2 * KEY_DIM, F32)]
    return pl.pallas_call(
        functools.partial(_inproj_kernel, tiles_a=tiles_a, seg=seg),
        out_shape=[o[0] for o in outs],
        grid=(N_W_BLOCKS + tiles_a + tiles_b,),
        in_specs=[pl.BlockSpec((tm, D_MODEL), lambda i: (jnp.minimum(tile(i), tiles_a - 1), 0)),
                  pl.BlockSpec((tm, D_MODEL), lambda i: (jnp.maximum(tile(i) - tiles_a, 0), 0)),
                  pl.BlockSpec((None, 1, N_MOD * D_MODEL), lambda i: (mod_row(tile(i)), 0, 0)),
                  _const_spec(n1g.shape),
                  pl.BlockSpec((W_BLOCK, D_MODEL), lambda i: (jnp.minimum(i, N_W_BLOCKS - 1), 0)),
                  pl.BlockSpec((MXU_LANES_V7X, D_MODEL),
                               lambda i: (jnp.where(i < N_HEAD_BLOCKS, HEAD_COLS // MXU_LANES_V7X,
                                                    (jnp.minimum(i, N_W_BLOCKS - 1) + 1) * lanes_per_block), 0)),
                  _const_spec(wgk.shape), _const_spec(bgk.shape)],
        out_specs=[o[1] for o in outs],
        scratch_shapes=[pltpu.VMEM((N_W_BLOCKS, D_MODEL, W_BLOCK), BF16),
                        pltpu.VMEM((D_MODEL, MXU_LANES_V7X), BF16)],
        compiler_params=_params("arbitrary"),
        name="in_proj",
    )(xa, xb, mod3, n1g, w_in_t, w_in_t, wgk, bgk)


def _cat(parts, axis=0):
    return parts[0] if len(parts) == 1 else jnp.concatenate(parts, axis=axis)


def _join_prefix(b, seg):
    parts = [b[:seg]]
    for c in range(1, b.shape[0] // seg):
        parts.append(b[c * seg:(c + 1) * seg] + parts[-1][seg - 1:seg, :])
    return _cat(parts)


def _join_suffix(b, seg):
    n = b.shape[0] // seg
    parts = [b[(n - 1) * seg:]]
    for c in range(n - 2, -1, -1):
        parts.insert(0, b[c * seg:(c + 1) * seg] + parts[0][0:1, :])
    return _cat(parts)


def _diag_args(b, off):
    return _cat([b[c:c + DIAG_BLOCK] - b[c + off:c + off + 1, :] for c in range(0, b.shape[0], DIAG_BLOCK)])


def _pair_args(bf, sb, s):
    lhs, rhs = [], []
    for e0 in range(0, bf.shape[0], 2 * s):
        o0 = e0 + s
        rf = bf[o0 - 1:o0, :]
        rb = sb[o0:o0 + 1, :]
        lhs += [sb[e0:o0] - rb, bf[o0:o0 + s] - rf]
        rhs += [rf - bf[e0:o0], rb - sb[o0:o0 + s]]
    return _cat(lhs), _cat(rhs)


def _mixer_kernel(*refs, seq, heads, seg, has_state_in, has_state_out, has_x1_in):
    (q_ref, k_ref, v_ref, gf_ref, gb_ref, og_ref, p_ref, cbg_ref, cw_ref, gn_ref, wo_ref, x_ref,
     mod_ref) = refs[:13]
    refs = refs[13:]
    if has_state_in:
        s0f_ref, s0b_ref = refs[:2]
        refs = refs[2:]
    if has_x1_in:
        refs = refs[1:]
    x1_ref = refs[0]
    refs = refs[1:]
    if has_state_out:
        sf_ref, sb_ref = refs[:2]
        refs = refs[2:]
    a_ref = refs[0]
    hsteps = N_HEADS // heads
    half = SUPER_BLOCK // 2

    bf_all = _join_prefix(gf_ref[...], seg)
    sb_all = _join_suffix(gb_ref[...], seg)

    rows_v = lax.broadcasted_iota(jnp.int32, (seq, HEAD_V), 0)
    r_loc = lax.broadcasted_iota(jnp.int32, (DIAG_BLOCK, half), 0)
    lane = lax.broadcasted_iota(jnp.int32, (DIAG_BLOCK, half), 1)
    ones = jnp.ones((HEAD_V, half), BF16)
    scale = lambda t, e: t * e.astype(BF16)

    mixes = []
    for h in range(heads):
        ks = slice(h * HEAD_K, (h + 1) * HEAD_K)
        vs = slice(h * HEAD_V, (h + 1) * HEAD_V)
        q = q_ref[:, ks]
        k = k_ref[:, ks]
        v = v_ref[:, vs]
        bf = bf_all[:, ks]
        sb = sb_all[:, ks]

        e0f = jnp.exp2(_diag_args(bf, DIAG_BLOCK // 2))
        e0b = jnp.exp2(_diag_args(sb, DIAG_BLOCK // 2 - 1))
        q0f, k0f = scale(q, e0f), scale(k, 1.0 / e0f)
        q0b, k0b = scale(q, e0b), scale(k, 1.0 / e0b)

        lhs_levels, rhs_levels = [], []
        s = DIAG_BLOCK
        while s < seq:
            la, ra = _pair_args(bf, sb, s)
            lhs_levels.append(scale(q, jnp.exp2(la)))
            rhs_levels.append(scale(k, jnp.exp2(ra)))
            s *= 2

        for base in range(0, seq, SUPER_BLOCK):
            rs = slice(base, base + SUPER_BLOCK)
            m0f = _dot_nt(q0f[rs], k0f[rs])
            m0b = _dot_nt(q0b[rs], k0b[rs])
            m1 = _dot_nt(lhs_levels[0][rs], rhs_levels[0][rs])
            m2 = _dot_nt(lhs_levels[1][rs], rhs_levels[1][rs])
            for bi in range(SUPER_BLOCK // DIAG_BLOCK):
                rr = slice(bi * DIAG_BLOCK, (bi + 1) * DIAG_BLOCK)
                for lt in range(2):
                    cc = slice(lt * half, (lt + 1) * half)
                    if lt == bi // 2:
                        c_loc = lane - DIAG_BLOCK * (bi % 2)
                        in_diag = (lane >= DIAG_BLOCK) if bi % 2 else (lane < DIAG_BLOCK)
                        diag = (jnp.where(c_loc <= r_loc, m0f[rr, cc], 0.0)
                                + jnp.where(c_loc >= r_loc, m0b[rr, cc], 0.0))
                        piece = jnp.where(in_diag, diag, m1[rr, cc])
                    else:
                        piece = m2[rr, cc]
                    a_ref[base + bi * DIAG_BLOCK:base + (bi + 1) * DIAG_BLOCK,
                          base + lt * half:base + (lt + 1) * half] = piece.astype(BF16)
        s = SUPER_BLOCK
        lev = 2
        while s < seq:
            lhs, rhs = lhs_levels[lev], rhs_levels[lev]
            for e0 in range(0, seq, 2 * s):
                ev = slice(e0, e0 + s)
                od = slice(e0 + s, e0 + 2 * s)
                a_ref[od, ev] = _dot_nt(lhs[od], rhs[ev]).astype(BF16)
                a_ref[ev, od] = _dot_nt(lhs[ev], rhs[od]).astype(BF16)
            s *= 2
            lev += 1

        o = _dot(a_ref[...], v)
        if has_state_in:
            qi = jnp.concatenate([scale(q, jnp.exp2(bf)), scale(q, jnp.exp2(sb))], axis=1)
            s0 = jnp.concatenate([s0f_ref[h].astype(BF16), s0b_ref[h].astype(BF16)], axis=0)
            o = o + _dot(qi, s0)
        if has_state_out:
            sf_ref[h] = _dot_tn(scale(k, jnp.exp2(bf[seq - 1:seq, :] - bf)), v)
            sb_ref[h] = _dot_tn(scale(k, jnp.exp2(sb[0:1, :] - sb)), v)

        inv = lax.rsqrt(_dot((o * o).astype(BF16), ones) * (1.0 / HEAD_V) + EPS)
        o = jnp.concatenate([o[:, :half] * inv, o[:, half:] * inv], axis=1) * gn_ref[...]
        p = p_ref[:, vs].astype(F32)
        conv = (jnp.where(rows_v >= 1, pltpu.roll(p, 1, axis=0), 0.0) * cw_ref[0:1, vs] + p * cw_ref[1:2, vs]
                + jnp.where(rows_v < seq - 1, pltpu.roll(p, seq - 1, axis=0), 0.0) * cw_ref[2:3, vs])
        mixes.append(og_ref[:, vs] * o.astype(BF16) + cbg_ref[:, vs] * conv.astype(BF16))

    contrib = _dot(_cat(mixes, axis=1), wo_ref[...])
    ga1 = mod_ref[:, 2 * D_MODEL:3 * D_MODEL]
    if hsteps == 1:
        x1_ref[...] = x_ref[...] + ga1 * contrib
    else:
        hstep = pl.program_id(1)

        @pl.when(hstep == 0)
        def _():
            x1_ref[...] = x_ref[...] + ga1 * contrib

        @pl.when(hstep != 0)
        def _():
            x1_ref[...] += ga1 * contrib


def _mixer(x, mod3, proj, conv_w, gn, wo, states, x1_prev, *, seq, heads, seg, first_row, mod_row,
           has_state_out):
    q, k, v, og, p, cbg, g = proj
    n = x.shape[0]
    nb = n // seq
    hsteps = N_HEADS // heads
    has_state_in = states is not None
    assert seq % SUPER_BLOCK == 0 and seq % seg == 0 and first_row % seq == 0
    b0 = first_row // seq
    col = lambda b, h: (b0 + b, h)
    st_block = (None, None, heads, HEAD_K, HEAD_V)
    st_map = lambda b, h: (b, 0, h, 0, 0)
    in_specs = [pl.BlockSpec((seq, heads * HEAD_K), col),
                pl.BlockSpec((seq, heads * HEAD_K), col),
                pl.BlockSpec((seq, heads * HEAD_V), col),
                pl.BlockSpec((seq, heads * HEAD_K), col),
                pl.BlockSpec((seq, heads * HEAD_K), lambda b, h: (b0 + b, hsteps + h)),
                pl.BlockSpec((seq, heads * HEAD_V), col),
                pl.BlockSpec((seq, heads * HEAD_V), col),
                pl.BlockSpec((seq, heads * HEAD_V), col),
                pl.BlockSpec((3, heads * HEAD_V), lambda b, h: (0, h)),
                pl.BlockSpec((1, HEAD_V), lambda b, h: (0, 0)),
                pl.BlockSpec((heads * HEAD_V, D_MODEL), lambda b, h: (h, 0)),
                pl.BlockSpec((seq, D_MODEL), lambda b, h: (b, 0)),
                pl.BlockSpec((None, 1, N_MOD * D_MODEL), lambda b, h: (mod_row(b), 0, 0))]
    args = [q, k, v, g, g, og, p, cbg, conv_w, gn, wo, x, mod3]
    if has_state_in:
        in_specs += [pl.BlockSpec(st_block, st_map)] * 2
        args += list(states)
    aliases = {}
    if x1_prev is not None:
        aliases = {len(args): 0}
        in_specs.append(pl.BlockSpec(memory_space=pl.ANY))
        args.append(x1_prev)
    out_shape = [jax.ShapeDtypeStruct((q.shape[0], D_MODEL), F32)]
    out_specs = [pl.BlockSpec((seq, D_MODEL), lambda b, h: (b0 + b, 0))]
    if has_state_out:
        out_shape += [jax.ShapeDtypeStruct((nb, 1, N_HEADS, HEAD_K, HEAD_V), F32)] * 2
        out_specs += [pl.BlockSpec(st_block, st_map)] * 2
    return pl.pallas_call(
        functools.partial(_mixer_kernel, seq=seq, heads=heads, seg=seg, has_state_in=has_state_in,
                          has_state_out=has_state_out, has_x1_in=x1_prev is not None),
        out_shape=out_shape,
        grid=(nb, hsteps),
        in_specs=in_specs,
        out_specs=out_specs,
        scratch_shapes=[pltpu.VMEM((seq, seq), BF16)],
        input_output_aliases=aliases,
        compiler_params=_params("arbitrary", "arbitrary"),
        name="mixer",
    )(*args)


def _ffn_kernel(x1_ref, mod_ref, n2_ref, wu_ref, wg_ref, cw_ref, cb_ref, wd_ref, nf_ref, o_ref, acc_ref,
                *, seq, width):
    tm = x1_ref.shape[0]
    mod = mod_ref[...]
    sh2 = mod[:, 3 * D_MODEL:4 * D_MODEL]
    sc2 = mod[:, 4 * D_MODEL:5 * D_MODEL]
    ga2 = mod[:, 5 * D_MODEL:6 * D_MODEL]
    x1 = x1_ref[...]
    xb = (_rms(x1, n2_ref[...]) * (1.0 + sc2) + sh2).astype(BF16)

    rows = lax.broadcasted_iota(jnp.int32, (tm, FFN_TILE), 0)
    col_in_row = rows & (width - 1)
    has_left = col_in_row != 0
    has_right = col_in_row != width - 1
    n_rows = seq // width
    zrow = jnp.zeros((width, FFN_TILE), F32)

    def up_rows(z):
        return jnp.concatenate([zrow, z[:tm - width]], axis=0)

    def down_rows(z):
        return jnp.concatenate([z[width:], zrow], axis=0)

    n_tiles = D_FF // FFN_TILE

    def up_and_gate(f):
        fs = slice(f * FFN_TILE, (f + 1) * FFN_TILE)
        return _dot(xb, wu_ref[:, fs]), _dot(xb, wg_ref[:, fs])

    ahead = up_and_gate(0)
    for f in range(n_tiles):
        fs = slice(f * FFN_TILE, (f + 1) * FFN_TILE)
        up, gate = ahead
        if f + 1 < n_tiles:
            ahead = up_and_gate(f + 1)
        left = jnp.where(has_left, pltpu.roll(up, 1, axis=0), 0.0)
        right = jnp.where(has_right, pltpu.roll(up, tm - 1, axis=0), 0.0)
        w = lambda i, j: cw_ref[3 * i + j:3 * i + j + 1, fs]
        cv = left * w(1, 0) + up * w(1, 1) + right * w(1, 2) + cb_ref[:, fs]
        if n_rows > 1:
            cv = cv + (up_rows(left) * w(0, 0) + up_rows(up) * w(0, 1) + up_rows(right) * w(0, 2)
                       + down_rows(left) * w(2, 0) + down_rows(up) * w(2, 1) + down_rows(right) * w(2, 2))
        act = (cv * _sigmoid(cv) * gate).astype(BF16)
        part = _dot(act, wd_ref[fs, :])
        if f == 0:
            acc_ref[...] = part
        else:
            acc_ref[...] += part

    o_ref[...] = _rms(x1 + ga2 * acc_ref[...], nf_ref[...])


def _ffn(x1, mod3, n2g, wu, wg, cw, cb, wd, nfg, *, tm, n_rows, first_row, seq, width, mod_row):
    assert width & (width - 1) == 0 and tm % seq == 0 and (seq == width or tm == seq) and first_row % tm == 0
    t0 = first_row // tm
    return pl.pallas_call(
        functools.partial(_ffn_kernel, seq=seq, width=width),
        out_shape=jax.ShapeDtypeStruct((n_rows, D_MODEL), F32),
        grid=(n_rows // tm,),
        in_specs=[pl.BlockSpec((tm, D_MODEL), lambda i: (t0 + i, 0)),
                  pl.BlockSpec((None, 1, N_MOD * D_MODEL), lambda i: (mod_row(i), 0, 0)),
                  _const_spec(n2g.shape), _const_spec(wu.shape), _const_spec(wg.shape),
                  _const_spec(cw.shape), _const_spec(cb.shape), _const_spec(wd.shape),
                  _const_spec(nfg.shape)],
        out_specs=pl.BlockSpec((tm, D_MODEL), lambda i: (i, 0)),
        scratch_shapes=[pltpu.VMEM((tm, D_MODEL), F32)],
        compiler_params=_params("arbitrary"),
        name="ffn",
    )(x1, mod3, n2g, wu, wg, cw, cb, wd, nfg)


def kernel(x_prompt, x_sample, c, state_gla_fwd, state_gla_bwd, c_ctx, w_ada, b_ada, norm1_g, w_in, w_gk_f,
           b_gk_f, w_gk_b, b_gk_b, gla_norm_g, conv_mix_w, w_out, norm2_g, ffn_w_up, ffn_w_gate,
           ffn_conv_w, ffn_conv_b, ffn_w_down, normf_g):
    bp, lp, d = x_prompt.shape
    bs, ls, _ = x_sample.shape
    assert w_ada.shape[0] == 1 and d == D_MODEL and bs + 1 <= MOD_ROWS

    cc = jnp.zeros((MOD_ROWS, d), F32).at[0].set(c_ctx).at[1:1 + bs].set(c)
    mod3 = _ada(cc, w_ada[0], b_ada[0][None, :]).reshape(MOD_ROWS, 1, N_MOD * d)

    assert IN_SPLITS == (KEY_DIM, KEY_DIM, VAL_DIM, VAL_DIM, GATE_RANK, GATE_RANK) + (CONV_DIM,) * 5
    wgk = (jnp.zeros((MXU_LANES_V7X, 2 * KEY_DIM), F32)
           .at[:GATE_RANK, :KEY_DIM].set(w_gk_f[0]).at[GATE_RANK:2 * GATE_RANK, KEY_DIM:].set(w_gk_b[0])
           .astype(BF16))
    bgk = jnp.concatenate([b_gk_f[0], b_gk_b[0]])[None, :]
    wo = w_out[0].astype(BF16)
    wu, wg, wd = ffn_w_up[0].astype(BF16), ffn_w_gate[0].astype(BF16), ffn_w_down[0].astype(BF16)
    cw9 = ffn_conv_w[0].reshape(9, D_FF)
    cb = ffn_conv_b[0][None, :]
    n1g, n2g, nfg, gn = norm1_g[0][None, :], norm2_g[0][None, :], normf_g[None, :], gla_norm_g[0][None, :]

    xp, xs = x_prompt.reshape(-1, d), x_sample.reshape(-1, d)
    tiles_p = xp.shape[0] // INPROJ_ROWS
    seg = min(lp, ls, INPROJ_ROWS)
    proj = _inproj(xp, xs, mod3, n1g, w_in[0].T, wgk, bgk, tm=INPROJ_ROWS, seg=seg,
                   mod_row=lambda t: jnp.where(t < tiles_p, 0, 1 + ((t - tiles_p) * INPROJ_ROWS) // ls))

    mix = functools.partial(_mixer, mod3=mod3, proj=proj, conv_w=conv_mix_w[0], gn=gn, wo=wo, seg=seg)
    x1, new_f, new_b = mix(xp, states=None, x1_prev=None, seq=lp, heads=N_HEADS, first_row=0,
                           mod_row=lambda b: 0, has_state_out=True)
    x1, = mix(xs, states=(state_gla_fwd, state_gla_bwd), x1_prev=x1, seq=ls, heads=1, first_row=xp.shape[0],
              mod_row=lambda b: 1 + b, has_state_out=False)

    ffn = functools.partial(_ffn, x1, mod3, n2g, wu, wg, cw9, cb, wd, nfg, tm=FFN_ROWS)
    y_prompt = ffn(n_rows=xp.shape[0], first_row=0, seq=lp, width=lp, mod_row=lambda i: 0)
    y_sample = ffn(n_rows=xs.shape[0], first_row=xp.shape[0], seq=ls, width=GRID_W,
                   mod_row=lambda i: 1 + (i * FFN_ROWS) // ls)
    return y_prompt.reshape(x_prompt.shape), y_sample.reshape(x_sample.shape), new_f, new_b
```

```python
import functools

import jax
import jax.numpy as jnp
from jax import lax
from jax.experimental import pallas as pl
from jax.experimental.pallas import tpu as pltpu

D_MODEL = 1024
N_HEADS = 4
HEAD_K = 128
HEAD_V = 256
KEY_DIM = N_HEADS * HEAD_K
VAL_DIM = N_HEADS * HEAD_V
GATE_RANK = 16
GATE_NORMALIZER = 16.0
LOG2_E = 1.4426950408889634
CONV_DIM = D_MODEL
D_FF = 2816
N_MOD = 6
EPS = 1e-6
GRID_W = 64
IN_SPLITS = (KEY_DIM, KEY_DIM, VAL_DIM, VAL_DIM, GATE_RANK, GATE_RANK,
             CONV_DIM, CONV_DIM, CONV_DIM, VAL_DIM, CONV_DIM)

DIAG_BLOCK = 64
SUPER_BLOCK = 256
MXU_LANES_V7X = 128
FFN_TILE = 256
FFN_ROWS = 1024
INPROJ_ROWS = 512
PROMPT_SEQS_PER_STEP = 2
ADA_TILE = 1024
MOD_ROWS = 8
VMEM_LIMIT_BYTES = 56 * 1024 * 1024

F32 = jnp.float32
BF16 = jnp.bfloat16


def _dot(a, b):
    return jnp.dot(a, b, preferred_element_type=F32)


def _dot_nt(a, b):
    return lax.dot_general(a, b, (((1,), (1,)), ((), ())), preferred_element_type=F32)


def _dot_tn(a, b):
    return lax.dot_general(a, b, (((0,), (0,)), ((), ())), preferred_element_type=F32)


def _sigmoid(x):
    return 1.0 / (1.0 + jnp.exp(-x))


def _rms(x, g):
    return x * lax.rsqrt(jnp.mean(x * x, axis=-1, keepdims=True) + EPS) * g


def _const_spec(shape):
    nd = len(shape)
    return pl.BlockSpec(shape, lambda *_: (0,) * nd, pipeline_mode=pl.Buffered(1))


def _params(*sem):
    return pltpu.CompilerParams(dimension_semantics=sem, vmem_limit_bytes=VMEM_LIMIT_BYTES)


def _ada_kernel(c_ref, w_ref, b_ref, o_ref):
    c = c_ref[...]
    s = (c * _sigmoid(c)).astype(BF16)
    o_ref[...] = _dot(s, w_ref[...].astype(BF16)) + b_ref[...]


def _ada(cc, w_ada, b_ada):
    n = w_ada.shape[1]
    return pl.pallas_call(
        _ada_kernel,
        out_shape=jax.ShapeDtypeStruct((MOD_ROWS, n), F32),
        grid=(n // ADA_TILE,),
        in_specs=[pl.BlockSpec((MOD_ROWS, D_MODEL), lambda j: (0, 0)),
                  pl.BlockSpec((D_MODEL, ADA_TILE), lambda j: (0, j)),
                  pl.BlockSpec((1, ADA_TILE), lambda j: (0, j))],
        out_specs=pl.BlockSpec((MOD_ROWS, ADA_TILE), lambda j: (0, j)),
        compiler_params=_params("arbitrary"),
        name="ada",
    )(cc, w_ada, b_ada)


def _prefix_sum(x, pos, seg):
    s = 1
    while s < seg:
        x = x + jnp.where(pos >= s, pltpu.roll(x, s, axis=0), 0.0)
        s *= 2
    return x


def _suffix_sum(x, pos, seg):
    n, s = x.shape[0], 1
    while s < seg:
        x = x + jnp.where(pos < seg - s, pltpu.roll(x, n - s, axis=0), 0.0)
        s *= 2
    return x


W_BLOCK = 512
HEAD_COLS = 2 * KEY_DIM + 2 * VAL_DIM
TAIL_OFF = HEAD_COLS + 2 * GATE_RANK
TAIL_SHIFT = TAIL_OFF % MXU_LANES_V7X
N_HEAD_BLOCKS = HEAD_COLS // W_BLOCK
N_W_BLOCKS = N_HEAD_BLOCKS + 5 * CONV_DIM // W_BLOCK


def _inproj_kernel(xa_ref, xb_ref, mod_ref, n1_ref, wa_ref, wb_ref, wgk_ref, bgk_ref,
                   q_ref, k_ref, v_ref, og_ref, p_ref, cbg_ref, g_ref, w_scr, wcode_scr, *, tiles_a, seg):
    step = pl.program_id(0)

    @pl.when(step < N_HEAD_BLOCKS)
    def _():
        w_scr[step] = wa_ref[...].T.astype(BF16)
        lane = lax.broadcasted_iota(jnp.int32, (D_MODEL, MXU_LANES_V7X), 1)
        wcode_scr[...] = jnp.where(lane < 2 * GATE_RANK, wb_ref[...].T, 0.0).astype(BF16)

    @pl.when(jnp.logical_and(step >= N_HEAD_BLOCKS, step < N_W_BLOCKS))
    def _():
        rows = jnp.concatenate([wa_ref[TAIL_SHIFT:, :], wb_ref[:TAIL_SHIFT, :]], axis=0)
        w_scr[step] = rows.T.astype(BF16)

    @pl.when(step >= N_W_BLOCKS)
    def _():
        first_group = step - N_W_BLOCKS < tiles_a
        mod = mod_ref[...]
        sh1 = mod[:, 0:D_MODEL]
        sc1 = mod[:, D_MODEL:2 * D_MODEL]
        x = jnp.where(first_group, xa_ref[...], xb_ref[...])
        xn = (_rms(x, n1_ref[...]) * (1.0 + sc1) + sh1).astype(BF16)
        halves = range(CONV_DIM // W_BLOCK)
        head = lambda blk: _dot(xn, w_scr[blk])
        tail = lambda grp, j: _dot(xn, w_scr[N_HEAD_BLOCKS + grp * len(halves) + j])
        cols = lambda j: slice(j * W_BLOCK, (j + 1) * W_BLOCK)

        z = _dot(_dot(xn, wcode_scr[...]).astype(BF16), wgk_ref[...]) + bgk_ref[...]
        g = (jnp.minimum(z, 0.0) - jnp.log(1.0 + jnp.exp(-jnp.abs(z)))) * (LOG2_E / GATE_NORMALIZER)
        g_hi = g.astype(BF16)
        g_lo = (g - g_hi.astype(F32)).astype(BF16)

        q_ref[...] = (head(0) * (HEAD_K ** -0.5)).astype(BF16)
        k_ref[...] = head(1).astype(BF16)
        for j in halves:
            v_ref[:, cols(j)] = head(2 + j).astype(BF16)

        r = lax.broadcasted_iota(jnp.int32, (seg, seg), 0)
        c = lax.broadcasted_iota(jnp.int32, (seg, seg), 1)
        lower = jnp.where(c <= r, 1.0, 0.0).astype(BF16)
        upper = jnp.where(c >= r, 1.0, 0.0).astype(BF16)
        for s0 in range(0, g.shape[0], seg):
            rows = slice(s0, s0 + seg)
            for tri, cs in ((lower, slice(0, KEY_DIM)), (upper, slice(KEY_DIM, 2 * KEY_DIM))):
                terms = jnp.concatenate([g_hi[rows, cs], g_lo[rows, cs]], axis=0)
                g_ref[rows, cs] = _dot(jnp.concatenate([tri, tri], axis=1), terms)

        for j in halves:
            g_out = head(2 + len(halves) + j)
            og_ref[:, cols(j)] = (g_out * _sigmoid(g_out) * _sigmoid(tail(3, j))).astype(BF16)
            p_ref[:, cols(j)] = (tail(1, j) * tail(2, j)).astype(BF16)
            cbg_ref[:, cols(j)] = (_sigmoid(tail(4, j)) * tail(0, j)).astype(BF16)


def _inproj(xa, xb, mod3, n1g, w_in_t, wgk, bgk, *, tm, seg, mod_row):
    tiles_a, tiles_b = xa.shape[0] // tm, xb.shape[0] // tm
    n = xa.shape[0] + xb.shape[0]
    assert seg & (seg - 1) == 0 and tm % seg == 0
    assert KEY_DIM == W_BLOCK and HEAD_COLS % W_BLOCK == 0 and W_BLOCK % MXU_LANES_V7X == 0
    tile = lambda i: jnp.maximum(i - N_W_BLOCKS, 0)
    row = lambda i: (tile(i), 0)
    lanes_per_block = W_BLOCK // MXU_LANES_V7X
    wide = lambda w, dt: (jax.ShapeDtypeStruct((n, w), dt), pl.BlockSpec((tm, w), row))
    outs = [wide(KEY_DIM, BF16), wide(KEY_DIM, BF16), wide(VAL_DIM, BF16), wide(VAL_DIM, BF16),
            wide(CONV_DIM, BF16), wide(CONV_DIM, BF16), wide(2 * KEY_DIM, F32)]
    return pl.pallas_call(
        functools.partial(_inproj_kernel, tiles_a=tiles_a, seg=seg),
        out_shape=[o[0] for o in outs],
        grid=(N_W_BLOCKS + tiles_a + tiles_b,),
        in_specs=[pl.BlockSpec((tm, D_MODEL), lambda i: (jnp.minimum(tile(i), tiles_a - 1), 0)),
                  pl.BlockSpec((tm, D_MODEL), lambda i: (jnp.maximum(tile(i) - tiles_a, 0), 0)),
                  pl.BlockSpec((None, 1, N_MOD * D_MODEL), lambda i: (mod_row(tile(i)), 0, 0)),
                  _const_spec(n1g.shape),
                  pl.BlockSpec((W_BLOCK, D_MODEL), lambda i: (jnp.minimum(i, N_W_BLOCKS - 1), 0)),
                  pl.BlockSpec((MXU_LANES_V7X, D_MODEL),
                               lambda i: (jnp.where(i < N_HEAD_BLOCKS, HEAD_COLS // MXU_LANES_V7X,
                                                    (jnp.minimum(i, N_W_BLOCKS - 1) + 1) * lanes_per_block), 0)),
                  _const_spec(wgk.shape), _const_spec(bgk.shape)],
        out_specs=[o[1] for o in outs],
        scratch_shapes=[pltpu.VMEM((N_W_BLOCKS, D_MODEL, W_BLOCK), BF16),
                        pltpu.VMEM((D_MODEL, MXU_LANES_V7X), BF16)],
        compiler_params=_params("arbitrary"),
        name="in_proj",
    )(xa, xb, mod3, n1g, w_in_t, w_in_t, wgk, bgk)


def _cat(parts, axis=0):
    return parts[0] if len(parts) == 1 else jnp.concatenate(parts, axis=axis)


def _join_prefix(b, seg):
    parts = [b[:seg]]
    for c in range(1, b.shape[0] // seg):
        parts.append(b[c * seg:(c + 1) * seg] + parts[-1][seg - 1:seg, :])
    return _cat(parts)


def _join_suffix(b, seg):
    n = b.shape[0] // seg
    parts = [b[(n - 1) * seg:]]
    for c in range(n - 2, -1, -1):
        parts.insert(0, b[c * seg:(c + 1) * seg] + parts[0][0:1, :])
    return _cat(parts)


def _diag_args(b, off):
    return _cat([b[c:c + DIAG_BLOCK] - b[c + off:c + off + 1, :] for c in range(0, b.shape[0], DIAG_BLOCK)])


def _pair_args(bf, sb, s):
    lhs, rhs = [], []
    for e0 in range(0, bf.shape[0], 2 * s):
        o0 = e0 + s
        rf = bf[o0 - 1:o0, :]
        rb = sb[o0:o0 + 1, :]
        lhs += [sb[e0:o0] - rb, bf[o0:o0 + s] - rf]
        rhs += [rf - bf[e0:o0], rb - sb[o0:o0 + s]]
    return _cat(lhs), _cat(rhs)


def _mixer_kernel(*refs, seq, heads, seg, has_state_in, has_state_out):
    (q_ref, k_ref, v_ref, gf_ref, gb_ref, og_ref, p_ref, cbg_ref, cw_ref, gn_ref, wo_ref, x_ref,
     mod_ref) = refs[:13]
    refs = refs[13:]
    if has_state_in:
        s0f_ref, s0b_ref = refs[:2]
        refs = refs[2:]
    x1_ref = refs[0]
    refs = refs[1:]
    if has_state_out:
        sf_ref, sb_ref = refs[:2]
        refs = refs[2:]
    a_scr = refs[0]
    hsteps = N_HEADS // heads
    half = SUPER_BLOCK // 2
    n_seq = x_ref.shape[0] // seq

    rows_v = lax.broadcasted_iota(jnp.int32, (seq, HEAD_V), 0)
    r_loc = lax.broadcasted_iota(jnp.int32, (DIAG_BLOCK, half), 0)
    lane = lax.broadcasted_iota(jnp.int32, (DIAG_BLOCK, half), 1)
    ones = jnp.ones((HEAD_V, half), BF16)
    scale = lambda t, e: t * e.astype(BF16)

    mixes = [[] for _ in range(n_seq)]
    for si, h in [(si, h) for h in range(heads) for si in range(n_seq)]:
        rq = slice(si * seq, (si + 1) * seq)
        a_ref = a_scr.at[si * heads + h]
        ks = slice(h * HEAD_K, (h + 1) * HEAD_K)
        vs = slice(h * HEAD_V, (h + 1) * HEAD_V)
        q = q_ref[rq, ks]
        k = k_ref[rq, ks]
        v = v_ref[rq, vs]
        bf = _join_prefix(gf_ref[rq, ks], seg)
        sb = _join_suffix(gb_ref[rq, ks], seg)

        e0f = jnp.exp2(_diag_args(bf, DIAG_BLOCK // 2))
        e0b = jnp.exp2(_diag_args(sb, DIAG_BLOCK // 2 - 1))
        q0f, k0f = scale(q, e0f), scale(k, 1.0 / e0f)
        q0b, k0b = scale(q, e0b), scale(k, 1.0 / e0b)

        lhs_levels, rhs_levels = [], []
        s = DIAG_BLOCK
        while s < seq:
            la, ra = _pair_args(bf, sb, s)
            lhs_levels.append(scale(q, jnp.exp2(la)))
            rhs_levels.append(scale(k, jnp.exp2(ra)))
            s *= 2

        for base in range(0, seq, SUPER_BLOCK):
            rs = slice(base, base + SUPER_BLOCK)
            m0f = _dot_nt(q0f[rs], k0f[rs])
            m0b = _dot_nt(q0b[rs], k0b[rs])
            m1 = _dot_nt(lhs_levels[0][rs], rhs_levels[0][rs])
            m2 = _dot_nt(lhs_levels[1][rs], rhs_levels[1][rs])
            for bi in range(SUPER_BLOCK // DIAG_BLOCK):
                rr = slice(bi * DIAG_BLOCK, (bi + 1) * DIAG_BLOCK)
                for lt in range(2):
                    cc = slice(lt * half, (lt + 1) * half)
                    if lt == bi // 2:
                        c_loc = lane - DIAG_BLOCK * (bi % 2)
                        in_diag = (lane >= DIAG_BLOCK) if bi % 2 else (lane < DIAG_BLOCK)
                        diag = (jnp.where(c_loc <= r_loc, m0f[rr, cc], 0.0)
                                + jnp.where(c_loc >= r_loc, m0b[rr, cc], 0.0))
                        piece = jnp.where(in_diag, diag, m1[rr, cc])
                    else:
                        piece = m2[rr, cc]
                    a_ref[base + bi * DIAG_BLOCK:base + (bi + 1) * DIAG_BLOCK,
                          base + lt * half:base + (lt + 1) * half] = piece.astype(BF16)
        s = SUPER_BLOCK
        lev = 2
        while s < seq:
            lhs, rhs = lhs_levels[lev], rhs_levels[lev]
            for e0 in range(0, seq, 2 * s):
                ev = slice(e0, e0 + s)
                od = slice(e0 + s, e0 + 2 * s)
                a_ref[od, ev] = _dot_nt(lhs[od], rhs[ev]).astype(BF16)
                a_ref[ev, od] = _dot_nt(lhs[ev], rhs[od]).astype(BF16)
            s *= 2
            lev += 1

        o = _dot(a_ref[...], v)
        if has_state_in:
            qi = jnp.concatenate([scale(q, jnp.exp2(bf)), scale(q, jnp.exp2(sb))], axis=1)
            s0 = jnp.concatenate([s0f_ref[si, h].astype(BF16), s0b_ref[si, h].astype(BF16)], axis=0)
            o = o + _dot(qi, s0)
        if has_state_out:
            sf_ref[si, h] = _dot_tn(scale(k, jnp.exp2(bf[seq - 1:seq, :] - bf)), v)
            sb_ref[si, h] = _dot_tn(scale(k, jnp.exp2(sb[0:1, :] - sb)), v)

        inv = lax.rsqrt(_dot((o * o).astype(BF16), ones) * (1.0 / HEAD_V) + EPS)
        o = jnp.concatenate([o[:, :half] * inv, o[:, half:] * inv], axis=1) * gn_ref[...]
        p = p_ref[rq, vs].astype(F32)
        conv = (jnp.where(rows_v >= 1, pltpu.roll(p, 1, axis=0), 0.0) * cw_ref[0:1, vs] + p * cw_ref[1:2, vs]
                + jnp.where(rows_v < seq - 1, pltpu.roll(p, seq - 1, axis=0), 0.0) * cw_ref[2:3, vs])
        mixes[si].append(og_ref[rq, vs] * o.astype(BF16) + cbg_ref[rq, vs] * conv.astype(BF16))

    contrib = _dot(_cat([_cat(m, axis=1) for m in mixes], axis=0), wo_ref[...])
    ga1 = mod_ref[:, 2 * D_MODEL:3 * D_MODEL]
    if hsteps == 1:
        x1_ref[...] = x_ref[...] + ga1 * contrib
    else:
        hstep = pl.program_id(1)

        @pl.when(hstep == 0)
        def _():
            x1_ref[...] = x_ref[...] + ga1 * contrib

        @pl.when(hstep != 0)
        def _():
            x1_ref[...] += ga1 * contrib


def _mixer(x, mod3, proj, conv_w, gn, wo, states, *, seq, heads, seqs_per_step, seg, first_row, mod_row,
           has_state_out):
    q, k, v, og, p, cbg, g = proj
    n = x.shape[0]
    rows = seqs_per_step * seq
    hsteps = N_HEADS // heads
    has_state_in = states is not None
    assert seq % SUPER_BLOCK == 0 and seq % seg == 0 and first_row % rows == 0 and n % rows == 0
    b0 = first_row // rows
    col = lambda b, h: (b0 + b, h)
    st_block = (seqs_per_step, None, heads, HEAD_K, HEAD_V)
    st_map = lambda b, h: (b, 0, h, 0, 0)
    in_specs = [pl.BlockSpec((rows, heads * HEAD_K), col),
                pl.BlockSpec((rows, heads * HEAD_K), col),
                pl.BlockSpec((rows, heads * HEAD_V), col),
                pl.BlockSpec((rows, heads * HEAD_K), col),
                pl.BlockSpec((rows, heads * HEAD_K), lambda b, h: (b0 + b, hsteps + h)),
                pl.BlockSpec((rows, heads * HEAD_V), col),
                pl.BlockSpec((rows, heads * HEAD_V), col),
                pl.BlockSpec((rows, heads * HEAD_V), col),
                pl.BlockSpec((3, heads * HEAD_V), lambda b, h: (0, h)),
                pl.BlockSpec((1, HEAD_V), lambda b, h: (0, 0)),
                pl.BlockSpec((heads * HEAD_V, D_MODEL), lambda b, h: (h, 0)),
                pl.BlockSpec((rows, D_MODEL), lambda b, h: (b, 0)),
                pl.BlockSpec((None, 1, N_MOD * D_MODEL), lambda b, h: (mod_row(b), 0, 0))]
    args = [q, k, v, g, g, og, p, cbg, conv_w, gn, wo, x, mod3]
    if has_state_in:
        in_specs += [pl.BlockSpec(st_block, st_map)] * 2
        args += list(states)
    out_shape = [jax.ShapeDtypeStruct((n, D_MODEL), F32)]
    out_specs = [pl.BlockSpec((rows, D_MODEL), lambda b, h: (b, 0))]
    if has_state_out:
        out_shape += [jax.ShapeDtypeStruct((n // seq, 1, N_HEADS, HEAD_K, HEAD_V), F32)] * 2
        out_specs += [pl.BlockSpec(st_block, st_map)] * 2
    return pl.pallas_call(
        functools.partial(_mixer_kernel, seq=seq, heads=heads, seg=seg, has_state_in=has_state_in,
                          has_state_out=has_state_out),
        out_shape=out_shape,
        grid=(n // rows, hsteps),
        in_specs=in_specs,
        out_specs=out_specs,
        scratch_shapes=[pltpu.VMEM((seqs_per_step * heads, seq, seq), BF16)],
        compiler_params=_params("arbitrary", "arbitrary"),
        name="mixer",
    )(*args)


def _ffn_kernel(x1_ref, mod_ref, n2_ref, wu_ref, wg_ref, cw_ref, cb_ref, wd_ref, nf_ref, o_ref, acc_ref,
                *, seq, width):
    tm = x1_ref.shape[0]
    mod = mod_ref[...]
    sh2 = mod[:, 3 * D_MODEL:4 * D_MODEL]
    sc2 = mod[:, 4 * D_MODEL:5 * D_MODEL]
    ga2 = mod[:, 5 * D_MODEL:6 * D_MODEL]
    x1 = x1_ref[...]
    xb = (_rms(x1, n2_ref[...]) * (1.0 + sc2) + sh2).astype(BF16)

    rows = lax.broadcasted_iota(jnp.int32, (tm, FFN_TILE), 0)
    col_in_row = rows & (width - 1)
    has_left = col_in_row != 0
    has_right = col_in_row != width - 1
    n_rows = seq // width
    zrow = jnp.zeros((width, FFN_TILE), F32)

    def up_rows(z):
        return jnp.concatenate([zrow, z[:tm - width]], axis=0)

    def down_rows(z):
        return jnp.concatenate([z[width:], zrow], axis=0)

    n_tiles = D_FF // FFN_TILE

    def up_and_gate(f):
        fs = slice(f * FFN_TILE, (f + 1) * FFN_TILE)
        return _dot(xb, wu_ref[:, fs]), _dot(xb, wg_ref[:, fs])

    ahead = up_and_gate(0)
    for f in range(n_tiles):
        fs = slice(f * FFN_TILE, (f + 1) * FFN_TILE)
        up, gate = ahead
        if f + 1 < n_tiles:
            ahead = up_and_gate(f + 1)
        left = jnp.where(has_left, pltpu.roll(up, 1, axis=0), 0.0)
        right = jnp.where(has_right, pltpu.roll(up, tm - 1, axis=0), 0.0)
        w = lambda i, j: cw_ref[3 * i + j:3 * i + j + 1, fs]
        cv = left * w(1, 0) + up * w(1, 1) + right * w(1, 2) + cb_ref[:, fs]
        if n_rows > 1:
            cv = cv + (up_rows(left) * w(0, 0) + up_rows(up) * w(0, 1) + up_rows(right) * w(0, 2)
                       + down_rows(left) * w(2, 0) + down_rows(up) * w(2, 1) + down_rows(right) * w(2, 2))
        act = (cv * _sigmoid(cv) * gate).astype(BF16)
        part = _dot(act, wd_ref[fs, :])
        if f == 0:
            acc_ref[...] = part
        else:
            acc_ref[...] += part

    o_ref[...] = _rms(x1 + ga2 * acc_ref[...], nf_ref[...])


def _ffn(x1, mod3, n2g, wu, wg, cw, cb, wd, nfg, *, tm, seq, width, mod_row):
    assert width & (width - 1) == 0 and tm % seq == 0 and (seq == width or tm == seq)
    return pl.pallas_call(
        functools.partial(_ffn_kernel, seq=seq, width=width),
        out_shape=jax.ShapeDtypeStruct(x1.shape, F32),
        grid=(x1.shape[0] // tm,),
        in_specs=[pl.BlockSpec((tm, D_MODEL), lambda i: (i, 0)),
                  pl.BlockSpec((None, 1, N_MOD * D_MODEL), lambda i: (mod_row(i), 0, 0)),
                  _const_spec(n2g.shape), _const_spec(wu.shape), _const_spec(wg.shape),
                  _const_spec(cw.shape), _const_spec(cb.shape), _const_spec(wd.shape),
                  _const_spec(nfg.shape)],
        out_specs=pl.BlockSpec((tm, D_MODEL), lambda i: (i, 0)),
        scratch_shapes=[pltpu.VMEM((tm, D_MODEL), F32)],
        compiler_params=_params("arbitrary"),
        name="ffn",
    )(x1, mod3, n2g, wu, wg, cw, cb, wd, nfg)


def kernel(x_prompt, x_sample, c, state_gla_fwd, state_gla_bwd, c_ctx, w_ada, b_ada, norm1_g, w_in, w_gk_f,
           b_gk_f, w_gk_b, b_gk_b, gla_norm_g, conv_mix_w, w_out, norm2_g, ffn_w_up, ffn_w_gate,
           ffn_conv_w, ffn_conv_b, ffn_w_down, normf_g):
    bp, lp, d = x_prompt.shape
    bs, ls, _ = x_sample.shape
    assert w_ada.shape[0] == 1 and d == D_MODEL and bs + 1 <= MOD_ROWS

    cc = jnp.zeros((MOD_ROWS, d), F32).at[0].set(c_ctx).at[1:1 + bs].set(c)
    mod3 = _ada(cc, w_ada[0], b_ada[0][None, :]).reshape(MOD_ROWS, 1, N_MOD * d)

    assert IN_SPLITS == (KEY_DIM, KEY_DIM, VAL_DIM, VAL_DIM, GATE_RANK, GATE_RANK) + (CONV_DIM,) * 5
    wgk = (jnp.zeros((MXU_LANES_V7X, 2 * KEY_DIM), F32)
           .at[:GATE_RANK, :KEY_DIM].set(w_gk_f[0]).at[GATE_RANK:2 * GATE_RANK, KEY_DIM:].set(w_gk_b[0])
           .astype(BF16))
    bgk = jnp.concatenate([b_gk_f[0], b_gk_b[0]])[None, :]
    wo = w_out[0].astype(BF16)
    wu, wg, wd = ffn_w_up[0].astype(BF16), ffn_w_gate[0].astype(BF16), ffn_w_down[0].astype(BF16)
    cw9 = ffn_conv_w[0].reshape(9, D_FF)
    cb = ffn_conv_b[0][None, :]
    n1g, n2g, nfg, gn = norm1_g[0][None, :], norm2_g[0][None, :], normf_g[None, :], gla_norm_g[0][None, :]

    xp, xs = x_prompt.reshape(-1, d), x_sample.reshape(-1, d)
    tiles_p = xp.shape[0] // INPROJ_ROWS
    seg = min(lp, ls, INPROJ_ROWS)
    proj = _inproj(xp, xs, mod3, n1g, w_in[0].T, wgk, bgk, tm=INPROJ_ROWS, seg=seg,
                   mod_row=lambda t: jnp.where(t < tiles_p, 0, 1 + ((t - tiles_p) * INPROJ_ROWS) // ls))

    mix = functools.partial(_mixer, mod3=mod3, proj=proj, conv_w=conv_mix_w[0], gn=gn, wo=wo, seg=seg)
    x1p, new_f, new_b = mix(xp, states=None, seq=lp, heads=N_HEADS, seqs_per_step=PROMPT_SEQS_PER_STEP,
                            first_row=0, mod_row=lambda b: 0, has_state_out=True)
    x1s, = mix(xs, states=(state_gla_fwd, state_gla_bwd), seq=ls, heads=1, seqs_per_step=1,
               first_row=xp.shape[0], mod_row=lambda b: 1 + b, has_state_out=False)

    ffn = functools.partial(_ffn, mod3=mod3, n2g=n2g, wu=wu, wg=wg, cw=cw9, cb=cb, wd=wd, nfg=nfg, tm=FFN_ROWS)
    y_prompt = ffn(x1p, seq=lp, width=lp, mod_row=lambda i: 0)
    y_sample = ffn(x1s, seq=ls, width=GRID_W, mod_row=lambda i: 1 + (i * FFN_ROWS) // ls)
    return y_prompt.reshape(x_prompt.shape), y_sample.reshape(x_sample.shape), new_f, new_b
```

```python
import functools

import jax
import jax.numpy as jnp
from jax import lax
from jax.experimental import pallas as pl
from jax.experimental.pallas import tpu as pltpu

D_MODEL = 1024
N_HEADS = 4
HEAD_K = 128
HEAD_V = 256
KEY_DIM = N_HEADS * HEAD_K
VAL_DIM = N_HEADS * HEAD_V
GATE_RANK = 16
GATE_NORMALIZER = 16.0
LOG2_E = 1.4426950408889634
CONV_DIM = D_MODEL
D_FF = 2816
N_MOD = 6
EPS = 1e-6
GRID_W = 64
IN_SPLITS = (KEY_DIM, KEY_DIM, VAL_DIM, VAL_DIM, GATE_RANK, GATE_RANK,
             CONV_DIM, CONV_DIM, CONV_DIM, VAL_DIM, CONV_DIM)

DIAG_BLOCK = 64
SUPER_BLOCK = 256
MXU_LANES_V7X = 128
FFN_TILE = 256
FFN_ROWS = 1024
INPROJ_ROWS = 512
PROMPT_SEQS_PER_STEP = 2
ADA_TILE = 1024
MOD_ROWS = 8
VMEM_LIMIT_BYTES = 56 * 1024 * 1024

F32 = jnp.float32
BF16 = jnp.bfloat16


def _dot(a, b):
    return jnp.dot(a, b, preferred_element_type=F32)


def _dot_nt(a, b):
    return lax.dot_general(a, b, (((1,), (1,)), ((), ())), preferred_element_type=F32)


def _dot_tn(a, b):
    return lax.dot_general(a, b, (((0,), (0,)), ((), ())), preferred_element_type=F32)


def _sigmoid(x):
    return 1.0 / (1.0 + jnp.exp(-x))


def _rms(x, g):
    return x * lax.rsqrt(jnp.mean(x * x, axis=-1, keepdims=True) + EPS) * g


def _const_spec(shape):
    nd = len(shape)
    return pl.BlockSpec(shape, lambda *_: (0,) * nd, pipeline_mode=pl.Buffered(1))


def _params(*sem):
    return pltpu.CompilerParams(dimension_semantics=sem, vmem_limit_bytes=VMEM_LIMIT_BYTES)


def _ada_kernel(c_ref, w_ref, b_ref, o_ref):
    c = c_ref[...]
    s = (c * _sigmoid(c)).astype(BF16)
    o_ref[...] = _dot(s, w_ref[...].astype(BF16)) + b_ref[...]


def _ada(cc, w_ada, b_ada):
    n = w_ada.shape[1]
    return pl.pallas_call(
        _ada_kernel,
        out_shape=jax.ShapeDtypeStruct((MOD_ROWS, n), F32),
        grid=(n // ADA_TILE,),
        in_specs=[pl.BlockSpec((MOD_ROWS, D_MODEL), lambda j: (0, 0)),
                  pl.BlockSpec((D_MODEL, ADA_TILE), lambda j: (0, j)),
                  pl.BlockSpec((1, ADA_TILE), lambda j: (0, j))],
        out_specs=pl.BlockSpec((MOD_ROWS, ADA_TILE), lambda j: (0, j)),
        compiler_params=_params("arbitrary"),
        name="ada",
    )(cc, w_ada, b_ada)


def _prefix_sum(x, pos, seg):
    s = 1
    while s < seg:
        x = x + jnp.where(pos >= s, pltpu.roll(x, s, axis=0), 0.0)
        s *= 2
    return x


def _suffix_sum(x, pos, seg):
    n, s = x.shape[0], 1
    while s < seg:
        x = x + jnp.where(pos < seg - s, pltpu.roll(x, n - s, axis=0), 0.0)
        s *= 2
    return x


W_BLOCK = 512
HEAD_COLS = 2 * KEY_DIM + 2 * VAL_DIM
TAIL_OFF = HEAD_COLS + 2 * GATE_RANK
TAIL_SHIFT = TAIL_OFF % MXU_LANES_V7X
N_HEAD_BLOCKS = HEAD_COLS // W_BLOCK
N_W_BLOCKS = N_HEAD_BLOCKS + 5 * CONV_DIM // W_BLOCK


def _inproj_kernel(xa_ref, xb_ref, mod_ref, n1_ref, wa_ref, wb_ref, wgk_ref, bgk_ref,
                   q_ref, k_ref, v_ref, og_ref, p_ref, cbg_ref, g_ref, w_scr, wcode_scr, *, tiles_a, seg):
    step = pl.program_id(0)

    @pl.when(step < N_HEAD_BLOCKS)
    def _():
        w_scr[step] = wa_ref[...].T.astype(BF16)
        lane = lax.broadcasted_iota(jnp.int32, (D_MODEL, MXU_LANES_V7X), 1)
        wcode_scr[...] = jnp.where(lane < 2 * GATE_RANK, wb_ref[...].T, 0.0).astype(BF16)

    @pl.when(jnp.logical_and(step >= N_HEAD_BLOCKS, step < N_W_BLOCKS))
    def _():
        rows = jnp.concatenate([wa_ref[TAIL_SHIFT:, :], wb_ref[:TAIL_SHIFT, :]], axis=0)
        w_scr[step] = rows.T.astype(BF16)

    @pl.when(step >= N_W_BLOCKS)
    def _():
        first_group = step - N_W_BLOCKS < tiles_a
        mod = mod_ref[...]
        sh1 = mod[:, 0:D_MODEL]
        sc1 = mod[:, D_MODEL:2 * D_MODEL]
        x = jnp.where(first_group, xa_ref[...], xb_ref[...])
        xn = (_rms(x, n1_ref[...]) * (1.0 + sc1) + sh1).astype(BF16)
        halves = range(CONV_DIM // W_BLOCK)
        head = lambda blk: _dot(xn, w_scr[blk])
        tail = lambda grp, j: _dot(xn, w_scr[N_HEAD_BLOCKS + grp * len(halves) + j])
        cols = lambda j: slice(j * W_BLOCK, (j + 1) * W_BLOCK)

        z = _dot(_dot(xn, wcode_scr[...]).astype(BF16), wgk_ref[...]) + bgk_ref[...]
        g = (jnp.minimum(z, 0.0) - jnp.log(1.0 + jnp.exp(-jnp.abs(z)))) * (LOG2_E / GATE_NORMALIZER)
        g_hi = g.astype(BF16)
        g_lo = (g - g_hi.astype(F32)).astype(BF16)

        q_ref[...] = (head(0) * (HEAD_K ** -0.5)).astype(BF16)
        k_ref[...] = head(1).astype(BF16)
        for j in halves:
            v_ref[:, cols(j)] = head(2 + j).astype(BF16)

        r = lax.broadcasted_iota(jnp.int32, (seg, seg), 0)
        c = lax.broadcasted_iota(jnp.int32, (seg, seg), 1)
        lower = jnp.where(c <= r, 1.0, 0.0).astype(BF16)
        upper = jnp.where(c >= r, 1.0, 0.0).astype(BF16)
        for s0 in range(0, g.shape[0], seg):
            rows = slice(s0, s0 + seg)
            for tri, cs in ((lower, slice(0, KEY_DIM)), (upper, slice(KEY_DIM, 2 * KEY_DIM))):
                terms = jnp.concatenate([g_hi[rows, cs], g_lo[rows, cs]], axis=0)
                g_ref[rows, cs] = _dot(jnp.concatenate([tri, tri], axis=1), terms)

        for j in halves:
            g_out = head(2 + len(halves) + j)
            og_ref[:, cols(j)] = (g_out * _sigmoid(g_out) * _sigmoid(tail(3, j))).astype(BF16)
            p_ref[:, cols(j)] = (tail(1, j) * tail(2, j)).astype(BF16)
            cbg_ref[:, cols(j)] = (_sigmoid(tail(4, j)) * tail(0, j)).astype(BF16)


def _inproj(xa, xb, mod3, n1g, w_in_t, wgk, bgk, *, tm, seg, mod_row):
    tiles_a, tiles_b = xa.shape[0] // tm, xb.shape[0] // tm
    n = xa.shape[0] + xb.shape[0]
    assert seg & (seg - 1) == 0 and tm % seg == 0
    assert KEY_DIM == W_BLOCK and HEAD_COLS % W_BLOCK == 0 and W_BLOCK % MXU_LANES_V7X == 0
    tile = lambda i: jnp.maximum(i - N_W_BLOCKS, 0)
    row = lambda i: (tile(i), 0)
    lanes_per_block = W_BLOCK // MXU_LANES_V7X
    wide = lambda w, dt: (jax.ShapeDtypeStruct((n, w), dt), pl.BlockSpec((tm, w), row))
    outs = [wide(KEY_DIM, BF16), wide(KEY_DIM, BF16), wide(VAL_DIM, BF16), wide(VAL_DIM, BF16),
            wide(CONV_DIM, BF16), wide(CONV_DIM, BF16), wide(2 * KEY_DIM, F32)]
    return pl.pallas_call(
        functools.partial(_inproj_kernel, tiles_a=tiles_a, seg=seg),
        out_shape=[o[0] for o in outs],
        grid=(N_W_BLOCKS + tiles_a + tiles_b,),
        in_specs=[pl.BlockSpec((tm, D_MODEL), lambda i: (jnp.minimum(tile(i), tiles_a - 1), 0)),
                  pl.BlockSpec((tm, D_MODEL), lambda i: (jnp.maximum(tile(i) - tiles_a, 0), 0)),
                  pl.BlockSpec((None, 1, N_MOD * D_MODEL), lambda i: (mod_row(tile(i)), 0, 0)),
                  _const_spec(n1g.shape),
                  pl.BlockSpec((W_BLOCK, D_MODEL), lambda i: (jnp.minimum(i, N_W_BLOCKS - 1), 0)),
                  pl.BlockSpec((MXU_LANES_V7X, D_MODEL),
                               lambda i: (jnp.where(i < N_HEAD_BLOCKS, HEAD_COLS // MXU_LANES_V7X,
                                                    (jnp.minimum(i, N_W_BLOCKS - 1) + 1) * lanes_per_block), 0)),
                  _const_spec(wgk.shape), _const_spec(bgk.shape)],
        out_specs=[o[1] for o in outs],
        scratch_shapes=[pltpu.VMEM((N_W_BLOCKS, D_MODEL, W_BLOCK), BF16),
                        pltpu.VMEM((D_MODEL, MXU_LANES_V7X), BF16)],
        compiler_params=_params("arbitrary"),
        name="in_proj",
    )(xa, xb, mod3, n1g, w_in_t, w_in_t, wgk, bgk)


def _cat(parts, axis=0):
    return parts[0] if len(parts) == 1 else jnp.concatenate(parts, axis=axis)


def _join_prefix(b, seg):
    parts = [b[:seg]]
    for c in range(1, b.shape[0] // seg):
        parts.append(b[c * seg:(c + 1) * seg] + parts[-1][seg - 1:seg, :])
    return _cat(parts)


def _join_suffix(b, seg):
    n = b.shape[0] // seg
    parts = [b[(n - 1) * seg:]]
    for c in range(n - 2, -1, -1):
        parts.insert(0, b[c * seg:(c + 1) * seg] + parts[0][0:1, :])
    return _cat(parts)


def _diag_args(b, off):
    return _cat([b[c:c + DIAG_BLOCK] - b[c + off:c + off + 1, :] for c in range(0, b.shape[0], DIAG_BLOCK)])


def _pair_args(bf, sb, s):
    lhs, rhs = [], []
    for e0 in range(0, bf.shape[0], 2 * s):
        o0 = e0 + s
        rf = bf[o0 - 1:o0, :]
        rb = sb[o0:o0 + 1, :]
        lhs += [sb[e0:o0] - rb, bf[o0:o0 + s] - rf]
        rhs += [rf - bf[e0:o0], rb - sb[o0:o0 + s]]
    return _cat(lhs), _cat(rhs)


def _mixer_kernel(*refs, seq, heads, seg, has_state_in, has_state_out, has_order_token):
    (q_ref, k_ref, v_ref, gf_ref, gb_ref, og_ref, p_ref, cbg_ref, cw_ref, gn_ref, wo_ref, x_ref,
     mod_ref) = refs[:13]
    refs = refs[13:]
    if has_state_in:
        s0f_ref, s0b_ref = refs[:2]
        refs = refs[2:]
    if has_order_token:
        refs = refs[1:]
    x1_ref = refs[0]
    refs = refs[1:]
    if has_state_out:
        sf_ref, sb_ref = refs[:2]
        refs = refs[2:]
    a_scr = refs[0]
    hsteps = N_HEADS // heads
    half = SUPER_BLOCK // 2
    n_seq = x_ref.shape[0] // seq

    rows_v = lax.broadcasted_iota(jnp.int32, (seq, HEAD_V), 0)
    r_loc = lax.broadcasted_iota(jnp.int32, (DIAG_BLOCK, half), 0)
    lane = lax.broadcasted_iota(jnp.int32, (DIAG_BLOCK, half), 1)
    ones = jnp.ones((HEAD_V, half), BF16)
    scale = lambda t, e: t * e.astype(BF16)

    mixes = [[] for _ in range(n_seq)]
    for si, h in [(si, h) for h in range(heads) for si in range(n_seq)]:
        rq = slice(si * seq, (si + 1) * seq)
        a_ref = a_scr.at[si * heads + h]
        ks = slice(h * HEAD_K, (h + 1) * HEAD_K)
        vs = slice(h * HEAD_V, (h + 1) * HEAD_V)
        q = q_ref[rq, ks]
        k = k_ref[rq, ks]
        v = v_ref[rq, vs]
        bf = _join_prefix(gf_ref[rq, ks], seg)
        sb = _join_suffix(gb_ref[rq, ks], seg)

        e0f = jnp.exp2(_diag_args(bf, DIAG_BLOCK // 2))
        e0b = jnp.exp2(_diag_args(sb, DIAG_BLOCK // 2 - 1))
        q0f, k0f = scale(q, e0f), scale(k, 1.0 / e0f)
        q0b, k0b = scale(q, e0b), scale(k, 1.0 / e0b)

        lhs_levels, rhs_levels = [], []
        s = DIAG_BLOCK
        while s < seq:
            la, ra = _pair_args(bf, sb, s)
            lhs_levels.append(scale(q, jnp.exp2(la)))
            rhs_levels.append(scale(k, jnp.exp2(ra)))
            s *= 2

        for base in range(0, seq, SUPER_BLOCK):
            rs = slice(base, base + SUPER_BLOCK)
            m0f = _dot_nt(q0f[rs], k0f[rs])
            m0b = _dot_nt(q0b[rs], k0b[rs])
            m1 = _dot_nt(lhs_levels[0][rs], rhs_levels[0][rs])
            m2 = _dot_nt(lhs_levels[1][rs], rhs_levels[1][rs])
            for bi in range(SUPER_BLOCK // DIAG_BLOCK):
                rr = slice(bi * DIAG_BLOCK, (bi + 1) * DIAG_BLOCK)
                for lt in range(2):
                    cc = slice(lt * half, (lt + 1) * half)
                    if lt == bi // 2:
                        c_loc = lane - DIAG_BLOCK * (bi % 2)
                        in_diag = (lane >= DIAG_BLOCK) if bi % 2 else (lane < DIAG_BLOCK)
                        diag = (jnp.where(c_loc <= r_loc, m0f[rr, cc], 0.0)
                                + jnp.where(c_loc >= r_loc, m0b[rr, cc], 0.0))
                        piece = jnp.where(in_diag, diag, m1[rr, cc])
                    else:
                        piece = m2[rr, cc]
                    a_ref[base + bi * DIAG_BLOCK:base + (bi + 1) * DIAG_BLOCK,
                          base + lt * half:base + (lt + 1) * half] = piece.astype(BF16)
        s = SUPER_BLOCK
        lev = 2
        while s < seq:
            lhs, rhs = lhs_levels[lev], rhs_levels[lev]
            for e0 in range(0, seq, 2 * s):
                ev = slice(e0, e0 + s)
                od = slice(e0 + s, e0 + 2 * s)
                a_ref[od, ev] = _dot_nt(lhs[od], rhs[ev]).astype(BF16)
                a_ref[ev, od] = _dot_nt(lhs[ev], rhs[od]).astype(BF16)
            s *= 2
            lev += 1

        o = _dot(a_ref[...], v)
        if has_state_in:
            qi = jnp.concatenate([scale(q, jnp.exp2(bf)), scale(q, jnp.exp2(sb))], axis=1)
            s0 = jnp.concatenate([s0f_ref[si, h].astype(BF16), s0b_ref[si, h].astype(BF16)], axis=0)
            o = o + _dot(qi, s0)
        if has_state_out:
            sf_ref[si, h] = _dot_tn(scale(k, jnp.exp2(bf[seq - 1:seq, :] - bf)), v)
            sb_ref[si, h] = _dot_tn(scale(k, jnp.exp2(sb[0:1, :] - sb)), v)

        inv = lax.rsqrt(_dot((o * o).astype(BF16), ones) * (1.0 / HEAD_V) + EPS)
        o = jnp.concatenate([o[:, :half] * inv, o[:, half:] * inv], axis=1) * gn_ref[...]
        p = p_ref[rq, vs].astype(F32)
        conv = (jnp.where(rows_v >= 1, pltpu.roll(p, 1, axis=0), 0.0) * cw_ref[0:1, vs] + p * cw_ref[1:2, vs]
                + jnp.where(rows_v < seq - 1, pltpu.roll(p, seq - 1, axis=0), 0.0) * cw_ref[2:3, vs])
        mixes[si].append(og_ref[rq, vs] * o.astype(BF16) + cbg_ref[rq, vs] * conv.astype(BF16))

    contrib = _dot(_cat([_cat(m, axis=1) for m in mixes], axis=0), wo_ref[...])
    ga1 = mod_ref[:, 2 * D_MODEL:3 * D_MODEL]
    if hsteps == 1:
        x1_ref[...] = x_ref[...] + ga1 * contrib
    else:
        hstep = pl.program_id(1)

        @pl.when(hstep == 0)
        def _():
            x1_ref[...] = x_ref[...] + ga1 * contrib

        @pl.when(hstep != 0)
        def _():
            x1_ref[...] += ga1 * contrib


def _mixer(x, mod3, proj, conv_w, gn, wo, states, *, seq, heads, seqs_per_step, seg, first_row, mod_row,
           has_state_out, run_after=None):
    q, k, v, og, p, cbg, g = proj
    n = x.shape[0]
    rows = seqs_per_step * seq
    hsteps = N_HEADS // heads
    has_state_in = states is not None
    assert seq % SUPER_BLOCK == 0 and seq % seg == 0 and first_row % rows == 0 and n % rows == 0
    b0 = first_row // rows
    col = lambda b, h: (b0 + b, h)
    st_block = (seqs_per_step, None, heads, HEAD_K, HEAD_V)
    st_map = lambda b, h: (b, 0, h, 0, 0)
    in_specs = [pl.BlockSpec((rows, heads * HEAD_K), col),
                pl.BlockSpec((rows, heads * HEAD_K), col),
                pl.BlockSpec((rows, heads * HEAD_V), col),
                pl.BlockSpec((rows, heads * HEAD_K), col),
                pl.BlockSpec((rows, heads * HEAD_K), lambda b, h: (b0 + b, hsteps + h)),
                pl.BlockSpec((rows, heads * HEAD_V), col),
                pl.BlockSpec((rows, heads * HEAD_V), col),
                pl.BlockSpec((rows, heads * HEAD_V), col),
                pl.BlockSpec((3, heads * HEAD_V), lambda b, h: (0, h)),
                pl.BlockSpec((1, HEAD_V), lambda b, h: (0, 0)),
                pl.BlockSpec((heads * HEAD_V, D_MODEL), lambda b, h: (h, 0)),
                pl.BlockSpec((rows, D_MODEL), lambda b, h: (b, 0)),
                pl.BlockSpec((None, 1, N_MOD * D_MODEL), lambda b, h: (mod_row(b), 0, 0))]
    args = [q, k, v, g, g, og, p, cbg, conv_w, gn, wo, x, mod3]
    if has_state_in:
        in_specs += [pl.BlockSpec(st_block, st_map)] * 2
        args += list(states)
    if run_after is not None:
        in_specs.append(pl.BlockSpec(memory_space=pl.ANY))
        args.append(run_after)
    out_shape = [jax.ShapeDtypeStruct((n, D_MODEL), F32)]
    out_specs = [pl.BlockSpec((rows, D_MODEL), lambda b, h: (b, 0))]
    if has_state_out:
        out_shape += [jax.ShapeDtypeStruct((n // seq, 1, N_HEADS, HEAD_K, HEAD_V), F32)] * 2
        out_specs += [pl.BlockSpec(st_block, st_map)] * 2
    return pl.pallas_call(
        functools.partial(_mixer_kernel, seq=seq, heads=heads, seg=seg, has_state_in=has_state_in,
                          has_state_out=has_state_out, has_order_token=run_after is not None),
        out_shape=out_shape,
        grid=(n // rows, hsteps),
        in_specs=in_specs,
        out_specs=out_specs,
        scratch_shapes=[pltpu.VMEM((seqs_per_step * heads, seq, seq), BF16)],
        compiler_params=_params("arbitrary", "arbitrary"),
        name="mixer",
    )(*args)


N_FFN_TILES = D_FF // FFN_TILE


def _ffn_kernel(x1_ref, mod_ref, n2_ref, wu_ref, wg_ref, cw_ref, cb_ref, wd_ref, nf_ref, o_ref, acc_ref,
                *, seq, width):
    cols = lambda f: slice(f * FFN_TILE, (f + 1) * FFN_TILE)
    _ffn_tile(x1_ref, mod_ref, n2_ref, cw_ref, cb_ref, nf_ref, o_ref, acc_ref, seq, width,
              lambda f: wu_ref[:, cols(f)], lambda f: wg_ref[:, cols(f)], lambda f: wd_ref[cols(f), :])


def _ffn_staging_kernel(x1_ref, mod_ref, n2_ref, wu_ref, wg_ref, cw_ref, cb_ref, wd_ref, nf_ref,
                        o_ref, wu_out, wg_out, wd_out, wu_scr, wg_scr, wd_scr, acc_ref, *, seq, width):
    step = pl.program_id(0)

    @pl.when(step < N_FFN_TILES)
    def _():
        for src, scr, out in ((wu_ref, wu_scr, wu_out), (wg_ref, wg_scr, wg_out), (wd_ref, wd_scr, wd_out)):
            w = src[...].astype(BF16)
            scr[step] = w
            out[...] = w

    @pl.when(step >= N_FFN_TILES)
    def _():
        _ffn_tile(x1_ref, mod_ref, n2_ref, cw_ref, cb_ref, nf_ref, o_ref, acc_ref, seq, width,
                  lambda f: wu_scr[f], lambda f: wg_scr[f], lambda f: wd_scr[f])


def _ffn_tile(x1_ref, mod_ref, n2_ref, cw_ref, cb_ref, nf_ref, o_ref, acc_ref, seq, width, wu, wg, wd):
    tm = x1_ref.shape[0]
    mod = mod_ref[...]
    sh2 = mod[:, 3 * D_MODEL:4 * D_MODEL]
    sc2 = mod[:, 4 * D_MODEL:5 * D_MODEL]
    ga2 = mod[:, 5 * D_MODEL:6 * D_MODEL]
    x1 = x1_ref[...]
    xb = (_rms(x1, n2_ref[...]) * (1.0 + sc2) + sh2).astype(BF16)

    rows = lax.broadcasted_iota(jnp.int32, (tm, FFN_TILE), 0)
    col_in_row = rows & (width - 1)
    has_left = col_in_row != 0
    has_right = col_in_row != width - 1
    n_rows = seq // width
    zrow = jnp.zeros((width, FFN_TILE), F32)

    def up_rows(z):
        return jnp.concatenate([zrow, z[:tm - width]], axis=0)

    def down_rows(z):
        return jnp.concatenate([z[width:], zrow], axis=0)

    def up_and_gate(f):
        return _dot(xb, wu(f)), _dot(xb, wg(f))

    ahead = up_and_gate(0)
    for f in range(N_FFN_TILES):
        fs = slice(f * FFN_TILE, (f + 1) * FFN_TILE)
        up, gate = ahead
        if f + 1 < N_FFN_TILES:
            ahead = up_and_gate(f + 1)
        left = jnp.where(has_left, pltpu.roll(up, 1, axis=0), 0.0)
        right = jnp.where(has_right, pltpu.roll(up, tm - 1, axis=0), 0.0)
        w = lambda i, j: cw_ref[3 * i + j:3 * i + j + 1, fs]
        cv = left * w(1, 0) + up * w(1, 1) + right * w(1, 2) + cb_ref[:, fs]
        if n_rows > 1:
            cv = cv + (up_rows(left) * w(0, 0) + up_rows(up) * w(0, 1) + up_rows(right) * w(0, 2)
                       + down_rows(left) * w(2, 0) + down_rows(up) * w(2, 1) + down_rows(right) * w(2, 2))
        act = (cv * _sigmoid(cv) * gate).astype(BF16)
        part = _dot(act, wd(f))
        if f == 0:
            acc_ref[...] = part
        else:
            acc_ref[...] += part

    o_ref[...] = _rms(x1 + ga2 * acc_ref[...], nf_ref[...])


def _ffn_staging(x1, mod3, n2g, wu, wg, cw, cb, wd, nfg, *, tm, seq, width, mod_row):
    assert width & (width - 1) == 0 and tm % seq == 0 and (seq == width or tm == seq)
    tile = lambda i: jnp.maximum(i - N_FFN_TILES, 0)
    w_step = lambda i: jnp.minimum(i, N_FFN_TILES - 1)
    col_blk = pl.BlockSpec((D_MODEL, FFN_TILE), lambda i: (0, w_step(i)))
    row_blk = pl.BlockSpec((FFN_TILE, D_MODEL), lambda i: (w_step(i), 0))
    return pl.pallas_call(
        functools.partial(_ffn_staging_kernel, seq=seq, width=width),
        out_shape=[jax.ShapeDtypeStruct(x1.shape, F32), jax.ShapeDtypeStruct(wu.shape, BF16),
                   jax.ShapeDtypeStruct(wg.shape, BF16), jax.ShapeDtypeStruct(wd.shape, BF16)],
        grid=(N_FFN_TILES + x1.shape[0] // tm,),
        in_specs=[pl.BlockSpec((tm, D_MODEL), lambda i: (tile(i), 0)),
                  pl.BlockSpec((None, 1, N_MOD * D_MODEL), lambda i: (mod_row(tile(i)), 0, 0)),
                  _const_spec(n2g.shape), col_blk, col_blk,
                  _const_spec(cw.shape), _const_spec(cb.shape), row_blk,
                  _const_spec(nfg.shape)],
        out_specs=[pl.BlockSpec((tm, D_MODEL), lambda i: (tile(i), 0)), col_blk, col_blk, row_blk],
        scratch_shapes=[pltpu.VMEM((N_FFN_TILES, D_MODEL, FFN_TILE), BF16),
                        pltpu.VMEM((N_FFN_TILES, D_MODEL, FFN_TILE), BF16),
                        pltpu.VMEM((N_FFN_TILES, FFN_TILE, D_MODEL), BF16),
                        pltpu.VMEM((tm, D_MODEL), F32)],
        compiler_params=_params("arbitrary"),
        name="ffn_staging",
    )(x1, mod3, n2g, wu, wg, cw, cb, wd, nfg)


def _ffn(x1, mod3, n2g, wu, wg, cw, cb, wd, nfg, *, tm, seq, width, mod_row):
    assert width & (width - 1) == 0 and tm % seq == 0 and (seq == width or tm == seq)
    return pl.pallas_call(
        functools.partial(_ffn_kernel, seq=seq, width=width),
        out_shape=jax.ShapeDtypeStruct(x1.shape, F32),
        grid=(x1.shape[0] // tm,),
        in_specs=[pl.BlockSpec((tm, D_MODEL), lambda i: (i, 0)),
                  pl.BlockSpec((None, 1, N_MOD * D_MODEL), lambda i: (mod_row(i), 0, 0)),
                  _const_spec(n2g.shape), _const_spec(wu.shape), _const_spec(wg.shape),
                  _const_spec(cw.shape), _const_spec(cb.shape), _const_spec(wd.shape),
                  _const_spec(nfg.shape)],
        out_specs=pl.BlockSpec((tm, D_MODEL), lambda i: (i, 0)),
        scratch_shapes=[pltpu.VMEM((tm, D_MODEL), F32)],
        compiler_params=_params("arbitrary"),
        name="ffn",
    )(x1, mod3, n2g, wu, wg, cw, cb, wd, nfg)


def kernel(x_prompt, x_sample, c, state_gla_fwd, state_gla_bwd, c_ctx, w_ada, b_ada, norm1_g, w_in, w_gk_f,
           b_gk_f, w_gk_b, b_gk_b, gla_norm_g, conv_mix_w, w_out, norm2_g, ffn_w_up, ffn_w_gate,
           ffn_conv_w, ffn_conv_b, ffn_w_down, normf_g):
    bp, lp, d = x_prompt.shape
    bs, ls, _ = x_sample.shape
    assert w_ada.shape[0] == 1 and d == D_MODEL and bs + 1 <= MOD_ROWS

    cc = jnp.zeros((MOD_ROWS, d), F32).at[0].set(c_ctx).at[1:1 + bs].set(c)
    mod3 = _ada(cc, w_ada[0], b_ada[0][None, :]).reshape(MOD_ROWS, 1, N_MOD * d)

    assert IN_SPLITS == (KEY_DIM, KEY_DIM, VAL_DIM, VAL_DIM, GATE_RANK, GATE_RANK) + (CONV_DIM,) * 5
    wgk = (jnp.zeros((MXU_LANES_V7X, 2 * KEY_DIM), F32)
           .at[:GATE_RANK, :KEY_DIM].set(w_gk_f[0]).at[GATE_RANK:2 * GATE_RANK, KEY_DIM:].set(w_gk_b[0])
           .astype(BF16))
    bgk = jnp.concatenate([b_gk_f[0], b_gk_b[0]])[None, :]
    wo = w_out[0].astype(BF16)
    cw9 = ffn_conv_w[0].reshape(9, D_FF)
    cb = ffn_conv_b[0][None, :]
    n1g, n2g, nfg, gn = norm1_g[0][None, :], norm2_g[0][None, :], normf_g[None, :], gla_norm_g[0][None, :]

    xp, xs = x_prompt.reshape(-1, d), x_sample.reshape(-1, d)
    tiles_p = xp.shape[0] // INPROJ_ROWS
    seg = min(lp, ls, INPROJ_ROWS)
    proj = _inproj(xp, xs, mod3, n1g, w_in[0].T, wgk, bgk, tm=INPROJ_ROWS, seg=seg,
                   mod_row=lambda t: jnp.where(t < tiles_p, 0, 1 + ((t - tiles_p) * INPROJ_ROWS) // ls))

    mix = functools.partial(_mixer, mod3=mod3, proj=proj, conv_w=conv_mix_w[0], gn=gn, wo=wo, seg=seg)
    x1p, new_f, new_b = mix(xp, states=None, seq=lp, heads=N_HEADS, seqs_per_step=PROMPT_SEQS_PER_STEP,
                            first_row=0, mod_row=lambda b: 0, has_state_out=True)
    y_prompt, wu, wg, wd = _ffn_staging(x1p, mod3, n2g, ffn_w_up[0], ffn_w_gate[0], cw9, cb, ffn_w_down[0], nfg,
                                        tm=FFN_ROWS, seq=lp, width=lp, mod_row=lambda i: 0)
    x1s, = mix(xs, states=(state_gla_fwd, state_gla_bwd), seq=ls, heads=1, seqs_per_step=1,
               first_row=xp.shape[0], mod_row=lambda b: 1 + b, has_state_out=False, run_after=wd)
    y_sample = _ffn(x1s, mod3, n2g, wu, wg, cw9, cb, wd, nfg, tm=FFN_ROWS, seq=ls, width=GRID_W,
                    mod_row=lambda i: 1 + (i * FFN_ROWS) // ls)
    return y_prompt.reshape(x_prompt.shape), y_sample.reshape(x_sample.shape), new_f, new_b
```

```python
import functools

import jax
import jax.numpy as jnp
from jax import lax
from jax.experimental import pallas as pl
from jax.experimental.pallas import tpu as pltpu

D_MODEL = 1024
N_HEADS = 4
HEAD_K = 128
HEAD_V = 256
KEY_DIM = N_HEADS * HEAD_K
VAL_DIM = N_HEADS * HEAD_V
GATE_RANK = 16
GATE_NORMALIZER = 16.0
LOG2_E = 1.4426950408889634
CONV_DIM = D_MODEL
D_FF = 2816
N_MOD = 6
EPS = 1e-6
GRID_W = 64
IN_SPLITS = (KEY_DIM, KEY_DIM, VAL_DIM, VAL_DIM, GATE_RANK, GATE_RANK,
             CONV_DIM, CONV_DIM, CONV_DIM, VAL_DIM, CONV_DIM)

DIAG_BLOCK = 64
SUPER_BLOCK = 256
MXU_LANES_V7X = 128
FFN_TILE = 256
FFN_ROWS = 1024
INPROJ_ROWS = 512
PROMPT_SEQS_PER_STEP = 2
ADA_TILE = 1024
MOD_ROWS = 8
VMEM_LIMIT_BYTES = 56 * 1024 * 1024
A_SCRATCH_BYTES = 4 * 1024 * 1024
MIXER_VMEM_LIMIT_BYTES = 61 * 1024 * 1024

F32 = jnp.float32
BF16 = jnp.bfloat16


def _dot(a, b):
    return jnp.dot(a, b, preferred_element_type=F32)


def _dot_nt(a, b):
    return lax.dot_general(a, b, (((1,), (1,)), ((), ())), preferred_element_type=F32)


def _dot_tn(a, b):
    return lax.dot_general(a, b, (((0,), (0,)), ((), ())), preferred_element_type=F32)


def _sigmoid(x):
    return 1.0 / (1.0 + jnp.exp(-x))


def _rms(x, g):
    return x * lax.rsqrt(jnp.mean(x * x, axis=-1, keepdims=True) + EPS) * g


def _const_spec(shape):
    nd = len(shape)
    return pl.BlockSpec(shape, lambda *_: (0,) * nd, pipeline_mode=pl.Buffered(1))


def _params(*sem, vmem=VMEM_LIMIT_BYTES):
    return pltpu.CompilerParams(dimension_semantics=sem, vmem_limit_bytes=vmem)


def _ada_kernel(c_ref, w_ref, b_ref, o_ref):
    c = c_ref[...]
    s = (c * _sigmoid(c)).astype(BF16)
    o_ref[...] = _dot(s, w_ref[...].astype(BF16)) + b_ref[...]


def _ada(cc, w_ada, b_ada):
    n = w_ada.shape[1]
    return pl.pallas_call(
        _ada_kernel,
        out_shape=jax.ShapeDtypeStruct((MOD_ROWS, n), F32),
        grid=(n // ADA_TILE,),
        in_specs=[pl.BlockSpec((MOD_ROWS, D_MODEL), lambda j: (0, 0)),
                  pl.BlockSpec((D_MODEL, ADA_TILE), lambda j: (0, j)),
                  pl.BlockSpec((1, ADA_TILE), lambda j: (0, j))],
        out_specs=pl.BlockSpec((MOD_ROWS, ADA_TILE), lambda j: (0, j)),
        compiler_params=_params("arbitrary"),
        name="ada",
    )(cc, w_ada, b_ada)


def _prefix_sum(x, pos, seg):
    s = 1
    while s < seg:
        x = x + jnp.where(pos >= s, pltpu.roll(x, s, axis=0), 0.0)
        s *= 2
    return x


def _suffix_sum(x, pos, seg):
    n, s = x.shape[0], 1
    while s < seg:
        x = x + jnp.where(pos < seg - s, pltpu.roll(x, n - s, axis=0), 0.0)
        s *= 2
    return x


W_BLOCK = 512
HEAD_COLS = 2 * KEY_DIM + 2 * VAL_DIM
TAIL_OFF = HEAD_COLS + 2 * GATE_RANK
TAIL_SHIFT = TAIL_OFF % MXU_LANES_V7X
N_HEAD_BLOCKS = HEAD_COLS // W_BLOCK
N_W_BLOCKS = N_HEAD_BLOCKS + 5 * CONV_DIM // W_BLOCK


def _inproj_kernel(xa_ref, xb_ref, mod_ref, n1_ref, wa_ref, wb_ref, wgk_ref, bgk_ref,
                   q_ref, k_ref, v_ref, og_ref, p_ref, cbg_ref, g_ref, w_scr, wcode_scr, *, tiles_a, seg):
    step = pl.program_id(0)

    @pl.when(step < N_HEAD_BLOCKS)
    def _():
        w_scr[step] = wa_ref[...].T.astype(BF16)
        lane = lax.broadcasted_iota(jnp.int32, (D_MODEL, MXU_LANES_V7X), 1)
        wcode_scr[...] = jnp.where(lane < 2 * GATE_RANK, wb_ref[...].T, 0.0).astype(BF16)

    @pl.when(jnp.logical_and(step >= N_HEAD_BLOCKS, step < N_W_BLOCKS))
    def _():
        rows = jnp.concatenate([wa_ref[TAIL_SHIFT:, :], wb_ref[:TAIL_SHIFT, :]], axis=0)
        w_scr[step] = rows.T.astype(BF16)

    @pl.when(step >= N_W_BLOCKS)
    def _():
        first_group = step - N_W_BLOCKS < tiles_a
        mod = mod_ref[...]
        sh1 = mod[:, 0:D_MODEL]
        sc1 = mod[:, D_MODEL:2 * D_MODEL]
        x = jnp.where(first_group, xa_ref[...], xb_ref[...])
        xn = (_rms(x, n1_ref[...]) * (1.0 + sc1) + sh1).astype(BF16)
        halves = range(CONV_DIM // W_BLOCK)
        head = lambda blk: _dot(xn, w_scr[blk])
        tail = lambda grp, j: _dot(xn, w_scr[N_HEAD_BLOCKS + grp * len(halves) + j])
        cols = lambda j: slice(j * W_BLOCK, (j + 1) * W_BLOCK)

        z = _dot(_dot(xn, wcode_scr[...]).astype(BF16), wgk_ref[...]) + bgk_ref[...]
        g = (jnp.minimum(z, 0.0) - jnp.log(1.0 + jnp.exp(-jnp.abs(z)))) * (LOG2_E / GATE_NORMALIZER)
        g_hi = g.astype(BF16)
        g_lo = (g - g_hi.astype(F32)).astype(BF16)

        q_ref[...] = (head(0) * (HEAD_K ** -0.5)).astype(BF16)
        k_ref[...] = head(1).astype(BF16)
        for j in halves:
            v_ref[:, cols(j)] = head(2 + j).astype(BF16)

        r = lax.broadcasted_iota(jnp.int32, (seg, seg), 0)
        c = lax.broadcasted_iota(jnp.int32, (seg, seg), 1)
        lower = jnp.where(c <= r, 1.0, 0.0).astype(BF16)
        upper = jnp.where(c >= r, 1.0, 0.0).astype(BF16)
        for s0 in range(0, g.shape[0], seg):
            rows = slice(s0, s0 + seg)
            for tri, cs in ((lower, slice(0, KEY_DIM)), (upper, slice(KEY_DIM, 2 * KEY_DIM))):
                terms = jnp.concatenate([g_hi[rows, cs], g_lo[rows, cs]], axis=0)
                g_ref[rows, cs] = _dot(jnp.concatenate([tri, tri], axis=1), terms)

        for j in halves:
            g_out = head(2 + len(halves) + j)
            og_ref[:, cols(j)] = (g_out * _sigmoid(g_out) * _sigmoid(tail(3, j))).astype(BF16)
            p_ref[:, cols(j)] = (tail(1, j) * tail(2, j)).astype(BF16)
            cbg_ref[:, cols(j)] = (_sigmoid(tail(4, j)) * tail(0, j)).astype(BF16)


def _inproj(xa, xb, mod3, n1g, w_in_t, wgk, bgk, *, tm, seg, mod_row):
    tiles_a, tiles_b = xa.shape[0] // tm, xb.shape[0] // tm
    n = xa.shape[0] + xb.shape[0]
    assert seg & (seg - 1) == 0 and tm % seg == 0
    assert KEY_DIM == W_BLOCK and HEAD_COLS % W_BLOCK == 0 and W_BLOCK % MXU_LANES_V7X == 0
    tile = lambda i: jnp.maximum(i - N_W_BLOCKS, 0)
    row = lambda i: (tile(i), 0)
    lanes_per_block = W_BLOCK // MXU_LANES_V7X
    wide = lambda w, dt: (jax.ShapeDtypeStruct((n, w), dt), pl.BlockSpec((tm, w), row))
    outs = [wide(KEY_DIM, BF16), wide(KEY_DIM, BF16), wide(VAL_DIM, BF16), wide(VAL_DIM, BF16),
            wide(CONV_DIM, BF16), wide(CONV_DIM, BF16), wide(2 * KEY_DIM, F32)]
    return pl.pallas_call(
        functools.partial(_inproj_kernel, tiles_a=tiles_a, seg=seg),
        out_shape=[o[0] for o in outs],
        grid=(N_W_BLOCKS + tiles_a + tiles_b,),
        in_specs=[pl.BlockSpec((tm, D_MODEL), lambda i: (jnp.minimum(tile(i), tiles_a - 1), 0)),
                  pl.BlockSpec((tm, D_MODEL), lambda i: (jnp.maximum(tile(i) - tiles_a, 0), 0)),
                  pl.BlockSpec((None, 1, N_MOD * D_MODEL), lambda i: (mod_row(tile(i)), 0, 0)),
                  _const_spec(n1g.shape),
                  pl.BlockSpec((W_BLOCK, D_MODEL), lambda i: (jnp.minimum(i, N_W_BLOCKS - 1), 0)),
                  pl.BlockSpec((MXU_LANES_V7X, D_MODEL),
                               lambda i: (jnp.where(i < N_HEAD_BLOCKS, HEAD_COLS // MXU_LANES_V7X,
                                                    (jnp.minimum(i, N_W_BLOCKS - 1) + 1) * lanes_per_block), 0)),
                  _const_spec(wgk.shape), _const_spec(bgk.shape)],
        out_specs=[o[1] for o in outs],
        scratch_shapes=[pltpu.VMEM((N_W_BLOCKS, D_MODEL, W_BLOCK), BF16),
                        pltpu.VMEM((D_MODEL, MXU_LANES_V7X), BF16)],
        compiler_params=_params("arbitrary"),
        name="in_proj",
    )(xa, xb, mod3, n1g, w_in_t, w_in_t, wgk, bgk)


def _cat(parts, axis=0):
    return parts[0] if len(parts) == 1 else jnp.concatenate(parts, axis=axis)


def _join_prefix(b, seg):
    parts = [b[:seg]]
    for c in range(1, b.shape[0] // seg):
        parts.append(b[c * seg:(c + 1) * seg] + parts[-1][seg - 1:seg, :])
    return _cat(parts)


def _join_suffix(b, seg):
    n = b.shape[0] // seg
    parts = [b[(n - 1) * seg:]]
    for c in range(n - 2, -1, -1):
        parts.insert(0, b[c * seg:(c + 1) * seg] + parts[0][0:1, :])
    return _cat(parts)


def _diag_args(b, off):
    return _cat([b[c:c + DIAG_BLOCK] - b[c + off:c + off + 1, :] for c in range(0, b.shape[0], DIAG_BLOCK)])


def _pair_args(bf, sb, s):
    lhs, rhs = [], []
    for e0 in range(0, bf.shape[0], 2 * s):
        o0 = e0 + s
        rf = bf[o0 - 1:o0, :]
        rb = sb[o0:o0 + 1, :]
        lhs += [sb[e0:o0] - rb, bf[o0:o0 + s] - rf]
        rhs += [rf - bf[e0:o0], rb - sb[o0:o0 + s]]
    return _cat(lhs), _cat(rhs)


def _mixer_kernel(*refs, seq, heads, seg, has_state_in, has_state_out, has_order_token):
    (q_ref, k_ref, v_ref, gf_ref, gb_ref, og_ref, p_ref, cbg_ref, cw_ref, gn_ref, wo_ref, x_ref,
     mod_ref) = refs[:13]
    refs = refs[13:]
    if has_state_in:
        s0f_ref, s0b_ref = refs[:2]
        refs = refs[2:]
    if has_order_token:
        refs = refs[1:]
    x1_ref = refs[0]
    refs = refs[1:]
    if has_state_out:
        sf_ref, sb_ref = refs[:2]
        refs = refs[2:]
    a_scr = refs[0]
    hsteps = N_HEADS // heads
    half = SUPER_BLOCK // 2
    n_seq = x_ref.shape[0] // seq

    rows_v = lax.broadcasted_iota(jnp.int32, (seq, HEAD_V), 0)
    r_loc = lax.broadcasted_iota(jnp.int32, (DIAG_BLOCK, half), 0)
    lane = lax.broadcasted_iota(jnp.int32, (DIAG_BLOCK, half), 1)
    ones = jnp.ones((HEAD_V, half), BF16)
    scale = lambda t, e: (t.astype(F32) * e).astype(BF16)

    mixes = [[] for _ in range(n_seq)]
    for si, h in [(si, h) for h in range(heads) for si in range(n_seq)]:
        rq = slice(si * seq, (si + 1) * seq)
        a_ref = a_scr.at[(si * heads + h) % a_scr.shape[0]]
        ks = slice(h * HEAD_K, (h + 1) * HEAD_K)
        vs = slice(h * HEAD_V, (h + 1) * HEAD_V)
        q = q_ref[rq, ks]
        k = k_ref[rq, ks]
        v = v_ref[rq, vs]
        bf = _join_prefix(gf_ref[rq, ks], seg)
        sb = _join_suffix(gb_ref[rq, ks], seg)

        e0f = jnp.exp2(_diag_args(bf, DIAG_BLOCK // 2))
        e0b = jnp.exp2(_diag_args(sb, DIAG_BLOCK // 2 - 1))
        q0f, k0f = scale(q, e0f), scale(k, 1.0 / e0f)
        q0b, k0b = scale(q, e0b), scale(k, 1.0 / e0b)

        lhs_levels, rhs_levels = [], []
        s = DIAG_BLOCK
        while s < seq:
            la, ra = _pair_args(bf, sb, s)
            lhs_levels.append(scale(q, jnp.exp2(la)))
            rhs_levels.append(scale(k, jnp.exp2(ra)))
            s *= 2

        for base in range(0, seq, SUPER_BLOCK):
            rs = slice(base, base + SUPER_BLOCK)
            m0f = _dot_nt(q0f[rs], k0f[rs])
            m0b = _dot_nt(q0b[rs], k0b[rs])
            m1 = _dot_nt(lhs_levels[0][rs], rhs_levels[0][rs])
            m2 = _dot_nt(lhs_levels[1][rs], rhs_levels[1][rs])
            for bi in range(SUPER_BLOCK // DIAG_BLOCK):
                rr = slice(bi * DIAG_BLOCK, (bi + 1) * DIAG_BLOCK)
                for lt in range(2):
                    cc = slice(lt * half, (lt + 1) * half)
                    if lt == bi // 2:
                        c_loc = lane - DIAG_BLOCK * (bi % 2)
                        in_diag = (lane >= DIAG_BLOCK) if bi % 2 else (lane < DIAG_BLOCK)
                        diag = (jnp.where(c_loc <= r_loc, m0f[rr, cc], 0.0)
                                + jnp.where(c_loc >= r_loc, m0b[rr, cc], 0.0))
                        piece = jnp.where(in_diag, diag, m1[rr, cc])
                    else:
                        piece = m2[rr, cc]
                    a_ref[base + bi * DIAG_BLOCK:base + (bi + 1) * DIAG_BLOCK,
                          base + lt * half:base + (lt + 1) * half] = piece.astype(BF16)
        s = SUPER_BLOCK
        lev = 2
        while s < seq:
            lhs, rhs = lhs_levels[lev], rhs_levels[lev]
            for e0 in range(0, seq, 2 * s):
                ev = slice(e0, e0 + s)
                od = slice(e0 + s, e0 + 2 * s)
                a_ref[od, ev] = _dot_nt(lhs[od], rhs[ev]).astype(BF16)
                a_ref[ev, od] = _dot_nt(lhs[ev], rhs[od]).astype(BF16)
            s *= 2
            lev += 1

        o = _dot(a_ref[...], v)
        if has_state_in:
            qi = jnp.concatenate([scale(q, jnp.exp2(bf)), scale(q, jnp.exp2(sb))], axis=1)
            s0 = jnp.concatenate([s0f_ref[si, h].astype(BF16), s0b_ref[si, h].astype(BF16)], axis=0)
            o = o + _dot(qi, s0)
        if has_state_out:
            sf_ref[si, h] = _dot_tn(scale(k, jnp.exp2(bf[seq - 1:seq, :] - bf)), v)
            sb_ref[si, h] = _dot_tn(scale(k, jnp.exp2(sb[0:1, :] - sb)), v)

        inv = lax.rsqrt(_dot((o * o).astype(BF16), ones) * (1.0 / HEAD_V) + EPS)
        o = jnp.concatenate([o[:, :half] * inv, o[:, half:] * inv], axis=1) * gn_ref[...]
        p = p_ref[rq, vs].astype(F32)
        conv = (jnp.where(rows_v >= 1, pltpu.roll(p, 1, axis=0), 0.0) * cw_ref[0:1, vs] + p * cw_ref[1:2, vs]
                + jnp.where(rows_v < seq - 1, pltpu.roll(p, seq - 1, axis=0), 0.0) * cw_ref[2:3, vs])
        mixes[si].append(og_ref[rq, vs] * o.astype(BF16) + cbg_ref[rq, vs] * conv.astype(BF16))

    contrib = _dot(_cat([_cat(m, axis=1) for m in mixes], axis=0), wo_ref[...])
    ga1 = mod_ref[:, 2 * D_MODEL:3 * D_MODEL]
    if hsteps == 1:
        x1_ref[...] = x_ref[...] + ga1 * contrib
    else:
        hstep = pl.program_id(1)

        @pl.when(hstep == 0)
        def _():
            x1_ref[...] = x_ref[...] + ga1 * contrib

        @pl.when(hstep != 0)
        def _():
            x1_ref[...] += ga1 * contrib


def _mixer(x, mod3, proj, conv_w, gn, wo, states, *, seq, heads, seqs_per_step, seg, first_row, mod_row,
           has_state_out, run_after=None):
    q, k, v, og, p, cbg, g = proj
    n = x.shape[0]
    rows = seqs_per_step * seq
    hsteps = N_HEADS // heads
    has_state_in = states is not None
    assert seq % SUPER_BLOCK == 0 and seq % seg == 0 and first_row % rows == 0 and n % rows == 0
    b0 = first_row // rows
    a_buffers = max(1, min(seqs_per_step * heads, A_SCRATCH_BYTES // (2 * seq * seq)))
    col = lambda b, h: (b0 + b, h)
    st_block = (seqs_per_step, None, heads, HEAD_K, HEAD_V)
    st_map = lambda b, h: (b, 0, h, 0, 0)
    in_specs = [pl.BlockSpec((rows, heads * HEAD_K), col),
                pl.BlockSpec((rows, heads * HEAD_K), col),
                pl.BlockSpec((rows, heads * HEAD_V), col),
                pl.BlockSpec((rows, heads * HEAD_K), col),
                pl.BlockSpec((rows, heads * HEAD_K), lambda b, h: (b0 + b, hsteps + h)),
                pl.BlockSpec((rows, heads * HEAD_V), col),
                pl.BlockSpec((rows, heads * HEAD_V), col),
                pl.BlockSpec((rows, heads * HEAD_V), col),
                pl.BlockSpec((3, heads * HEAD_V), lambda b, h: (0, h)),
                pl.BlockSpec((1, HEAD_V), lambda b, h: (0, 0)),
                (pl.BlockSpec((heads * HEAD_V, D_MODEL), lambda b, h: (h, 0)) if hsteps > 1
                 else _const_spec((heads * HEAD_V, D_MODEL))),
                pl.BlockSpec((rows, D_MODEL), lambda b, h: (b, 0)),
                pl.BlockSpec((None, 1, N_MOD * D_MODEL), lambda b, h: (mod_row(b), 0, 0))]
    args = [q, k, v, g, g, og, p, cbg, conv_w, gn, wo, x, mod3]
    if has_state_in:
        in_specs += [pl.BlockSpec(st_block, st_map)] * 2
        args += list(states)
    if run_after is not None:
        in_specs.append(pl.BlockSpec(memory_space=pl.ANY))
        args.append(run_after)
    out_shape = [jax.ShapeDtypeStruct((n, D_MODEL), F32)]
    out_specs = [pl.BlockSpec((rows, D_MODEL), lambda b, h: (b, 0))]
    if has_state_out:
        out_shape += [jax.ShapeDtypeStruct((n // seq, 1, N_HEADS, HEAD_K, HEAD_V), F32)] * 2
        out_specs += [pl.BlockSpec(st_block, st_map)] * 2
    return pl.pallas_call(
        functools.partial(_mixer_kernel, seq=seq, heads=heads, seg=seg, has_state_in=has_state_in,
                          has_state_out=has_state_out, has_order_token=run_after is not None),
        out_shape=out_shape,
        grid=(n // rows, hsteps),
        in_specs=in_specs,
        out_specs=out_specs,
        scratch_shapes=[pltpu.VMEM((a_buffers, seq, seq), BF16)],
        compiler_params=_params("arbitrary", "arbitrary", vmem=MIXER_VMEM_LIMIT_BYTES),
        name="mixer",
    )(*args)


N_FFN_TILES = D_FF // FFN_TILE


def _ffn_kernel(x1_ref, mod_ref, n2_ref, wu_ref, wg_ref, cw_ref, cb_ref, wd_ref, nf_ref, o_ref, acc_ref,
                *, seq, width):
    cols = lambda f: slice(f * FFN_TILE, (f + 1) * FFN_TILE)
    _ffn_tile(x1_ref, mod_ref, n2_ref, cw_ref, cb_ref, nf_ref, o_ref, acc_ref, seq, width,
              lambda f: wu_ref[:, cols(f)], lambda f: wg_ref[:, cols(f)], lambda f: wd_ref[cols(f), :])


def _ffn_staging_kernel(x1_ref, mod_ref, n2_ref, wu_ref, wg_ref, cw_ref, cb_ref, wd_ref, nf_ref,
                        o_ref, wu_out, wg_out, wd_out, wu_scr, wg_scr, wd_scr, acc_ref, *, seq, width):
    step = pl.program_id(0)

    @pl.when(step < N_FFN_TILES)
    def _():
        for src, scr, out in ((wu_ref, wu_scr, wu_out), (wg_ref, wg_scr, wg_out), (wd_ref, wd_scr, wd_out)):
            w = src[...].astype(BF16)
            scr[step] = w
            out[...] = w

    @pl.when(step >= N_FFN_TILES)
    def _():
        _ffn_tile(x1_ref, mod_ref, n2_ref, cw_ref, cb_ref, nf_ref, o_ref, acc_ref, seq, width,
                  lambda f: wu_scr[f], lambda f: wg_scr[f], lambda f: wd_scr[f])


def _ffn_tile(x1_ref, mod_ref, n2_ref, cw_ref, cb_ref, nf_ref, o_ref, acc_ref, seq, width, wu, wg, wd):
    tm = x1_ref.shape[0]
    mod = mod_ref[...]
    sh2 = mod[:, 3 * D_MODEL:4 * D_MODEL]
    sc2 = mod[:, 4 * D_MODEL:5 * D_MODEL]
    ga2 = mod[:, 5 * D_MODEL:6 * D_MODEL]
    x1 = x1_ref[...]
    xb = (_rms(x1, n2_ref[...]) * (1.0 + sc2) + sh2).astype(BF16)

    rows = lax.broadcasted_iota(jnp.int32, (tm, FFN_TILE), 0)
    col_in_row = rows & (width - 1)
    has_left = col_in_row != 0
    has_right = col_in_row != width - 1
    n_rows = seq // width
    zrow = jnp.zeros((width, FFN_TILE), F32)

    def up_rows(z):
        return jnp.concatenate([zrow, z[:tm - width]], axis=0)

    def down_rows(z):
        return jnp.concatenate([z[width:], zrow], axis=0)

    def up_and_gate(f):
        return _dot(xb, wu(f)), _dot(xb, wg(f))

    ahead = up_and_gate(0)
    for f in range(N_FFN_TILES):
        fs = slice(f * FFN_TILE, (f + 1) * FFN_TILE)
        up, gate = ahead
        if f + 1 < N_FFN_TILES:
            ahead = up_and_gate(f + 1)
        left = jnp.where(has_left, pltpu.roll(up, 1, axis=0), 0.0)
        right = jnp.where(has_right, pltpu.roll(up, tm - 1, axis=0), 0.0)
        w = lambda i, j: cw_ref[3 * i + j:3 * i + j + 1, fs]
        cv = left * w(1, 0) + up * w(1, 1) + right * w(1, 2) + cb_ref[:, fs]
        if n_rows > 1:
            cv = cv + (up_rows(left) * w(0, 0) + up_rows(up) * w(0, 1) + up_rows(right) * w(0, 2)
                       + down_rows(left) * w(2, 0) + down_rows(up) * w(2, 1) + down_rows(right) * w(2, 2))
        act = (cv * _sigmoid(cv) * gate).astype(BF16)
        part = _dot(act, wd(f))
        if f == 0:
            acc_ref[...] = part
        else:
            acc_ref[...] += part

    o_ref[...] = _rms(x1 + ga2 * acc_ref[...], nf_ref[...])


def _ffn_staging(x1, mod3, n2g, wu, wg, cw, cb, wd, nfg, *, tm, seq, width, mod_row):
    assert width & (width - 1) == 0 and tm % seq == 0 and (seq == width or tm == seq)
    tile = lambda i: jnp.maximum(i - N_FFN_TILES, 0)
    w_step = lambda i: jnp.minimum(i, N_FFN_TILES - 1)
    col_blk = pl.BlockSpec((D_MODEL, FFN_TILE), lambda i: (0, w_step(i)))
    row_blk = pl.BlockSpec((FFN_TILE, D_MODEL), lambda i: (w_step(i), 0))
    return pl.pallas_call(
        functools.partial(_ffn_staging_kernel, seq=seq, width=width),
        out_shape=[jax.ShapeDtypeStruct(x1.shape, F32), jax.ShapeDtypeStruct(wu.shape, BF16),
                   jax.ShapeDtypeStruct(wg.shape, BF16), jax.ShapeDtypeStruct(wd.shape, BF16)],
        grid=(N_FFN_TILES + x1.shape[0] // tm,),
        in_specs=[pl.BlockSpec((tm, D_MODEL), lambda i: (tile(i), 0)),
                  pl.BlockSpec((None, 1, N_MOD * D_MODEL), lambda i: (mod_row(tile(i)), 0, 0)),
                  _const_spec(n2g.shape), col_blk, col_blk,
                  _const_spec(cw.shape), _const_spec(cb.shape), row_blk,
                  _const_spec(nfg.shape)],
        out_specs=[pl.BlockSpec((tm, D_MODEL), lambda i: (tile(i), 0)), col_blk, col_blk, row_blk],
        scratch_shapes=[pltpu.VMEM((N_FFN_TILES, D_MODEL, FFN_TILE), BF16),
                        pltpu.VMEM((N_FFN_TILES, D_MODEL, FFN_TILE), BF16),
                        pltpu.VMEM((N_FFN_TILES, FFN_TILE, D_MODEL), BF16),
                        pltpu.VMEM((tm, D_MODEL), F32)],
        compiler_params=_params("arbitrary"),
        name="ffn_staging",
    )(x1, mod3, n2g, wu, wg, cw, cb, wd, nfg)


def _ffn(x1, mod3, n2g, wu, wg, cw, cb, wd, nfg, *, tm, seq, width, mod_row):
    assert width & (width - 1) == 0 and tm % seq == 0 and (seq == width or tm == seq)
    return pl.pallas_call(
        functools.partial(_ffn_kernel, seq=seq, width=width),
        out_shape=jax.ShapeDtypeStruct(x1.shape, F32),
        grid=(x1.shape[0] // tm,),
        in_specs=[pl.BlockSpec((tm, D_MODEL), lambda i: (i, 0)),
                  pl.BlockSpec((None, 1, N_MOD * D_MODEL), lambda i: (mod_row(i), 0, 0)),
                  _const_spec(n2g.shape), _const_spec(wu.shape), _const_spec(wg.shape),
                  _const_spec(cw.shape), _const_spec(cb.shape), _const_spec(wd.shape),
                  _const_spec(nfg.shape)],
        out_specs=pl.BlockSpec((tm, D_MODEL), lambda i: (i, 0)),
        scratch_shapes=[pltpu.VMEM((tm, D_MODEL), F32)],
        compiler_params=_params("arbitrary"),
        name="ffn",
    )(x1, mod3, n2g, wu, wg, cw, cb, wd, nfg)


def kernel(x_prompt, x_sample, c, state_gla_fwd, state_gla_bwd, c_ctx, w_ada, b_ada, norm1_g, w_in, w_gk_f,
           b_gk_f, w_gk_b, b_gk_b, gla_norm_g, conv_mix_w, w_out, norm2_g, ffn_w_up, ffn_w_gate,
           ffn_conv_w, ffn_conv_b, ffn_w_down, normf_g):
    bp, lp, d = x_prompt.shape
    bs, ls, _ = x_sample.shape
    assert w_ada.shape[0] == 1 and d == D_MODEL and bs + 1 <= MOD_ROWS

    cc = jnp.zeros((MOD_ROWS, d), F32).at[0].set(c_ctx).at[1:1 + bs].set(c)
    mod3 = _ada(cc, w_ada[0], b_ada[0][None, :]).reshape(MOD_ROWS, 1, N_MOD * d)

    assert IN_SPLITS == (KEY_DIM, KEY_DIM, VAL_DIM, VAL_DIM, GATE_RANK, GATE_RANK) + (CONV_DIM,) * 5
    wgk = (jnp.zeros((MXU_LANES_V7X, 2 * KEY_DIM), F32)
           .at[:GATE_RANK, :KEY_DIM].set(w_gk_f[0]).at[GATE_RANK:2 * GATE_RANK, KEY_DIM:].set(w_gk_b[0])
           .astype(BF16))
    bgk = jnp.concatenate([b_gk_f[0], b_gk_b[0]])[None, :]
    wo = w_out[0].astype(BF16)
    cw9 = ffn_conv_w[0].reshape(9, D_FF)
    cb = ffn_conv_b[0][None, :]
    n1g, n2g, nfg, gn = norm1_g[0][None, :], norm2_g[0][None, :], normf_g[None, :], gla_norm_g[0][None, :]

    xp, xs = x_prompt.reshape(-1, d), x_sample.reshape(-1, d)
    tiles_p = xp.shape[0] // INPROJ_ROWS
    seg = min(lp, ls, INPROJ_ROWS)
    proj = _inproj(xp, xs, mod3, n1g, w_in[0].T, wgk, bgk, tm=INPROJ_ROWS, seg=seg,
                   mod_row=lambda t: jnp.where(t < tiles_p, 0, 1 + ((t - tiles_p) * INPROJ_ROWS) // ls))

    mix = functools.partial(_mixer, mod3=mod3, proj=proj, conv_w=conv_mix_w[0], gn=gn, wo=wo, seg=seg)
    x1p, new_f, new_b = mix(xp, states=None, seq=lp, heads=N_HEADS, seqs_per_step=PROMPT_SEQS_PER_STEP,
                            first_row=0, mod_row=lambda b: 0, has_state_out=True)
    y_prompt, wu, wg, wd = _ffn_staging(x1p, mod3, n2g, ffn_w_up[0], ffn_w_gate[0], cw9, cb, ffn_w_down[0], nfg,
                                        tm=FFN_ROWS, seq=lp, width=lp, mod_row=lambda i: 0)
    x1s, = mix(xs, states=(state_gla_fwd, state_gla_bwd), seq=ls, heads=N_HEADS, seqs_per_step=1,
               first_row=xp.shape[0], mod_row=lambda b: 1 + b, has_state_out=False, run_after=wd)
    y_sample = _ffn(x1s, mod3, n2g, wu, wg, cw9, cb, wd, nfg, tm=FFN_ROWS, seq=ls, width=GRID_W,
                    mod_row=lambda i: 1 + (i * FFN_ROWS) // ls)
    return y_prompt.reshape(x_prompt.shape), y_sample.reshape(x_sample.shape), new_f, new_b
```

```python
import functools

import jax
import jax.numpy as jnp
from jax import lax
from jax.experimental import pallas as pl
from jax.experimental.pallas import tpu as pltpu

D_MODEL = 1024
N_HEADS = 4
HEAD_K = 128
HEAD_V = 256
KEY_DIM = N_HEADS * HEAD_K
VAL_DIM = N_HEADS * HEAD_V
GATE_RANK = 16
GATE_NORMALIZER = 16.0
LOG2_E = 1.4426950408889634
CONV_DIM = D_MODEL
D_FF = 2816
N_MOD = 6
EPS = 1e-6
GRID_W = 64
IN_SPLITS = (KEY_DIM, KEY_DIM, VAL_DIM, VAL_DIM, GATE_RANK, GATE_RANK,
             CONV_DIM, CONV_DIM, CONV_DIM, VAL_DIM, CONV_DIM)

DIAG_BLOCK = 64
SUPER_BLOCK = 256
MXU_LANES_V7X = 128
FFN_TILE = 256
FFN_ROWS = 1024
INPROJ_ROWS = 512
PROMPT_SEQS_PER_STEP = 2
ADA_TILE = 1024
MOD_ROWS = 8
VMEM_LIMIT_BYTES = 56 * 1024 * 1024
A_SCRATCH_BYTES = 4 * 1024 * 1024
MIXER_VMEM_LIMIT_BYTES = 61 * 1024 * 1024

F32 = jnp.float32
BF16 = jnp.bfloat16


def _dot(a, b):
    return jnp.dot(a, b, preferred_element_type=F32)


def _dot_nt(a, b):
    return lax.dot_general(a, b, (((1,), (1,)), ((), ())), preferred_element_type=F32)


def _dot_tn(a, b):
    return lax.dot_general(a, b, (((0,), (0,)), ((), ())), preferred_element_type=F32)


def _sigmoid(x):
    return 1.0 / (1.0 + jnp.exp(-x))


def _rms(x, g):
    return x * lax.rsqrt(jnp.mean(x * x, axis=-1, keepdims=True) + EPS) * g


def _const_spec(shape):
    nd = len(shape)
    return pl.BlockSpec(shape, lambda *_: (0,) * nd, pipeline_mode=pl.Buffered(1))


def _params(*sem, vmem=VMEM_LIMIT_BYTES):
    return pltpu.CompilerParams(dimension_semantics=sem, vmem_limit_bytes=vmem)


def _ada_kernel(cctx_ref, c_ref, w_ref, b_ref, o_ref, rows_ref):
    rows_ref[...] = jnp.zeros_like(rows_ref)
    rows_ref[0:1, :] = cctx_ref[...]
    rows_ref[1:1 + c_ref.shape[0], :] = c_ref[...]
    c = rows_ref[...]
    s = (c * _sigmoid(c)).astype(BF16)
    o_ref[:, 0, :] = _dot(s, w_ref[...].astype(BF16)) + b_ref[...]


def _ada(c_ctx, c, w_ada, b_ada):
    n = w_ada.shape[1]
    assert 1 + c.shape[0] <= MOD_ROWS
    return pl.pallas_call(
        _ada_kernel,
        out_shape=jax.ShapeDtypeStruct((MOD_ROWS, 1, n), F32),
        grid=(n // ADA_TILE,),
        in_specs=[pl.BlockSpec(c_ctx.shape, lambda j: (0, 0)),
                  pl.BlockSpec(c.shape, lambda j: (0, 0)),
                  pl.BlockSpec((D_MODEL, ADA_TILE), lambda j: (0, j)),
                  pl.BlockSpec((1, ADA_TILE), lambda j: (0, j))],
        out_specs=pl.BlockSpec((MOD_ROWS, 1, ADA_TILE), lambda j: (0, 0, j)),
        scratch_shapes=[pltpu.VMEM((MOD_ROWS, D_MODEL), F32)],
        compiler_params=_params("arbitrary"),
        name="ada",
    )(c_ctx, c, w_ada, b_ada)


W_BLOCK = 512
HEAD_COLS = 2 * KEY_DIM + 2 * VAL_DIM
TAIL_OFF = HEAD_COLS + 2 * GATE_RANK
TAIL_SHIFT = TAIL_OFF % MXU_LANES_V7X
N_HEAD_BLOCKS = HEAD_COLS // W_BLOCK
N_W_BLOCKS = N_HEAD_BLOCKS + 5 * CONV_DIM // W_BLOCK


def _inproj_kernel(xa_ref, xb_ref, mod_ref, n1_ref, wa_ref, wb_ref, wgf_ref, wgb_ref, bgf_ref, bgb_ref,
                   q_ref, k_ref, v_ref, og_ref, p_ref, cbg_ref, g_ref, w_scr, wcode_scr, wgk_scr,
                   *, tiles_a, seg):
    step = pl.program_id(0)

    @pl.when(step < N_HEAD_BLOCKS)
    def _():
        w_scr[step] = wa_ref[...].T.astype(BF16)
        wgk_scr[...] = jnp.zeros_like(wgk_scr)
        wgk_scr[0:GATE_RANK, 0:KEY_DIM] = wgf_ref[...].astype(BF16)
        wgk_scr[GATE_RANK:2 * GATE_RANK, KEY_DIM:] = wgb_ref[...].astype(BF16)
        lane = lax.broadcasted_iota(jnp.int32, (D_MODEL, MXU_LANES_V7X), 1)
        wcode_scr[...] = jnp.where(lane < 2 * GATE_RANK, wb_ref[...].T, 0.0).astype(BF16)

    @pl.when(jnp.logical_and(step >= N_HEAD_BLOCKS, step < N_W_BLOCKS))
    def _():
        rows = jnp.concatenate([wa_ref[TAIL_SHIFT:, :], wb_ref[:TAIL_SHIFT, :]], axis=0)
        w_scr[step] = rows.T.astype(BF16)

    @pl.when(step >= N_W_BLOCKS)
    def _():
        first_group = step - N_W_BLOCKS < tiles_a
        mod = mod_ref[...]
        sh1 = mod[:, 0:D_MODEL]
        sc1 = mod[:, D_MODEL:2 * D_MODEL]
        x = jnp.where(first_group, xa_ref[...], xb_ref[...])
        xn = (_rms(x, n1_ref[...]) * (1.0 + sc1) + sh1).astype(BF16)
        halves = range(CONV_DIM // W_BLOCK)
        head = lambda blk: _dot(xn, w_scr[blk])
        tail = lambda grp, j: _dot(xn, w_scr[N_HEAD_BLOCKS + grp * len(halves) + j])
        cols = lambda j: slice(j * W_BLOCK, (j + 1) * W_BLOCK)

        z = (_dot(_dot(xn, wcode_scr[...]).astype(BF16), wgk_scr[...])
             + jnp.concatenate([bgf_ref[...], bgb_ref[...]], axis=1))
        g = (jnp.minimum(z, 0.0) - jnp.log(1.0 + jnp.exp(-jnp.abs(z)))) * (LOG2_E / GATE_NORMALIZER)
        g_hi = g.astype(BF16)
        g_lo = (g - g_hi.astype(F32)).astype(BF16)

        q_ref[...] = (head(0) * (HEAD_K ** -0.5)).astype(BF16)
        k_ref[...] = head(1).astype(BF16)
        for j in halves:
            v_ref[:, cols(j)] = head(2 + j).astype(BF16)

        r = lax.broadcasted_iota(jnp.int32, (seg, seg), 0)
        c = lax.broadcasted_iota(jnp.int32, (seg, seg), 1)
        lower = jnp.where(c <= r, 1.0, 0.0).astype(BF16)
        upper = jnp.where(c >= r, 1.0, 0.0).astype(BF16)
        for s0 in range(0, g.shape[0], seg):
            rows = slice(s0, s0 + seg)
            for tri, cs in ((lower, slice(0, KEY_DIM)), (upper, slice(KEY_DIM, 2 * KEY_DIM))):
                terms = jnp.concatenate([g_hi[rows, cs], g_lo[rows, cs]], axis=0)
                g_ref[rows, cs] = _dot(jnp.concatenate([tri, tri], axis=1), terms)

        for j in halves:
            g_out = head(2 + len(halves) + j)
            og_ref[:, cols(j)] = (g_out * _sigmoid(g_out) * _sigmoid(tail(3, j))).astype(BF16)
            p_ref[:, cols(j)] = (tail(1, j) * tail(2, j)).astype(BF16)
            cbg_ref[:, cols(j)] = (_sigmoid(tail(4, j)) * tail(0, j)).astype(BF16)


def _inproj(xa, xb, mod3, n1g, w_in_t, wgf, wgb, bgf, bgb, *, tm, seg, mod_row):
    tiles_a, tiles_b = xa.shape[0] // tm, xb.shape[0] // tm
    n = xa.shape[0] + xb.shape[0]
    assert seg & (seg - 1) == 0 and tm % seg == 0
    assert KEY_DIM == W_BLOCK and HEAD_COLS % W_BLOCK == 0 and W_BLOCK % MXU_LANES_V7X == 0
    tile = lambda i: jnp.maximum(i - N_W_BLOCKS, 0)
    row = lambda i: (tile(i), 0)
    lanes_per_block = W_BLOCK // MXU_LANES_V7X
    wide = lambda w, dt: (jax.ShapeDtypeStruct((n, w), dt), pl.BlockSpec((tm, w), row))
    outs = [wide(KEY_DIM, BF16), wide(KEY_DIM, BF16), wide(VAL_DIM, BF16), wide(VAL_DIM, BF16),
            wide(CONV_DIM, BF16), wide(CONV_DIM, BF16), wide(2 * KEY_DIM, F32)]
    return pl.pallas_call(
        functools.partial(_inproj_kernel, tiles_a=tiles_a, seg=seg),
        out_shape=[o[0] for o in outs],
        grid=(N_W_BLOCKS + tiles_a + tiles_b,),
        in_specs=[pl.BlockSpec((tm, D_MODEL), lambda i: (jnp.minimum(tile(i), tiles_a - 1), 0)),
                  pl.BlockSpec((tm, D_MODEL), lambda i: (jnp.maximum(tile(i) - tiles_a, 0), 0)),
                  pl.BlockSpec((None, 1, N_MOD * D_MODEL), lambda i: (mod_row(tile(i)), 0, 0)),
                  _const_spec(n1g.shape),
                  pl.BlockSpec((W_BLOCK, D_MODEL), lambda i: (jnp.minimum(i, N_W_BLOCKS - 1), 0)),
                  pl.BlockSpec((MXU_LANES_V7X, D_MODEL),
                               lambda i: (jnp.where(i < N_HEAD_BLOCKS, HEAD_COLS // MXU_LANES_V7X,
                                                    (jnp.minimum(i, N_W_BLOCKS - 1) + 1) * lanes_per_block), 0)),
                  _const_spec(wgf.shape), _const_spec(wgb.shape), _const_spec(bgf.shape), _const_spec(bgb.shape)],
        out_specs=[o[1] for o in outs],
        scratch_shapes=[pltpu.VMEM((N_W_BLOCKS, D_MODEL, W_BLOCK), BF16),
                        pltpu.VMEM((D_MODEL, MXU_LANES_V7X), BF16),
                        pltpu.VMEM((MXU_LANES_V7X, 2 * KEY_DIM), BF16)],
        compiler_params=_params("arbitrary"),
        name="in_proj",
    )(xa, xb, mod3, n1g, w_in_t, w_in_t, wgf, wgb, bgf, bgb)


def _cat(parts, axis=0):
    return parts[0] if len(parts) == 1 else jnp.concatenate(parts, axis=axis)


def _join_prefix(b, seg):
    parts = [b[:seg]]
    for c in range(1, b.shape[0] // seg):
        parts.append(b[c * seg:(c + 1) * seg] + parts[-1][seg - 1:seg, :])
    return _cat(parts)


def _join_suffix(b, seg):
    n = b.shape[0] // seg
    parts = [b[(n - 1) * seg:]]
    for c in range(n - 2, -1, -1):
        parts.insert(0, b[c * seg:(c + 1) * seg] + parts[0][0:1, :])
    return _cat(parts)


def _diag_args(b, off):
    return _cat([b[c:c + DIAG_BLOCK] - b[c + off:c + off + 1, :] for c in range(0, b.shape[0], DIAG_BLOCK)])


def _pair_args(bf, sb, s):
    lhs, rhs = [], []
    for e0 in range(0, bf.shape[0], 2 * s):
        o0 = e0 + s
        rf = bf[o0 - 1:o0, :]
        rb = sb[o0:o0 + 1, :]
        lhs += [sb[e0:o0] - rb, bf[o0:o0 + s] - rf]
        rhs += [rf - bf[e0:o0], rb - sb[o0:o0 + s]]
    return _cat(lhs), _cat(rhs)


def _mixer_kernel(*refs, seq, heads, seg, has_state_in, has_state_out, has_order_token):
    (q_ref, k_ref, v_ref, gf_ref, gb_ref, og_ref, p_ref, cbg_ref, cw_ref, gn_ref, wo_ref, x_ref,
     mod_ref) = refs[:13]
    refs = refs[13:]
    if has_state_in:
        s0f_ref, s0b_ref = refs[:2]
        refs = refs[2:]
    if has_order_token:
        refs = refs[1:]
    x1_ref = refs[0]
    refs = refs[1:]
    if has_state_out:
        sf_ref, sb_ref = refs[:2]
        refs = refs[2:]
    a_scr = refs[0]
    hsteps = N_HEADS // heads
    half = SUPER_BLOCK // 2
    n_seq = x_ref.shape[0] // seq

    rows_v = lax.broadcasted_iota(jnp.int32, (seq, HEAD_V), 0)
    r_loc = lax.broadcasted_iota(jnp.int32, (DIAG_BLOCK, half), 0)
    lane = lax.broadcasted_iota(jnp.int32, (DIAG_BLOCK, half), 1)
    ones = jnp.ones((HEAD_V, half), BF16)
    scale = lambda t, e: (t.astype(F32) * e).astype(BF16)

    mixes = [[] for _ in range(n_seq)]
    for si, h in [(si, h) for h in range(heads) for si in range(n_seq)]:
        rq = slice(si * seq, (si + 1) * seq)
        a_ref = a_scr.at[(si * heads + h) % a_scr.shape[0]]
        ks = slice(h * HEAD_K, (h + 1) * HEAD_K)
        vs = slice(h * HEAD_V, (h + 1) * HEAD_V)
        q = q_ref[rq, ks]
        k = k_ref[rq, ks]
        v = v_ref[rq, vs]
        bf = _join_prefix(gf_ref[rq, ks], seg)
        sb = _join_suffix(gb_ref[rq, ks], seg)

        e0f = jnp.exp2(_diag_args(bf, DIAG_BLOCK // 2))
        e0b = jnp.exp2(_diag_args(sb, DIAG_BLOCK // 2 - 1))
        q0f, k0f = scale(q, e0f), scale(k, 1.0 / e0f)
        q0b, k0b = scale(q, e0b), scale(k, 1.0 / e0b)

        lhs_levels, rhs_levels = [], []
        s = DIAG_BLOCK
        while s < seq:
            la, ra = _pair_args(bf, sb, s)
            lhs_levels.append(scale(q, jnp.exp2(la)))
            rhs_levels.append(scale(k, jnp.exp2(ra)))
            s *= 2

        for base in range(0, seq, SUPER_BLOCK):
            rs = slice(base, base + SUPER_BLOCK)
            m0f = _dot_nt(q0f[rs], k0f[rs])
            m0b = _dot_nt(q0b[rs], k0b[rs])
            m1 = _dot_nt(lhs_levels[0][rs], rhs_levels[0][rs])
            m2 = _dot_nt(lhs_levels[1][rs], rhs_levels[1][rs])
            for bi in range(SUPER_BLOCK // DIAG_BLOCK):
                rr = slice(bi * DIAG_BLOCK, (bi + 1) * DIAG_BLOCK)
                for lt in range(2):
                    cc = slice(lt * half, (lt + 1) * half)
                    if lt == bi // 2:
                        c_loc = lane - DIAG_BLOCK * (bi % 2)
                        in_diag = (lane >= DIAG_BLOCK) if bi % 2 else (lane < DIAG_BLOCK)
                        diag = (jnp.where(c_loc <= r_loc, m0f[rr, cc], 0.0)
                                + jnp.where(c_loc >= r_loc, m0b[rr, cc], 0.0))
                        piece = jnp.where(in_diag, diag, m1[rr, cc])
                    else:
                        piece = m2[rr, cc]
                    a_ref[base + bi * DIAG_BLOCK:base + (bi + 1) * DIAG_BLOCK,
                          base + lt * half:base + (lt + 1) * half] = piece.astype(BF16)
        s = SUPER_BLOCK
        lev = 2
        while s < seq:
            lhs, rhs = lhs_levels[lev], rhs_levels[lev]
            for e0 in range(0, seq, 2 * s):
                ev = slice(e0, e0 + s)
                od = slice(e0 + s, e0 + 2 * s)
                a_ref[od, ev] = _dot_nt(lhs[od], rhs[ev]).astype(BF16)
                a_ref[ev, od] = _dot_nt(lhs[ev], rhs[od]).astype(BF16)
            s *= 2
            lev += 1

        o = _dot(a_ref[...], v)
        if has_state_in:
            qi = jnp.concatenate([scale(q, jnp.exp2(bf)), scale(q, jnp.exp2(sb))], axis=1)
            s0 = jnp.concatenate([s0f_ref[si, h].astype(BF16), s0b_ref[si, h].astype(BF16)], axis=0)
            o = o + _dot(qi, s0)
        if has_state_out:
            sf_ref[si, h] = _dot_tn(scale(k, jnp.exp2(bf[seq - 1:seq, :] - bf)), v)
            sb_ref[si, h] = _dot_tn(scale(k, jnp.exp2(sb[0:1, :] - sb)), v)

        inv = lax.rsqrt(_dot((o * o).astype(BF16), ones) * (1.0 / HEAD_V) + EPS)
        o = jnp.concatenate([o[:, :half] * inv, o[:, half:] * inv], axis=1) * gn_ref[...]
        p = p_ref[rq, vs].astype(F32)
        conv = (jnp.where(rows_v >= 1, pltpu.roll(p, 1, axis=0), 0.0) * cw_ref[0:1, vs] + p * cw_ref[1:2, vs]
                + jnp.where(rows_v < seq - 1, pltpu.roll(p, seq - 1, axis=0), 0.0) * cw_ref[2:3, vs])
        mixes[si].append(og_ref[rq, vs] * o.astype(BF16) + cbg_ref[rq, vs] * conv.astype(BF16))

    contrib = _dot(_cat([_cat(m, axis=1) for m in mixes], axis=0), wo_ref[...])
    ga1 = mod_ref[:, 2 * D_MODEL:3 * D_MODEL]
    if hsteps == 1:
        x1_ref[...] = x_ref[...] + ga1 * contrib
    else:
        hstep = pl.program_id(1)

        @pl.when(hstep == 0)
        def _():
            x1_ref[...] = x_ref[...] + ga1 * contrib

        @pl.when(hstep != 0)
        def _():
            x1_ref[...] += ga1 * contrib


def _mixer(x, mod3, proj, conv_w, gn, wo, states, *, seq, heads, seqs_per_step, seg, first_row, mod_row,
           has_state_out, run_after=None):
    q, k, v, og, p, cbg, g = proj
    n = x.shape[0]
    rows = seqs_per_step * seq
    hsteps = N_HEADS // heads
    has_state_in = states is not None
    assert seq % SUPER_BLOCK == 0 and seq % seg == 0 and first_row % rows == 0 and n % rows == 0
    b0 = first_row // rows
    a_buffers = max(1, min(seqs_per_step * heads, A_SCRATCH_BYTES // (2 * seq * seq)))
    col = lambda b, h: (b0 + b, h)
    st_block = (seqs_per_step, None, heads, HEAD_K, HEAD_V)
    st_map = lambda b, h: (b, 0, h, 0, 0)
    in_specs = [pl.BlockSpec((rows, heads * HEAD_K), col),
                pl.BlockSpec((rows, heads * HEAD_K), col),
                pl.BlockSpec((rows, heads * HEAD_V), col),
                pl.BlockSpec((rows, heads * HEAD_K), col),
                pl.BlockSpec((rows, heads * HEAD_K), lambda b, h: (b0 + b, hsteps + h)),
                pl.BlockSpec((rows, heads * HEAD_V), col),
                pl.BlockSpec((rows, heads * HEAD_V), col),
                pl.BlockSpec((rows, heads * HEAD_V), col),
                pl.BlockSpec((3, heads * HEAD_V), lambda b, h: (0, h)),
                pl.BlockSpec((1, HEAD_V), lambda b, h: (0, 0)),
                (pl.BlockSpec((heads * HEAD_V, D_MODEL), lambda b, h: (h, 0)) if hsteps > 1
                 else _const_spec((heads * HEAD_V, D_MODEL))),
                pl.BlockSpec((rows, D_MODEL), lambda b, h: (b, 0)),
                pl.BlockSpec((None, 1, N_MOD * D_MODEL), lambda b, h: (mod_row(b), 0, 0))]
    args = [q, k, v, g, g, og, p, cbg, conv_w, gn, wo, x, mod3]
    if has_state_in:
        in_specs += [pl.BlockSpec(st_block, st_map)] * 2
        args += list(states)
    if run_after is not None:
        in_specs.append(pl.BlockSpec(memory_space=pl.ANY))
        args.append(run_after)
    out_shape = [jax.ShapeDtypeStruct((n, D_MODEL), F32)]
    out_specs = [pl.BlockSpec((rows, D_MODEL), lambda b, h: (b, 0))]
    if has_state_out:
        out_shape += [jax.ShapeDtypeStruct((n // seq, 1, N_HEADS, HEAD_K, HEAD_V), F32)] * 2
        out_specs += [pl.BlockSpec(st_block, st_map)] * 2
    return pl.pallas_call(
        functools.partial(_mixer_kernel, seq=seq, heads=heads, seg=seg, has_state_in=has_state_in,
                          has_state_out=has_state_out, has_order_token=run_after is not None),
        out_shape=out_shape,
        grid=(n // rows, hsteps),
        in_specs=in_specs,
        out_specs=out_specs,
        scratch_shapes=[pltpu.VMEM((a_buffers, seq, seq), BF16)],
        compiler_params=_params("arbitrary", "arbitrary", vmem=MIXER_VMEM_LIMIT_BYTES),
        name="mixer",
    )(*args)


N_FFN_TILES = D_FF // FFN_TILE


def _ffn_kernel(x1_ref, mod_ref, n2_ref, wu_ref, wg_ref, cw_ref, cb_ref, wd_ref, nf_ref, o_ref, acc_ref,
                *, seq, width):
    cols = lambda f: slice(f * FFN_TILE, (f + 1) * FFN_TILE)
    _ffn_tile(x1_ref, mod_ref, n2_ref, cw_ref, cb_ref, nf_ref, o_ref, acc_ref, seq, width,
              lambda f: wu_ref[:, cols(f)], lambda f: wg_ref[:, cols(f)], lambda f: wd_ref[cols(f), :])


def _ffn_staging_kernel(x1_ref, mod_ref, n2_ref, wu_ref, wg_ref, cw_ref, cb_ref, wd_ref, nf_ref,
                        o_ref, wu_out, wg_out, wd_out, wu_scr, wg_scr, wd_scr, acc_ref, *, seq, width):
    step = pl.program_id(0)

    @pl.when(step < N_FFN_TILES)
    def _():
        for src, scr, out in ((wu_ref, wu_scr, wu_out), (wg_ref, wg_scr, wg_out), (wd_ref, wd_scr, wd_out)):
            w = src[...].astype(BF16)
            scr[step] = w
            out[...] = w

    @pl.when(step >= N_FFN_TILES)
    def _():
        _ffn_tile(x1_ref, mod_ref, n2_ref, cw_ref, cb_ref, nf_ref, o_ref, acc_ref, seq, width,
                  lambda f: wu_scr[f], lambda f: wg_scr[f], lambda f: wd_scr[f])


def _ffn_tile(x1_ref, mod_ref, n2_ref, cw_ref, cb_ref, nf_ref, o_ref, acc_ref, seq, width, wu, wg, wd):
    tm = x1_ref.shape[0]
    mod = mod_ref[...]
    sh2 = mod[:, 3 * D_MODEL:4 * D_MODEL]
    sc2 = mod[:, 4 * D_MODEL:5 * D_MODEL]
    ga2 = mod[:, 5 * D_MODEL:6 * D_MODEL]
    x1 = x1_ref[...]
    xb = (_rms(x1, n2_ref[...]) * (1.0 + sc2) + sh2).astype(BF16)

    rows = lax.broadcasted_iota(jnp.int32, (tm, FFN_TILE), 0)
    col_in_row = rows & (width - 1)
    has_left = col_in_row != 0
    has_right = col_in_row != width - 1
    n_rows = seq // width
    zrow = jnp.zeros((width, FFN_TILE), F32)

    def up_rows(z):
        return jnp.concatenate([zrow, z[:tm - width]], axis=0)

    def down_rows(z):
        return jnp.concatenate([z[width:], zrow], axis=0)

    def up_and_gate(f):
        return _dot(xb, wu(f)), _dot(xb, wg(f))

    ahead = up_and_gate(0)
    for f in range(N_FFN_TILES):
        fs = slice(f * FFN_TILE, (f + 1) * FFN_TILE)
        up, gate = ahead
        if f + 1 < N_FFN_TILES:
            ahead = up_and_gate(f + 1)
        left = jnp.where(has_left, pltpu.roll(up, 1, axis=0), 0.0)
        right = jnp.where(has_right, pltpu.roll(up, tm - 1, axis=0), 0.0)
        w = lambda i, j: cw_ref[i, j:j + 1, fs]
        cv = left * w(1, 0) + up * w(1, 1) + right * w(1, 2) + cb_ref[:, fs]
        if n_rows > 1:
            cv = cv + (up_rows(left) * w(0, 0) + up_rows(up) * w(0, 1) + up_rows(right) * w(0, 2)
                       + down_rows(left) * w(2, 0) + down_rows(up) * w(2, 1) + down_rows(right) * w(2, 2))
        act = (cv * _sigmoid(cv) * gate).astype(BF16)
        part = _dot(act, wd(f))
        if f == 0:
            acc_ref[...] = part
        else:
            acc_ref[...] += part

    o_ref[...] = _rms(x1 + ga2 * acc_ref[...], nf_ref[...])


def _ffn_staging(x1, mod3, n2g, wu, wg, cw, cb, wd, nfg, *, tm, seq, width, mod_row):
    assert width & (width - 1) == 0 and tm % seq == 0 and (seq == width or tm == seq)
    tile = lambda i: jnp.maximum(i - N_FFN_TILES, 0)
    w_step = lambda i: jnp.minimum(i, N_FFN_TILES - 1)
    col_blk = pl.BlockSpec((D_MODEL, FFN_TILE), lambda i: (0, w_step(i)))
    row_blk = pl.BlockSpec((FFN_TILE, D_MODEL), lambda i: (w_step(i), 0))
    return pl.pallas_call(
        functools.partial(_ffn_staging_kernel, seq=seq, width=width),
        out_shape=[jax.ShapeDtypeStruct(x1.shape, F32), jax.ShapeDtypeStruct(wu.shape, BF16),
                   jax.ShapeDtypeStruct(wg.shape, BF16), jax.ShapeDtypeStruct(wd.shape, BF16)],
        grid=(N_FFN_TILES + x1.shape[0] // tm,),
        in_specs=[pl.BlockSpec((tm, D_MODEL), lambda i: (tile(i), 0)),
                  pl.BlockSpec((None, 1, N_MOD * D_MODEL), lambda i: (mod_row(tile(i)), 0, 0)),
                  _const_spec(n2g.shape), col_blk, col_blk,
                  _const_spec(cw.shape), _const_spec(cb.shape), row_blk,
                  _const_spec(nfg.shape)],
        out_specs=[pl.BlockSpec((tm, D_MODEL), lambda i: (tile(i), 0)), col_blk, col_blk, row_blk],
        scratch_shapes=[pltpu.VMEM((N_FFN_TILES, D_MODEL, FFN_TILE), BF16),
                        pltpu.VMEM((N_FFN_TILES, D_MODEL, FFN_TILE), BF16),
                        pltpu.VMEM((N_FFN_TILES, FFN_TILE, D_MODEL), BF16),
                        pltpu.VMEM((tm, D_MODEL), F32)],
        compiler_params=_params("arbitrary"),
        name="ffn_staging",
    )(x1, mod3, n2g, wu, wg, cw, cb, wd, nfg)


def _ffn(x1, mod3, n2g, wu, wg, cw, cb, wd, nfg, *, tm, seq, width, mod_row):
    assert width & (width - 1) == 0 and tm % seq == 0 and (seq == width or tm == seq)
    return pl.pallas_call(
        functools.partial(_ffn_kernel, seq=seq, width=width),
        out_shape=jax.ShapeDtypeStruct(x1.shape, F32),
        grid=(x1.shape[0] // tm,),
        in_specs=[pl.BlockSpec((tm, D_MODEL), lambda i: (i, 0)),
                  pl.BlockSpec((None, 1, N_MOD * D_MODEL), lambda i: (mod_row(i), 0, 0)),
                  _const_spec(n2g.shape), _const_spec(wu.shape), _const_spec(wg.shape),
                  _const_spec(cw.shape), _const_spec(cb.shape), _const_spec(wd.shape),
                  _const_spec(nfg.shape)],
        out_specs=pl.BlockSpec((tm, D_MODEL), lambda i: (i, 0)),
        scratch_shapes=[pltpu.VMEM((tm, D_MODEL), F32)],
        compiler_params=_params("arbitrary"),
        name="ffn",
    )(x1, mod3, n2g, wu, wg, cw, cb, wd, nfg)


def kernel(x_prompt, x_sample, c, state_gla_fwd, state_gla_bwd, c_ctx, w_ada, b_ada, norm1_g, w_in, w_gk_f,
           b_gk_f, w_gk_b, b_gk_b, gla_norm_g, conv_mix_w, w_out, norm2_g, ffn_w_up, ffn_w_gate,
           ffn_conv_w, ffn_conv_b, ffn_w_down, normf_g):
    bp, lp, d = x_prompt.shape
    bs, ls, _ = x_sample.shape
    assert w_ada.shape[0] == 1 and d == D_MODEL and bs + 1 <= MOD_ROWS

    mod3 = _ada(c_ctx[None, :], c, w_ada[0], b_ada)

    assert IN_SPLITS == (KEY_DIM, KEY_DIM, VAL_DIM, VAL_DIM, GATE_RANK, GATE_RANK) + (CONV_DIM,) * 5
    wo = w_out[0].astype(BF16)
    cw9 = ffn_conv_w[0]
    cb = ffn_conv_b
    n1g, n2g, nfg, gn = norm1_g, norm2_g, normf_g[None, :], gla_norm_g

    xp, xs = x_prompt.reshape(-1, d), x_sample.reshape(-1, d)
    tiles_p = xp.shape[0] // INPROJ_ROWS
    seg = min(lp, ls, INPROJ_ROWS)
    proj = _inproj(xp, xs, mod3, n1g, w_in[0].T, w_gk_f[0], w_gk_b[0], b_gk_f, b_gk_b, tm=INPROJ_ROWS, seg=seg,
                   mod_row=lambda t: jnp.where(t < tiles_p, 0, 1 + ((t - tiles_p) * INPROJ_ROWS) // ls))

    mix = functools.partial(_mixer, mod3=mod3, proj=proj, conv_w=conv_mix_w[0], gn=gn, wo=wo, seg=seg)
    x1p, new_f, new_b = mix(xp, states=None, seq=lp, heads=N_HEADS, seqs_per_step=PROMPT_SEQS_PER_STEP,
                            first_row=0, mod_row=lambda b: 0, has_state_out=True)
    y_prompt, wu, wg, wd = _ffn_staging(x1p, mod3, n2g, ffn_w_up[0], ffn_w_gate[0], cw9, cb, ffn_w_down[0], nfg,
                                        tm=FFN_ROWS, seq=lp, width=lp, mod_row=lambda i: 0)
    x1s, = mix(xs, states=(state_gla_fwd, state_gla_bwd), seq=ls, heads=N_HEADS, seqs_per_step=1,
               first_row=xp.shape[0], mod_row=lambda b: 1 + b, has_state_out=False, run_after=wd)
    y_sample = _ffn(x1s, mod3, n2g, wu, wg, cw9, cb, wd, nfg, tm=FFN_ROWS, seq=ls, width=GRID_W,
                    mod_row=lambda i: 1 + (i * FFN_ROWS) // ls)
    return y_prompt.reshape(x_prompt.shape), y_sample.reshape(x_sample.shape), new_f, new_b
```

```python
import functools

import jax
import jax.numpy as jnp
from jax import lax
from jax.experimental import pallas as pl
from jax.experimental.pallas import tpu as pltpu

D_MODEL = 1024
N_HEADS = 4
HEAD_K = 128
HEAD_V = 256
KEY_DIM = N_HEADS * HEAD_K
VAL_DIM = N_HEADS * HEAD_V
GATE_RANK = 16
GATE_NORMALIZER = 16.0
LOG2_E = 1.4426950408889634
CONV_DIM = D_MODEL
D_FF = 2816
N_MOD = 6
EPS = 1e-6
GRID_W = 64
IN_SPLITS = (KEY_DIM, KEY_DIM, VAL_DIM, VAL_DIM, GATE_RANK, GATE_RANK,
             CONV_DIM, CONV_DIM, CONV_DIM, VAL_DIM, CONV_DIM)

DIAG_BLOCK = 64
SUPER_BLOCK = 256
MXU_LANES_V7X = 128
FFN_TILE = 256
FFN_ROWS = 1024
INPROJ_ROWS = 512
PROMPT_SEQS_PER_STEP = 2
ADA_TILE = 1024
MOD_ROWS = 8
VMEM_LIMIT_BYTES = 56 * 1024 * 1024
A_SCRATCH_BYTES = 4 * 1024 * 1024
MIXER_VMEM_LIMIT_BYTES = 61 * 1024 * 1024

F32 = jnp.float32
BF16 = jnp.bfloat16


def _dot(a, b):
    return jnp.dot(a, b, preferred_element_type=F32)


def _dot_nt(a, b):
    return lax.dot_general(a, b, (((1,), (1,)), ((), ())), preferred_element_type=F32)


def _dot_tn(a, b):
    return lax.dot_general(a, b, (((0,), (0,)), ((), ())), preferred_element_type=F32)


def _sigmoid(x):
    return 1.0 / (1.0 + jnp.exp(-x))


def _rms(x, g):
    return x * lax.rsqrt(jnp.mean(x * x, axis=-1, keepdims=True) + EPS) * g


def _const_spec(shape):
    nd = len(shape)
    return pl.BlockSpec(shape, lambda *_: (0,) * nd, pipeline_mode=pl.Buffered(1))


def _params(*sem, vmem=VMEM_LIMIT_BYTES):
    return pltpu.CompilerParams(dimension_semantics=sem, vmem_limit_bytes=vmem)


def _ada_kernel(cctx_ref, c_ref, w_ref, b_ref, o_ref, rows_ref):
    rows_ref[...] = jnp.zeros_like(rows_ref)
    rows_ref[0:1, :] = cctx_ref[...]
    rows_ref[1:1 + c_ref.shape[0], :] = c_ref[...]
    c = rows_ref[...]
    s = (c * _sigmoid(c)).astype(BF16)
    o_ref[:, 0, :] = _dot(s, w_ref[...].astype(BF16)) + b_ref[...]


def _ada(c_ctx, c, w_ada, b_ada):
    n = w_ada.shape[1]
    assert 1 + c.shape[0] <= MOD_ROWS
    return pl.pallas_call(
        _ada_kernel,
        out_shape=jax.ShapeDtypeStruct((MOD_ROWS, 1, n), F32),
        grid=(n // ADA_TILE,),
        in_specs=[pl.BlockSpec(c_ctx.shape, lambda j: (0, 0)),
                  pl.BlockSpec(c.shape, lambda j: (0, 0)),
                  pl.BlockSpec((D_MODEL, ADA_TILE), lambda j: (0, j)),
                  pl.BlockSpec((1, ADA_TILE), lambda j: (0, j))],
        out_specs=pl.BlockSpec((MOD_ROWS, 1, ADA_TILE), lambda j: (0, 0, j)),
        scratch_shapes=[pltpu.VMEM((MOD_ROWS, D_MODEL), F32)],
        compiler_params=_params("arbitrary"),
        name="ada",
    )(c_ctx, c, w_ada, b_ada)


W_BLOCK = 512
HEAD_COLS = 2 * KEY_DIM + 2 * VAL_DIM
TAIL_OFF = HEAD_COLS + 2 * GATE_RANK
TAIL_SHIFT = TAIL_OFF % MXU_LANES_V7X
N_HEAD_BLOCKS = HEAD_COLS // W_BLOCK
N_W_BLOCKS = N_HEAD_BLOCKS + 5 * CONV_DIM // W_BLOCK


def _inproj_kernel(xa_ref, xb_ref, mod_ref, n1_ref, wa_ref, wb_ref, wgf_ref, wgb_ref, bgf_ref, bgb_ref,
                   q_ref, k_ref, v_ref, og_ref, p_ref, cbg_ref, g_ref, w_scr, wcode_scr, wgk_scr,
                   *, tiles_a, seg):
    step = pl.program_id(0)

    @pl.when(step < N_HEAD_BLOCKS)
    def _():
        w_scr[step] = wa_ref[...].T.astype(BF16)
        wgk_scr[...] = jnp.zeros_like(wgk_scr)
        wgk_scr[0:GATE_RANK, 0:KEY_DIM] = wgf_ref[...].astype(BF16)
        wgk_scr[GATE_RANK:2 * GATE_RANK, KEY_DIM:] = wgb_ref[...].astype(BF16)
        lane = lax.broadcasted_iota(jnp.int32, (D_MODEL, MXU_LANES_V7X), 1)
        wcode_scr[...] = jnp.where(lane < 2 * GATE_RANK, wb_ref[...].T, 0.0).astype(BF16)

    @pl.when(jnp.logical_and(step >= N_HEAD_BLOCKS, step < N_W_BLOCKS))
    def _():
        rows = jnp.concatenate([wa_ref[TAIL_SHIFT:, :], wb_ref[:TAIL_SHIFT, :]], axis=0)
        w_scr[step] = rows.T.astype(BF16)

    @pl.when(step >= N_W_BLOCKS)
    def _():
        first_group = step - N_W_BLOCKS < tiles_a
        mod = mod_ref[...]
        sh1 = mod[:, 0:D_MODEL]
        sc1 = mod[:, D_MODEL:2 * D_MODEL]
        x = jnp.where(first_group, xa_ref[...], xb_ref[...])
        xn = (_rms(x, n1_ref[...]) * (1.0 + sc1) + sh1).astype(BF16)
        halves = range(CONV_DIM // W_BLOCK)
        head = lambda blk: _dot(xn, w_scr[blk])
        tail = lambda grp, j: _dot(xn, w_scr[N_HEAD_BLOCKS + grp * len(halves) + j])
        cols = lambda j: slice(j * W_BLOCK, (j + 1) * W_BLOCK)

        z = (_dot(_dot(xn, wcode_scr[...]).astype(BF16), wgk_scr[...])
             + jnp.concatenate([bgf_ref[...], bgb_ref[...]], axis=1))
        g = (jnp.minimum(z, 0.0) - jnp.log(1.0 + jnp.exp(-jnp.abs(z)))) * (LOG2_E / GATE_NORMALIZER)
        g_hi = g.astype(BF16)
        g_lo = (g - g_hi.astype(F32)).astype(BF16)

        q_ref[...] = (head(0) * (HEAD_K ** -0.5)).astype(BF16)
        k_ref[...] = head(1).astype(BF16)
        for j in halves:
            v_ref[:, cols(j)] = head(2 + j).astype(BF16)

        r = lax.broadcasted_iota(jnp.int32, (seg, seg), 0)
        c = lax.broadcasted_iota(jnp.int32, (seg, seg), 1)
        lower = jnp.where(c <= r, 1.0, 0.0).astype(BF16)
        upper = jnp.where(c >= r, 1.0, 0.0).astype(BF16)
        for s0 in range(0, g.shape[0], seg):
            rows = slice(s0, s0 + seg)
            for tri, cs in ((lower, slice(0, KEY_DIM)), (upper, slice(KEY_DIM, 2 * KEY_DIM))):
                terms = jnp.concatenate([g_hi[rows, cs], g_lo[rows, cs]], axis=0)
                g_ref[rows, cs] = _dot(jnp.concatenate([tri, tri], axis=1), terms)

        for j in halves:
            g_out = head(2 + len(halves) + j)
            og_ref[:, cols(j)] = (g_out * _sigmoid(g_out) * _sigmoid(tail(3, j))).astype(BF16)
            p_ref[:, cols(j)] = (tail(1, j) * tail(2, j)).astype(BF16)
            cbg_ref[:, cols(j)] = (_sigmoid(tail(4, j)) * tail(0, j)).astype(BF16)


def _inproj(xa, xb, mod3, n1g, w_in_t, wgf, wgb, bgf, bgb, *, tm, seg, mod_row):
    tiles_a, tiles_b = xa.shape[0] // tm, xb.shape[0] // tm
    n = xa.shape[0] + xb.shape[0]
    assert seg & (seg - 1) == 0 and tm % seg == 0
    assert KEY_DIM == W_BLOCK and HEAD_COLS % W_BLOCK == 0 and W_BLOCK % MXU_LANES_V7X == 0
    tile = lambda i: jnp.maximum(i - N_W_BLOCKS, 0)
    row = lambda i: (tile(i), 0)
    lanes_per_block = W_BLOCK // MXU_LANES_V7X
    wide = lambda w, dt: (jax.ShapeDtypeStruct((n, w), dt), pl.BlockSpec((tm, w), row))
    outs = [wide(KEY_DIM, BF16), wide(KEY_DIM, BF16), wide(VAL_DIM, BF16), wide(VAL_DIM, BF16),
            wide(CONV_DIM, BF16), wide(CONV_DIM, BF16), wide(2 * KEY_DIM, F32)]
    return pl.pallas_call(
        functools.partial(_inproj_kernel, tiles_a=tiles_a, seg=seg),
        out_shape=[o[0] for o in outs],
        grid=(N_W_BLOCKS + tiles_a + tiles_b,),
        in_specs=[pl.BlockSpec((tm, D_MODEL), lambda i: (jnp.minimum(tile(i), tiles_a - 1), 0)),
                  pl.BlockSpec((tm, D_MODEL), lambda i: (jnp.maximum(tile(i) - tiles_a, 0), 0)),
                  pl.BlockSpec((None, 1, N_MOD * D_MODEL), lambda i: (mod_row(tile(i)), 0, 0)),
                  _const_spec(n1g.shape),
                  pl.BlockSpec((W_BLOCK, D_MODEL), lambda i: (jnp.minimum(i, N_W_BLOCKS - 1), 0)),
                  pl.BlockSpec((MXU_LANES_V7X, D_MODEL),
                               lambda i: (jnp.where(i < N_HEAD_BLOCKS, HEAD_COLS // MXU_LANES_V7X,
                                                    (jnp.minimum(i, N_W_BLOCKS - 1) + 1) * lanes_per_block), 0)),
                  _const_spec(wgf.shape), _const_spec(wgb.shape), _const_spec(bgf.shape), _const_spec(bgb.shape)],
        out_specs=[o[1] for o in outs],
        scratch_shapes=[pltpu.VMEM((N_W_BLOCKS, D_MODEL, W_BLOCK), BF16),
                        pltpu.VMEM((D_MODEL, MXU_LANES_V7X), BF16),
                        pltpu.VMEM((MXU_LANES_V7X, 2 * KEY_DIM), BF16)],
        compiler_params=_params("arbitrary"),
        name="in_proj",
    )(xa, xb, mod3, n1g, w_in_t, w_in_t, wgf, wgb, bgf, bgb)


def _cat(parts, axis=0):
    return parts[0] if len(parts) == 1 else jnp.concatenate(parts, axis=axis)


def _join_prefix(b, seg):
    parts = [b[:seg]]
    for c in range(1, b.shape[0] // seg):
        parts.append(b[c * seg:(c + 1) * seg] + parts[-1][seg - 1:seg, :])
    return _cat(parts)


def _join_suffix(b, seg):
    n = b.shape[0] // seg
    parts = [b[(n - 1) * seg:]]
    for c in range(n - 2, -1, -1):
        parts.insert(0, b[c * seg:(c + 1) * seg] + parts[0][0:1, :])
    return _cat(parts)


def _diag_args(b, off):
    return _cat([b[c:c + DIAG_BLOCK] - b[c + off:c + off + 1, :] for c in range(0, b.shape[0], DIAG_BLOCK)])


def _pair_args(bf, sb, s):
    lhs, rhs = [], []
    for e0 in range(0, bf.shape[0], 2 * s):
        o0 = e0 + s
        rf = bf[o0 - 1:o0, :]
        rb = sb[o0:o0 + 1, :]
        lhs += [sb[e0:o0] - rb, bf[o0:o0 + s] - rf]
        rhs += [rf - bf[e0:o0], rb - sb[o0:o0 + s]]
    return _cat(lhs), _cat(rhs)


def _mixer_kernel(*refs, seq, heads, seg, has_state_in, has_state_out, has_order_token):
    (q_ref, k_ref, v_ref, gf_ref, gb_ref, og_ref, p_ref, cbg_ref, cw_ref, gn_ref, wo_ref, x_ref,
     mod_ref) = refs[:13]
    refs = refs[13:]
    if has_state_in:
        s0f_ref, s0b_ref = refs[:2]
        refs = refs[2:]
    if has_order_token:
        refs = refs[1:]
    x1_ref = refs[0]
    refs = refs[1:]
    if has_state_out:
        sf_ref, sb_ref = refs[:2]
        refs = refs[2:]
    a_scr = refs[0]
    hsteps = N_HEADS // heads
    half = SUPER_BLOCK // 2
    n_seq = x_ref.shape[0] // seq

    rows_v = lax.broadcasted_iota(jnp.int32, (seq, HEAD_V), 0)
    r_loc = lax.broadcasted_iota(jnp.int32, (DIAG_BLOCK, half), 0)
    lane = lax.broadcasted_iota(jnp.int32, (DIAG_BLOCK, half), 1)
    ones = jnp.ones((HEAD_V, half), BF16)
    scale = lambda t, e: (t.astype(F32) * e).astype(BF16)

    mixes = [[] for _ in range(n_seq)]
    for si, h in [(si, h) for h in range(heads) for si in range(n_seq)]:
        rq = slice(si * seq, (si + 1) * seq)
        a_ref = a_scr.at[(si * heads + h) % a_scr.shape[0]]
        ks = slice(h * HEAD_K, (h + 1) * HEAD_K)
        vs = slice(h * HEAD_V, (h + 1) * HEAD_V)
        q = q_ref[rq, ks]
        k = k_ref[rq, ks]
        v = v_ref[rq, vs]
        bf = _join_prefix(gf_ref[rq, ks], seg)
        sb = _join_suffix(gb_ref[rq, ks], seg)

        e0f = jnp.exp2(_diag_args(bf, DIAG_BLOCK // 2))
        e0b = jnp.exp2(_diag_args(sb, DIAG_BLOCK // 2 - 1))
        q0f, k0f = scale(q, e0f), scale(k, 1.0 / e0f)
        q0b, k0b = scale(q, e0b), scale(k, 1.0 / e0b)

        lhs_levels, rhs_levels = [], []
        s = DIAG_BLOCK
        while s < seq:
            la, ra = _pair_args(bf, sb, s)
            lhs_levels.append(scale(q, jnp.exp2(la)))
            rhs_levels.append(scale(k, jnp.exp2(ra)))
            s *= 2

        for base in range(0, seq, SUPER_BLOCK):
            rs = slice(base, base + SUPER_BLOCK)
            m0f = _dot_nt(q0f[rs], k0f[rs])
            m0b = _dot_nt(q0b[rs], k0b[rs])
            m1 = _dot_nt(lhs_levels[0][rs], rhs_levels[0][rs])
            m2 = _dot_nt(lhs_levels[1][rs], rhs_levels[1][rs])
            for bi in range(SUPER_BLOCK // DIAG_BLOCK):
                rr = slice(bi * DIAG_BLOCK, (bi + 1) * DIAG_BLOCK)
                for lt in range(2):
                    cc = slice(lt * half, (lt + 1) * half)
                    if lt == bi // 2:
                        c_loc = lane - DIAG_BLOCK * (bi % 2)
                        in_diag = (lane >= DIAG_BLOCK) if bi % 2 else (lane < DIAG_BLOCK)
                        diag = (jnp.where(c_loc <= r_loc, m0f[rr, cc], 0.0)
                                + jnp.where(c_loc >= r_loc, m0b[rr, cc], 0.0))
                        piece = jnp.where(in_diag, diag, m1[rr, cc])
                    else:
                        piece = m2[rr, cc]
                    a_ref[base + bi * DIAG_BLOCK:base + (bi + 1) * DIAG_BLOCK,
                          base + lt * half:base + (lt + 1) * half] = piece.astype(BF16)
        s = SUPER_BLOCK
        lev = 2
        while s < seq:
            lhs, rhs = lhs_levels[lev], rhs_levels[lev]
            for e0 in range(0, seq, 2 * s):
                ev = slice(e0, e0 + s)
                od = slice(e0 + s, e0 + 2 * s)
                a_ref[od, ev] = _dot_nt(lhs[od], rhs[ev]).astype(BF16)
                a_ref[ev, od] = _dot_nt(lhs[ev], rhs[od]).astype(BF16)
            s *= 2
            lev += 1

        o = _dot(a_ref[...], v)
        if has_state_in:
            qi = jnp.concatenate([scale(q, jnp.exp2(bf)), scale(q, jnp.exp2(sb))], axis=1)
            s0 = jnp.concatenate([s0f_ref[si, h].astype(BF16), s0b_ref[si, h].astype(BF16)], axis=0)
            o = o + _dot(qi, s0)
        if has_state_out:
            sf_ref[si, h] = _dot_tn(scale(k, jnp.exp2(bf[seq - 1:seq, :] - bf)), v)
            sb_ref[si, h] = _dot_tn(scale(k, jnp.exp2(sb[0:1, :] - sb)), v)

        inv = lax.rsqrt(_dot((o * o).astype(BF16), ones) * (1.0 / HEAD_V) + EPS)
        o = jnp.concatenate([o[:, :half] * inv, o[:, half:] * inv], axis=1) * gn_ref[...]
        p = p_ref[rq, vs].astype(F32)
        conv = (jnp.where(rows_v >= 1, pltpu.roll(p, 1, axis=0), 0.0) * cw_ref[0:1, vs] + p * cw_ref[1:2, vs]
                + jnp.where(rows_v < seq - 1, pltpu.roll(p, seq - 1, axis=0), 0.0) * cw_ref[2:3, vs])
        mixes[si].append(og_ref[rq, vs] * o.astype(BF16) + cbg_ref[rq, vs] * conv.astype(BF16))

    contrib = _dot(_cat([_cat(m, axis=1) for m in mixes], axis=0), wo_ref[...])
    ga1 = mod_ref[:, 2 * D_MODEL:3 * D_MODEL]
    if hsteps == 1:
        x1_ref[...] = x_ref[...] + ga1 * contrib
    else:
        hstep = pl.program_id(1)

        @pl.when(hstep == 0)
        def _():
            x1_ref[...] = x_ref[...] + ga1 * contrib

        @pl.when(hstep != 0)
        def _():
            x1_ref[...] += ga1 * contrib


def _mixer(x, mod3, proj, conv_w, gn, wo, states, *, seq, heads, seqs_per_step, seg, first_row, mod_row,
           has_state_out, run_after=None):
    q, k, v, og, p, cbg, g = proj
    n = x.shape[0]
    rows = seqs_per_step * seq
    hsteps = N_HEADS // heads
    has_state_in = states is not None
    assert seq % SUPER_BLOCK == 0 and seq % seg == 0 and first_row % rows == 0 and n % rows == 0
    b0 = first_row // rows
    a_buffers = max(1, min(seqs_per_step * heads, A_SCRATCH_BYTES // (2 * seq * seq)))
    col = lambda b, h: (b0 + b, h)
    st_block = (seqs_per_step, None, heads, HEAD_K, HEAD_V)
    st_map = lambda b, h: (b, 0, h, 0, 0)
    in_specs = [pl.BlockSpec((rows, heads * HEAD_K), col),
                pl.BlockSpec((rows, heads * HEAD_K), col),
                pl.BlockSpec((rows, heads * HEAD_V), col),
                pl.BlockSpec((rows, heads * HEAD_K), col),
                pl.BlockSpec((rows, heads * HEAD_K), lambda b, h: (b0 + b, hsteps + h)),
                pl.BlockSpec((rows, heads * HEAD_V), col),
                pl.BlockSpec((rows, heads * HEAD_V), col),
                pl.BlockSpec((rows, heads * HEAD_V), col),
                pl.BlockSpec((3, heads * HEAD_V), lambda b, h: (0, h)),
                pl.BlockSpec((1, HEAD_V), lambda b, h: (0, 0)),
                (pl.BlockSpec((heads * HEAD_V, D_MODEL), lambda b, h: (h, 0)) if hsteps > 1
                 else _const_spec((heads * HEAD_V, D_MODEL))),
                pl.BlockSpec((rows, D_MODEL), lambda b, h: (b, 0)),
                pl.BlockSpec((None, 1, N_MOD * D_MODEL), lambda b, h: (mod_row(b), 0, 0))]
    args = [q, k, v, g, g, og, p, cbg, conv_w, gn, wo, x, mod3]
    if has_state_in:
        in_specs += [pl.BlockSpec(st_block, st_map)] * 2
        args += list(states)
    if run_after is not None:
        in_specs.append(pl.BlockSpec(memory_space=pl.ANY))
        args.append(run_after)
    out_shape = [jax.ShapeDtypeStruct((n, D_MODEL), F32)]
    out_specs = [pl.BlockSpec((rows, D_MODEL), lambda b, h: (b, 0))]
    if has_state_out:
        out_shape += [jax.ShapeDtypeStruct((n // seq, 1, N_HEADS, HEAD_K, HEAD_V), F32)] * 2
        out_specs += [pl.BlockSpec(st_block, st_map)] * 2
    return pl.pallas_call(
        functools.partial(_mixer_kernel, seq=seq, heads=heads, seg=seg, has_state_in=has_state_in,
                          has_state_out=has_state_out, has_order_token=run_after is not None),
        out_shape=out_shape,
        grid=(n // rows, hsteps),
        in_specs=in_specs,
        out_specs=out_specs,
        scratch_shapes=[pltpu.VMEM((a_buffers, seq, seq), BF16)],
        compiler_params=_params("arbitrary", "arbitrary", vmem=MIXER_VMEM_LIMIT_BYTES),
        name="mixer",
    )(*args)


N_FFN_TILES = D_FF // FFN_TILE


def _ffn_kernel(x1_ref, mod_ref, n2_ref, wu_ref, wg_ref, cw_ref, cb_ref, wd_ref, nf_ref, o_ref, acc_ref,
                *, seq, width):
    cols = lambda f: slice(f * FFN_TILE, (f + 1) * FFN_TILE)
    _ffn_tile(x1_ref, mod_ref, n2_ref, cw_ref, cb_ref, nf_ref, o_ref, acc_ref, seq, width,
              lambda f: wu_ref[:, cols(f)], lambda f: wg_ref[:, cols(f)], lambda f: wd_ref[cols(f), :])


def _ffn_staging_kernel(x1_ref, mod_ref, n2_ref, wu_ref, wg_ref, cw_ref, cb_ref, wd_ref, nf_ref,
                        o_ref, wu_out, wg_out, wd_out, wu_scr, wg_scr, wd_scr, acc_ref, *, seq, width):
    step = pl.program_id(0)

    @pl.when(step < N_FFN_TILES)
    def _():
        for src, scr, out in ((wu_ref, wu_scr, wu_out), (wg_ref, wg_scr, wg_out), (wd_ref, wd_scr, wd_out)):
            w = src[...].astype(BF16)
            scr[step] = w
            out[...] = w

    @pl.when(step >= N_FFN_TILES)
    def _():
        _ffn_tile(x1_ref, mod_ref, n2_ref, cw_ref, cb_ref, nf_ref, o_ref, acc_ref, seq, width,
                  lambda f: wu_scr[f], lambda f: wg_scr[f], lambda f: wd_scr[f])


def _ffn_tile(x1_ref, mod_ref, n2_ref, cw_ref, cb_ref, nf_ref, o_ref, acc_ref, seq, width, wu, wg, wd):
    tm = x1_ref.shape[0]
    mod = mod_ref[...]
    sh2 = mod[:, 3 * D_MODEL:4 * D_MODEL]
    sc2 = mod[:, 4 * D_MODEL:5 * D_MODEL]
    ga2 = mod[:, 5 * D_MODEL:6 * D_MODEL]
    x1 = x1_ref[...]
    xb = (_rms(x1, n2_ref[...]) * (1.0 + sc2) + sh2).astype(BF16)

    rows = lax.broadcasted_iota(jnp.int32, (tm, FFN_TILE), 0)
    col_in_row = rows & (width - 1)
    has_left = col_in_row != 0
    has_right = col_in_row != width - 1
    n_rows = seq // width
    zrow = jnp.zeros((width, FFN_TILE), F32)

    def up_rows(z):
        return jnp.concatenate([zrow, z[:tm - width]], axis=0)

    def down_rows(z):
        return jnp.concatenate([z[width:], zrow], axis=0)

    def up_and_gate(f):
        return _dot(xb, wu(f)), _dot(xb, wg(f))

    ahead = up_and_gate(0)
    for f in range(N_FFN_TILES):
        fs = slice(f * FFN_TILE, (f + 1) * FFN_TILE)
        up, gate = ahead
        if f + 1 < N_FFN_TILES:
            ahead = up_and_gate(f + 1)
        left = jnp.where(has_left, pltpu.roll(up, 1, axis=0), 0.0)
        right = jnp.where(has_right, pltpu.roll(up, tm - 1, axis=0), 0.0)
        w = lambda i, j: cw_ref[3 * i + j:3 * i + j + 1, fs]
        cv = left * w(1, 0) + up * w(1, 1) + right * w(1, 2) + cb_ref[:, fs]
        if n_rows > 1:
            cv = cv + (up_rows(left) * w(0, 0) + up_rows(up) * w(0, 1) + up_rows(right) * w(0, 2)
                       + down_rows(left) * w(2, 0) + down_rows(up) * w(2, 1) + down_rows(right) * w(2, 2))
        act = (cv * _sigmoid(cv) * gate).astype(BF16)
        part = _dot(act, wd(f))
        if f == 0:
            acc_ref[...] = part
        else:
            acc_ref[...] += part

    o_ref[...] = _rms(x1 + ga2 * acc_ref[...], nf_ref[...])


def _ffn_staging(x1, mod3, n2g, wu, wg, cw, cb, wd, nfg, *, tm, seq, width, mod_row):
    assert width & (width - 1) == 0 and tm % seq == 0 and (seq == width or tm == seq)
    tile = lambda i: jnp.maximum(i - N_FFN_TILES, 0)
    w_step = lambda i: jnp.minimum(i, N_FFN_TILES - 1)
    col_blk = pl.BlockSpec((D_MODEL, FFN_TILE), lambda i: (0, w_step(i)))
    row_blk = pl.BlockSpec((FFN_TILE, D_MODEL), lambda i: (w_step(i), 0))
    return pl.pallas_call(
        functools.partial(_ffn_staging_kernel, seq=seq, width=width),
        out_shape=[jax.ShapeDtypeStruct(x1.shape, F32), jax.ShapeDtypeStruct(wu.shape, BF16),
                   jax.ShapeDtypeStruct(wg.shape, BF16), jax.ShapeDtypeStruct(wd.shape, BF16)],
        grid=(N_FFN_TILES + x1.shape[0] // tm,),
        in_specs=[pl.BlockSpec((tm, D_MODEL), lambda i: (tile(i), 0)),
                  pl.BlockSpec((None, 1, N_MOD * D_MODEL), lambda i: (mod_row(tile(i)), 0, 0)),
                  _const_spec(n2g.shape), col_blk, col_blk,
                  _const_spec(cw.shape), _const_spec(cb.shape), row_blk,
                  _const_spec(nfg.shape)],
        out_specs=[pl.BlockSpec((tm, D_MODEL), lambda i: (tile(i), 0)), col_blk, col_blk, row_blk],
        scratch_shapes=[pltpu.VMEM((N_FFN_TILES, D_MODEL, FFN_TILE), BF16),
                        pltpu.VMEM((N_FFN_TILES, D_MODEL, FFN_TILE), BF16),
                        pltpu.VMEM((N_FFN_TILES, FFN_TILE, D_MODEL), BF16),
                        pltpu.VMEM((tm, D_MODEL), F32)],
        compiler_params=_params("arbitrary"),
        name="ffn_staging",
    )(x1, mod3, n2g, wu, wg, cw, cb, wd, nfg)


def _ffn(x1, mod3, n2g, wu, wg, cw, cb, wd, nfg, *, tm, seq, width, mod_row):
    assert width & (width - 1) == 0 and tm % seq == 0 and (seq == width or tm == seq)
    return pl.pallas_call(
        functools.partial(_ffn_kernel, seq=seq, width=width),
        out_shape=jax.ShapeDtypeStruct(x1.shape, F32),
        grid=(x1.shape[0] // tm,),
        in_specs=[pl.BlockSpec((tm, D_MODEL), lambda i: (i, 0)),
                  pl.BlockSpec((None, 1, N_MOD * D_MODEL), lambda i: (mod_row(i), 0, 0)),
                  _const_spec(n2g.shape), _const_spec(wu.shape), _const_spec(wg.shape),
                  _const_spec(cw.shape), _const_spec(cb.shape), _const_spec(wd.shape),
                  _const_spec(nfg.shape)],
        out_specs=pl.BlockSpec((tm, D_MODEL), lambda i: (i, 0)),
        scratch_shapes=[pltpu.VMEM((tm, D_MODEL), F32)],
        compiler_params=_params("arbitrary"),
        name="ffn",
    )(x1, mod3, n2g, wu, wg, cw, cb, wd, nfg)


def kernel(x_prompt, x_sample, c, state_gla_fwd, state_gla_bwd, c_ctx, w_ada, b_ada, norm1_g, w_in, w_gk_f,
           b_gk_f, w_gk_b, b_gk_b, gla_norm_g, conv_mix_w, w_out, norm2_g, ffn_w_up, ffn_w_gate,
           ffn_conv_w, ffn_conv_b, ffn_w_down, normf_g):
    bp, lp, d = x_prompt.shape
    bs, ls, _ = x_sample.shape
    assert w_ada.shape[0] == 1 and d == D_MODEL and bs + 1 <= MOD_ROWS

    mod3 = _ada(c_ctx[None, :], c, w_ada[0], b_ada)

    assert IN_SPLITS == (KEY_DIM, KEY_DIM, VAL_DIM, VAL_DIM, GATE_RANK, GATE_RANK) + (CONV_DIM,) * 5
    wo = w_out[0].astype(BF16)
    cw9 = ffn_conv_w[0].reshape(9, D_FF)
    cb = ffn_conv_b
    n1g, n2g, nfg, gn = norm1_g, norm2_g, normf_g[None, :], gla_norm_g

    xp, xs = x_prompt.reshape(-1, d), x_sample.reshape(-1, d)
    tiles_p = xp.shape[0] // INPROJ_ROWS
    seg = min(lp, ls, INPROJ_ROWS)
    proj = _inproj(xp, xs, mod3, n1g, w_in[0].T, w_gk_f[0], w_gk_b[0], b_gk_f, b_gk_b, tm=INPROJ_ROWS, seg=seg,
                   mod_row=lambda t: jnp.where(t < tiles_p, 0, 1 + ((t - tiles_p) * INPROJ_ROWS) // ls))

    mix = functools.partial(_mixer, mod3=mod3, proj=proj, conv_w=conv_mix_w[0], gn=gn, wo=wo, seg=seg)
    x1p, new_f, new_b = mix(xp, states=None, seq=lp, heads=N_HEADS, seqs_per_step=PROMPT_SEQS_PER_STEP,
                            first_row=0, mod_row=lambda b: 0, has_state_out=True)
    y_prompt, wu, wg, wd = _ffn_staging(x1p, mod3, n2g, ffn_w_up[0], ffn_w_gate[0], cw9, cb, ffn_w_down[0], nfg,
                                        tm=FFN_ROWS, seq=lp, width=lp, mod_row=lambda i: 0)
    x1s, = mix(xs, states=(state_gla_fwd, state_gla_bwd), seq=ls, heads=N_HEADS, seqs_per_step=1,
               first_row=xp.shape[0], mod_row=lambda b: 1 + b, has_state_out=False, run_after=wd)
    y_sample = _ffn(x1s, mod3, n2g, wu, wg, cw9, cb, wd, nfg, tm=FFN_ROWS, seq=ls, width=GRID_W,
                    mod_row=lambda i: 1 + (i * FFN_ROWS) // ls)
    return y_prompt.reshape(x_prompt.shape), y_sample.reshape(x_sample.shape), new_f, new_b
```

```python
import functools

import jax
import jax.numpy as jnp
from jax import lax
from jax.experimental import pallas as pl
from jax.experimental.pallas import tpu as pltpu

D_MODEL = 1024
N_HEADS = 4
HEAD_K = 128
HEAD_V = 256
KEY_DIM = N_HEADS * HEAD_K
VAL_DIM = N_HEADS * HEAD_V
GATE_RANK = 16
GATE_NORMALIZER = 16.0
LOG2_E = 1.4426950408889634
CONV_DIM = D_MODEL
D_FF = 2816
N_MOD = 6
EPS = 1e-6
GRID_W = 64
IN_SPLITS = (KEY_DIM, KEY_DIM, VAL_DIM, VAL_DIM, GATE_RANK, GATE_RANK,
             CONV_DIM, CONV_DIM, CONV_DIM, VAL_DIM, CONV_DIM)

DIAG_BLOCK = 64
SUPER_BLOCK = 256
MXU_LANES_V7X = 128
FFN_TILE = 256
FFN_ROWS = 1024
INPROJ_ROWS = 512
PROMPT_SEQS_PER_STEP = 2
ADA_TILE = 1024
MOD_ROWS = 8
VMEM_LIMIT_BYTES = 56 * 1024 * 1024
A_SCRATCH_BYTES = 4 * 1024 * 1024
MIXER_VMEM_LIMIT_BYTES = 61 * 1024 * 1024

F32 = jnp.float32
BF16 = jnp.bfloat16


def _dot(a, b):
    return jnp.dot(a, b, preferred_element_type=F32)


def _dot_nt(a, b):
    return lax.dot_general(a, b, (((1,), (1,)), ((), ())), preferred_element_type=F32)


def _dot_tn(a, b):
    return lax.dot_general(a, b, (((0,), (0,)), ((), ())), preferred_element_type=F32)


def _sigmoid(x):
    return 1.0 / (1.0 + jnp.exp(-x))


def _rms(x, g):
    return x * lax.rsqrt(jnp.mean(x * x, axis=-1, keepdims=True) + EPS) * g


def _const_spec(shape):
    nd = len(shape)
    return pl.BlockSpec(shape, lambda *_: (0,) * nd, pipeline_mode=pl.Buffered(1))


def _params(*sem, vmem=VMEM_LIMIT_BYTES):
    return pltpu.CompilerParams(dimension_semantics=sem, vmem_limit_bytes=vmem)


def _ada_kernel(cctx_ref, c_ref, w_ref, b_ref, o_ref, rows_ref):
    rows_ref[...] = jnp.zeros_like(rows_ref)
    rows_ref[0:1, :] = cctx_ref[...]
    rows_ref[1:1 + c_ref.shape[0], :] = c_ref[...]
    c = rows_ref[...]
    s = (c * _sigmoid(c)).astype(BF16)
    o_ref[:, 0, :] = _dot(s, w_ref[...].astype(BF16)) + b_ref[...]


def _ada(c_ctx, c, w_ada, b_ada):
    n = w_ada.shape[1]
    assert 1 + c.shape[0] <= MOD_ROWS
    return pl.pallas_call(
        _ada_kernel,
        out_shape=jax.ShapeDtypeStruct((MOD_ROWS, 1, n), F32),
        grid=(n // ADA_TILE,),
        in_specs=[pl.BlockSpec(c_ctx.shape, lambda j: (0, 0)),
                  pl.BlockSpec(c.shape, lambda j: (0, 0)),
                  pl.BlockSpec((D_MODEL, ADA_TILE), lambda j: (0, j)),
                  pl.BlockSpec((1, ADA_TILE), lambda j: (0, j))],
        out_specs=pl.BlockSpec((MOD_ROWS, 1, ADA_TILE), lambda j: (0, 0, j)),
        scratch_shapes=[pltpu.VMEM((MOD_ROWS, D_MODEL), F32)],
        compiler_params=_params("arbitrary"),
        name="ada",
    )(c_ctx, c, w_ada, b_ada)


W_BLOCK = 512
HEAD_COLS = 2 * KEY_DIM + 2 * VAL_DIM
TAIL_OFF = HEAD_COLS + 2 * GATE_RANK
TAIL_SHIFT = TAIL_OFF % MXU_LANES_V7X
N_HEAD_BLOCKS = HEAD_COLS // W_BLOCK
N_W_BLOCKS = N_HEAD_BLOCKS + 5 * CONV_DIM // W_BLOCK


def _inproj_kernel(xa_ref, xb_ref, mod_ref, n1_ref, wa_ref, wb_ref, wgf_ref, wgb_ref, bgf_ref, bgb_ref,
                   q_ref, k_ref, v_ref, og_ref, p_ref, cbg_ref, g_ref, w_scr, wcode_scr, wgk_scr,
                   *, tiles_a, seg):
    step = pl.program_id(0)

    @pl.when(step < N_HEAD_BLOCKS)
    def _():
        w_scr[step] = wa_ref[...].T.astype(BF16)
        wgk_scr[...] = jnp.zeros_like(wgk_scr)
        wgk_scr[0:GATE_RANK, 0:KEY_DIM] = wgf_ref[...].astype(BF16)
        wgk_scr[GATE_RANK:2 * GATE_RANK, KEY_DIM:] = wgb_ref[...].astype(BF16)
        lane = lax.broadcasted_iota(jnp.int32, (D_MODEL, MXU_LANES_V7X), 1)
        wcode_scr[...] = jnp.where(lane < 2 * GATE_RANK, wb_ref[...].T, 0.0).astype(BF16)

    @pl.when(jnp.logical_and(step >= N_HEAD_BLOCKS, step < N_W_BLOCKS))
    def _():
        rows = jnp.concatenate([wa_ref[TAIL_SHIFT:, :], wb_ref[:TAIL_SHIFT, :]], axis=0)
        w_scr[step] = rows.T.astype(BF16)

    @pl.when(step >= N_W_BLOCKS)
    def _():
        first_group = step - N_W_BLOCKS < tiles_a
        mod = mod_ref[...]
        sh1 = mod[:, 0:D_MODEL]
        sc1 = mod[:, D_MODEL:2 * D_MODEL]
        x = jnp.where(first_group, xa_ref[...], xb_ref[...])
        xn = (_rms(x, n1_ref[...]) * (1.0 + sc1) + sh1).astype(BF16)
        halves = range(CONV_DIM // W_BLOCK)
        head = lambda blk: _dot(xn, w_scr[blk])
        tail = lambda grp, j: _dot(xn, w_scr[N_HEAD_BLOCKS + grp * len(halves) + j])
        cols = lambda j: slice(j * W_BLOCK, (j + 1) * W_BLOCK)

        z = (_dot(_dot(xn, wcode_scr[...]).astype(BF16), wgk_scr[...])
             + jnp.concatenate([bgf_ref[...], bgb_ref[...]], axis=1))
        g = (jnp.minimum(z, 0.0) - jnp.log(1.0 + jnp.exp(-jnp.abs(z)))) * (LOG2_E / GATE_NORMALIZER)
        g_hi = g.astype(BF16)
        g_lo = (g - g_hi.astype(F32)).astype(BF16)

        q_ref[...] = (head(0) * (HEAD_K ** -0.5)).astype(BF16)
        k_ref[...] = head(1).astype(BF16)
        for j in halves:
            v_ref[:, cols(j)] = head(2 + j).astype(BF16)

        r = lax.broadcasted_iota(jnp.int32, (seg, seg), 0)
        c = lax.broadcasted_iota(jnp.int32, (seg, seg), 1)
        lower = jnp.where(c <= r, 1.0, 0.0).astype(BF16)
        upper = jnp.where(c >= r, 1.0, 0.0).astype(BF16)
        for s0 in range(0, g.shape[0], seg):
            rows = slice(s0, s0 + seg)
            for tri, cs in ((lower, slice(0, KEY_DIM)), (upper, slice(KEY_DIM, 2 * KEY_DIM))):
                terms = jnp.concatenate([g_hi[rows, cs], g_lo[rows, cs]], axis=0)
                g_ref[rows, cs] = _dot(jnp.concatenate([tri, tri], axis=1), terms)

        for j in halves:
            g_out = head(2 + len(halves) + j)
            og_ref[:, cols(j)] = (g_out * _sigmoid(g_out) * _sigmoid(tail(3, j))).astype(BF16)
            p_ref[:, cols(j)] = (tail(1, j) * tail(2, j)).astype(BF16)
            cbg_ref[:, cols(j)] = (_sigmoid(tail(4, j)) * tail(0, j)).astype(BF16)


def _inproj(xa, xb, mod3, n1g, w_in_t, wgf, wgb, bgf, bgb, *, tm, seg, mod_row):
    tiles_a, tiles_b = xa.shape[0] // tm, xb.shape[0] // tm
    n = xa.shape[0] + xb.shape[0]
    assert seg & (seg - 1) == 0 and tm % seg == 0
    assert KEY_DIM == W_BLOCK and HEAD_COLS % W_BLOCK == 0 and W_BLOCK % MXU_LANES_V7X == 0
    tile = lambda i: jnp.maximum(i - N_W_BLOCKS, 0)
    row = lambda i: (tile(i), 0)
    lanes_per_block = W_BLOCK // MXU_LANES_V7X
    wide = lambda w, dt: (jax.ShapeDtypeStruct((n, w), dt), pl.BlockSpec((tm, w), row))
    outs = [wide(KEY_DIM, BF16), wide(KEY_DIM, BF16), wide(VAL_DIM, BF16), wide(VAL_DIM, BF16),
            wide(CONV_DIM, BF16), wide(CONV_DIM, BF16), wide(2 * KEY_DIM, F32)]
    return pl.pallas_call(
        functools.partial(_inproj_kernel, tiles_a=tiles_a, seg=seg),
        out_shape=[o[0] for o in outs],
        grid=(N_W_BLOCKS + tiles_a + tiles_b,),
        in_specs=[pl.BlockSpec((tm, D_MODEL), lambda i: (jnp.minimum(tile(i), tiles_a - 1), 0)),
                  pl.BlockSpec((tm, D_MODEL), lambda i: (jnp.maximum(tile(i) - tiles_a, 0), 0)),
                  pl.BlockSpec((None, 1, N_MOD * D_MODEL), lambda i: (mod_row(tile(i)), 0, 0)),
                  _const_spec(n1g.shape),
                  pl.BlockSpec((W_BLOCK, D_MODEL), lambda i: (jnp.minimum(i, N_W_BLOCKS - 1), 0)),
                  pl.BlockSpec((MXU_LANES_V7X, D_MODEL),
                               lambda i: (jnp.where(i < N_HEAD_BLOCKS, HEAD_COLS // MXU_LANES_V7X,
                                                    (jnp.minimum(i, N_W_BLOCKS - 1) + 1) * lanes_per_block), 0)),
                  _const_spec(wgf.shape), _const_spec(wgb.shape), _const_spec(bgf.shape), _const_spec(bgb.shape)],
        out_specs=[o[1] for o in outs],
        scratch_shapes=[pltpu.VMEM((N_W_BLOCKS, D_MODEL, W_BLOCK), BF16),
                        pltpu.VMEM((D_MODEL, MXU_LANES_V7X), BF16),
                        pltpu.VMEM((MXU_LANES_V7X, 2 * KEY_DIM), BF16)],
        compiler_params=_params("arbitrary"),
        name="in_proj",
    )(xa, xb, mod3, n1g, w_in_t, w_in_t, wgf, wgb, bgf, bgb)


def _cat(parts, axis=0):
    return parts[0] if len(parts) == 1 else jnp.concatenate(parts, axis=axis)


def _join_prefix(b, seg):
    parts = [b[:seg]]
    for c in range(1, b.shape[0] // seg):
        parts.append(b[c * seg:(c + 1) * seg] + parts[-1][seg - 1:seg, :])
    return _cat(parts)


def _join_suffix(b, seg):
    n = b.shape[0] // seg
    parts = [b[(n - 1) * seg:]]
    for c in range(n - 2, -1, -1):
        parts.insert(0, b[c * seg:(c + 1) * seg] + parts[0][0:1, :])
    return _cat(parts)


def _diag_args(b, off):
    return _cat([b[c:c + DIAG_BLOCK] - b[c + off:c + off + 1, :] for c in range(0, b.shape[0], DIAG_BLOCK)])


def _pair_args(bf, sb, s):
    lhs, rhs = [], []
    for e0 in range(0, bf.shape[0], 2 * s):
        o0 = e0 + s
        rf = bf[o0 - 1:o0, :]
        rb = sb[o0:o0 + 1, :]
        lhs += [sb[e0:o0] - rb, bf[o0:o0 + s] - rf]
        rhs += [rf - bf[e0:o0], rb - sb[o0:o0 + s]]
    return _cat(lhs), _cat(rhs)


def _mixer_kernel(*refs, seq, heads, seg, has_state_in, has_state_out, has_order_token):
    (q_ref, k_ref, v_ref, gf_ref, gb_ref, og_ref, p_ref, cbg_ref, cw_ref, gn_ref, wo_ref, x_ref,
     mod_ref) = refs[:13]
    refs = refs[13:]
    if has_state_in:
        s0f_ref, s0b_ref = refs[:2]
        refs = refs[2:]
    if has_order_token:
        refs = refs[1:]
    x1_ref = refs[0]
    refs = refs[1:]
    if has_state_out:
        sf_ref, sb_ref = refs[:2]
        refs = refs[2:]
    a_scr = refs[0]
    hsteps = N_HEADS // heads
    half = SUPER_BLOCK // 2
    n_seq = x_ref.shape[0] // seq

    rows_v = lax.broadcasted_iota(jnp.int32, (seq, HEAD_V), 0)
    r_loc = lax.broadcasted_iota(jnp.int32, (DIAG_BLOCK, half), 0)
    lane = lax.broadcasted_iota(jnp.int32, (DIAG_BLOCK, half), 1)
    ones = jnp.ones((HEAD_V, half), BF16)
    scale = lambda t, e: (t.astype(F32) * e).astype(BF16)

    mixes = [[] for _ in range(n_seq)]
    for si, h in [(si, h) for h in range(heads) for si in range(n_seq)]:
        rq = slice(si * seq, (si + 1) * seq)
        a_ref = a_scr.at[(si * heads + h) % a_scr.shape[0]]
        ks = slice(h * HEAD_K, (h + 1) * HEAD_K)
        vs = slice(h * HEAD_V, (h + 1) * HEAD_V)
        q = q_ref[rq, ks]
        k = k_ref[rq, ks]
        v = v_ref[rq, vs]
        bf = _join_prefix(gf_ref[rq, ks], seg)
        sb = _join_suffix(gb_ref[rq, ks], seg)

        e0f = jnp.exp2(_diag_args(bf, DIAG_BLOCK // 2))
        e0b = jnp.exp2(_diag_args(sb, DIAG_BLOCK // 2 - 1))
        q0f, k0f = scale(q, e0f), scale(k, 1.0 / e0f)
        q0b, k0b = scale(q, e0b), scale(k, 1.0 / e0b)

        lhs_levels, rhs_levels = [], []
        s = DIAG_BLOCK
        while s < seq:
            la, ra = _pair_args(bf, sb, s)
            lhs_levels.append(scale(q, jnp.exp2(la)))
            rhs_levels.append(scale(k, jnp.exp2(ra)))
            s *= 2

        for base in range(0, seq, SUPER_BLOCK):
            rs = slice(base, base + SUPER_BLOCK)
            m0f = _dot_nt(q0f[rs], k0f[rs])
            m0b = _dot_nt(q0b[rs], k0b[rs])
            m1 = _dot_nt(lhs_levels[0][rs], rhs_levels[0][rs])
            m2 = _dot_nt(lhs_levels[1][rs], rhs_levels[1][rs])
            for bi in range(SUPER_BLOCK // DIAG_BLOCK):
                rr = slice(bi * DIAG_BLOCK, (bi + 1) * DIAG_BLOCK)
                for lt in range(2):
                    cc = slice(lt * half, (lt + 1) * half)
                    if lt == bi // 2:
                        c_loc = lane - DIAG_BLOCK * (bi % 2)
                        in_diag = (lane >= DIAG_BLOCK) if bi % 2 else (lane < DIAG_BLOCK)
                        diag = (jnp.where(c_loc <= r_loc, m0f[rr, cc], 0.0)
                                + jnp.where(c_loc >= r_loc, m0b[rr, cc], 0.0))
                        piece = jnp.where(in_diag, diag, m1[rr, cc])
                    else:
                        piece = m2[rr, cc]
                    a_ref[base + bi * DIAG_BLOCK:base + (bi + 1) * DIAG_BLOCK,
                          base + lt * half:base + (lt + 1) * half] = piece.astype(BF16)
        s = SUPER_BLOCK
        lev = 2
        while s < seq:
            lhs, rhs = lhs_levels[lev], rhs_levels[lev]
            for e0 in range(0, seq, 2 * s):
                ev = slice(e0, e0 + s)
                od = slice(e0 + s, e0 + 2 * s)
                a_ref[od, ev] = _dot_nt(lhs[od], rhs[ev]).astype(BF16)
                a_ref[ev, od] = _dot_nt(lhs[ev], rhs[od]).astype(BF16)
            s *= 2
            lev += 1

        o = _dot(a_ref[...], v)
        if has_state_in:
            qi = jnp.concatenate([scale(q, jnp.exp2(bf)), scale(q, jnp.exp2(sb))], axis=1)
            s0 = jnp.concatenate([s0f_ref[si, h].astype(BF16), s0b_ref[si, h].astype(BF16)], axis=0)
            o = o + _dot(qi, s0)
        if has_state_out:
            sf_ref[si, h] = _dot_tn(scale(k, jnp.exp2(bf[seq - 1:seq, :] - bf)), v)
            sb_ref[si, h] = _dot_tn(scale(k, jnp.exp2(sb[0:1, :] - sb)), v)

        inv = lax.rsqrt(_dot((o * o).astype(BF16), ones) * (1.0 / HEAD_V) + EPS)
        o = jnp.concatenate([o[:, :half] * inv, o[:, half:] * inv], axis=1) * gn_ref[...]
        p = p_ref[rq, vs].astype(F32)
        conv = (jnp.where(rows_v >= 1, pltpu.roll(p, 1, axis=0), 0.0) * cw_ref[0:1, vs] + p * cw_ref[1:2, vs]
                + jnp.where(rows_v < seq - 1, pltpu.roll(p, seq - 1, axis=0), 0.0) * cw_ref[2:3, vs])
        mixes[si].append(og_ref[rq, vs] * o.astype(BF16) + cbg_ref[rq, vs] * conv.astype(BF16))

    contrib = _dot(_cat([_cat(m, axis=1) for m in mixes], axis=0), wo_ref[...])
    ga1 = mod_ref[:, 2 * D_MODEL:3 * D_MODEL]
    if hsteps == 1:
        x1_ref[...] = x_ref[...] + ga1 * contrib
    else:
        hstep = pl.program_id(1)

        @pl.when(hstep == 0)
        def _():
            x1_ref[...] = x_ref[...] + ga1 * contrib

        @pl.when(hstep != 0)
        def _():
            x1_ref[...] += ga1 * contrib


def _mixer(x, mod3, proj, conv_w, gn, wo, states, *, seq, heads, seqs_per_step, seg, first_row, mod_row,
           has_state_out, run_after=None):
    q, k, v, og, p, cbg, g = proj
    n = x.shape[0]
    rows = seqs_per_step * seq
    hsteps = N_HEADS // heads
    has_state_in = states is not None
    assert seq % SUPER_BLOCK == 0 and seq % seg == 0 and first_row % rows == 0 and n % rows == 0
    b0 = first_row // rows
    a_buffers = max(1, min(seqs_per_step * heads, A_SCRATCH_BYTES // (2 * seq * seq)))
    col = lambda b, h: (b0 + b, h)
    st_block = (seqs_per_step, None, heads, HEAD_K, HEAD_V)
    st_map = lambda b, h: (b, 0, h, 0, 0)
    in_specs = [pl.BlockSpec((rows, heads * HEAD_K), col),
                pl.BlockSpec((rows, heads * HEAD_K), col),
                pl.BlockSpec((rows, heads * HEAD_V), col),
                pl.BlockSpec((rows, heads * HEAD_K), col),
                pl.BlockSpec((rows, heads * HEAD_K), lambda b, h: (b0 + b, hsteps + h)),
                pl.BlockSpec((rows, heads * HEAD_V), col),
                pl.BlockSpec((rows, heads * HEAD_V), col),
                pl.BlockSpec((rows, heads * HEAD_V), col),
                pl.BlockSpec((3, heads * HEAD_V), lambda b, h: (0, h)),
                pl.BlockSpec((1, HEAD_V), lambda b, h: (0, 0)),
                (pl.BlockSpec((heads * HEAD_V, D_MODEL), lambda b, h: (h, 0)) if hsteps > 1
                 else _const_spec((heads * HEAD_V, D_MODEL))),
                pl.BlockSpec((rows, D_MODEL), lambda b, h: (b, 0)),
                pl.BlockSpec((None, 1, N_MOD * D_MODEL), lambda b, h: (mod_row(b), 0, 0))]
    args = [q, k, v, g, g, og, p, cbg, conv_w, gn, wo, x, mod3]
    if has_state_in:
        in_specs += [pl.BlockSpec(st_block, st_map)] * 2
        args += list(states)
    if run_after is not None:
        in_specs.append(pl.BlockSpec(memory_space=pl.ANY))
        args.append(run_after)
    out_shape = [jax.ShapeDtypeStruct((n, D_MODEL), F32)]
    out_specs = [pl.BlockSpec((rows, D_MODEL), lambda b, h: (b, 0))]
    if has_state_out:
        out_shape += [jax.ShapeDtypeStruct((n // seq, 1, N_HEADS, HEAD_K, HEAD_V), F32)] * 2
        out_specs += [pl.BlockSpec(st_block, st_map)] * 2
    return pl.pallas_call(
        functools.partial(_mixer_kernel, seq=seq, heads=heads, seg=seg, has_state_in=has_state_in,
                          has_state_out=has_state_out, has_order_token=run_after is not None),
        out_shape=out_shape,
        grid=(n // rows, hsteps),
        in_specs=in_specs,
        out_specs=out_specs,
        scratch_shapes=[pltpu.VMEM((a_buffers, seq, seq), BF16)],
        compiler_params=_params("arbitrary", "arbitrary", vmem=MIXER_VMEM_LIMIT_BYTES),
        name="mixer",
    )(*args)


N_FFN_TILES = D_FF // FFN_TILE


def _ffn_kernel(x1_ref, mod_ref, n2_ref, wu_ref, wg_ref, cw_ref, cb_ref, wd_ref, nf_ref, o_ref, acc_ref,
                *, seq, width):
    cols = lambda f: slice(f * FFN_TILE, (f + 1) * FFN_TILE)
    _ffn_tile(x1_ref, mod_ref, n2_ref, cw_ref, cb_ref, nf_ref, o_ref, acc_ref, seq, width,
              lambda f: wu_ref[:, cols(f)], lambda f: wg_ref[:, cols(f)], lambda f: wd_ref[cols(f), :])


def _ffn_staging_kernel(x1_ref, mod_ref, n2_ref, wu_ref, wg_ref, cw_ref, cb_ref, wd_ref, nf_ref,
                        o_ref, wu_out, wg_out, wd_out, wu_scr, wg_scr, wd_scr, acc_ref, *, seq, width):
    step = pl.program_id(0)

    @pl.when(step < N_FFN_TILES)
    def _():
        for src, scr, out in ((wu_ref, wu_scr, wu_out), (wg_ref, wg_scr, wg_out), (wd_ref, wd_scr, wd_out)):
            w = src[...].astype(BF16)
            scr[step] = w
            out[...] = w

    @pl.when(step >= N_FFN_TILES)
    def _():
        _ffn_tile(x1_ref, mod_ref, n2_ref, cw_ref, cb_ref, nf_ref, o_ref, acc_ref, seq, width,
                  lambda f: wu_scr[f], lambda f: wg_scr[f], lambda f: wd_scr[f])


def _ffn_tile(x1_ref, mod_ref, n2_ref, cw_ref, cb_ref, nf_ref, o_ref, acc_ref, seq, width, wu, wg, wd):
    tm = x1_ref.shape[0]
    mod = mod_ref[...]
    sh2 = mod[:, 3 * D_MODEL:4 * D_MODEL]
    sc2 = mod[:, 4 * D_MODEL:5 * D_MODEL]
    ga2 = mod[:, 5 * D_MODEL:6 * D_MODEL]
    x1 = x1_ref[...]
    xb = (_rms(x1, n2_ref[...]) * (1.0 + sc2) + sh2).astype(BF16)

    rows = lax.broadcasted_iota(jnp.int32, (tm, FFN_TILE), 0)
    col_in_row = rows & (width - 1)
    has_left = col_in_row != 0
    has_right = col_in_row != width - 1
    n_rows = seq // width
    zrow = jnp.zeros((width, FFN_TILE), F32)

    def up_rows(z):
        return jnp.concatenate([zrow, z[:tm - width]], axis=0)

    def down_rows(z):
        return jnp.concatenate([z[width:], zrow], axis=0)

    def up_and_gate(f):
        return _dot(xb, wu(f)), _dot(xb, wg(f))

    ahead = up_and_gate(0)
    for f in range(N_FFN_TILES):
        fs = slice(f * FFN_TILE, (f + 1) * FFN_TILE)
        up, gate = ahead
        if f + 1 < N_FFN_TILES:
            ahead = up_and_gate(f + 1)
        left = jnp.where(has_left, pltpu.roll(up, 1, axis=0), 0.0)
        right = jnp.where(has_right, pltpu.roll(up, tm - 1, axis=0), 0.0)
        w = lambda i, j: cw_ref[3 * i + j:3 * i + j + 1, fs]
        taps = lambda i: left * w(i, 0) + up * w(i, 1) + right * w(i, 2)
        cv = taps(1) + cb_ref[:, fs]
        if n_rows > 1:
            cv = cv + up_rows(taps(0)) + down_rows(taps(2))
        act = (cv * _sigmoid(cv) * gate).astype(BF16)
        part = _dot(act, wd(f))
        if f == 0:
            acc_ref[...] = part
        else:
            acc_ref[...] += part

    o_ref[...] = _rms(x1 + ga2 * acc_ref[...], nf_ref[...])


def _ffn_staging(x1, mod3, n2g, wu, wg, cw, cb, wd, nfg, *, tm, seq, width, mod_row):
    assert width & (width - 1) == 0 and tm % seq == 0 and (seq == width or tm == seq)
    tile = lambda i: jnp.maximum(i - N_FFN_TILES, 0)
    w_step = lambda i: jnp.minimum(i, N_FFN_TILES - 1)
    col_blk = pl.BlockSpec((D_MODEL, FFN_TILE), lambda i: (0, w_step(i)))
    row_blk = pl.BlockSpec((FFN_TILE, D_MODEL), lambda i: (w_step(i), 0))
    return pl.pallas_call(
        functools.partial(_ffn_staging_kernel, seq=seq, width=width),
        out_shape=[jax.ShapeDtypeStruct(x1.shape, F32), jax.ShapeDtypeStruct(wu.shape, BF16),
                   jax.ShapeDtypeStruct(wg.shape, BF16), jax.ShapeDtypeStruct(wd.shape, BF16)],
        grid=(N_FFN_TILES + x1.shape[0] // tm,),
        in_specs=[pl.BlockSpec((tm, D_MODEL), lambda i: (tile(i), 0)),
                  pl.BlockSpec((None, 1, N_MOD * D_MODEL), lambda i: (mod_row(tile(i)), 0, 0)),
                  _const_spec(n2g.shape), col_blk, col_blk,
                  _const_spec(cw.shape), _const_spec(cb.shape), row_blk,
                  _const_spec(nfg.shape)],
        out_specs=[pl.BlockSpec((tm, D_MODEL), lambda i: (tile(i), 0)), col_blk, col_blk, row_blk],
        scratch_shapes=[pltpu.VMEM((N_FFN_TILES, D_MODEL, FFN_TILE), BF16),
                        pltpu.VMEM((N_FFN_TILES, D_MODEL, FFN_TILE), BF16),
                        pltpu.VMEM((N_FFN_TILES, FFN_TILE, D_MODEL), BF16),
                        pltpu.VMEM((tm, D_MODEL), F32)],
        compiler_params=_params("arbitrary"),
        name="ffn_staging",
    )(x1, mod3, n2g, wu, wg, cw, cb, wd, nfg)


def _ffn(x1, mod3, n2g, wu, wg, cw, cb, wd, nfg, *, tm, seq, width, mod_row):
    assert width & (width - 1) == 0 and tm % seq == 0 and (seq == width or tm == seq)
    return pl.pallas_call(
        functools.partial(_ffn_kernel, seq=seq, width=width),
        out_shape=jax.ShapeDtypeStruct(x1.shape, F32),
        grid=(x1.shape[0] // tm,),
        in_specs=[pl.BlockSpec((tm, D_MODEL), lambda i: (i, 0)),
                  pl.BlockSpec((None, 1, N_MOD * D_MODEL), lambda i: (mod_row(i), 0, 0)),
                  _const_spec(n2g.shape), _const_spec(wu.shape), _const_spec(wg.shape),
                  _const_spec(cw.shape), _const_spec(cb.shape), _const_spec(wd.shape),
                  _const_spec(nfg.shape)],
        out_specs=pl.BlockSpec((tm, D_MODEL), lambda i: (i, 0)),
        scratch_shapes=[pltpu.VMEM((tm, D_MODEL), F32)],
        compiler_params=_params("arbitrary"),
        name="ffn",
    )(x1, mod3, n2g, wu, wg, cw, cb, wd, nfg)


def kernel(x_prompt, x_sample, c, state_gla_fwd, state_gla_bwd, c_ctx, w_ada, b_ada, norm1_g, w_in, w_gk_f,
           b_gk_f, w_gk_b, b_gk_b, gla_norm_g, conv_mix_w, w_out, norm2_g, ffn_w_up, ffn_w_gate,
           ffn_conv_w, ffn_conv_b, ffn_w_down, normf_g):
    bp, lp, d = x_prompt.shape
    bs, ls, _ = x_sample.shape
    assert w_ada.shape[0] == 1 and d == D_MODEL and bs + 1 <= MOD_ROWS

    mod3 = _ada(c_ctx[None, :], c, w_ada[0], b_ada)

    assert IN_SPLITS == (KEY_DIM, KEY_DIM, VAL_DIM, VAL_DIM, GATE_RANK, GATE_RANK) + (CONV_DIM,) * 5
    wo = w_out[0].astype(BF16)
    cw9 = ffn_conv_w[0].reshape(9, D_FF)
    cb = ffn_conv_b
    n1g, n2g, nfg, gn = norm1_g, norm2_g, normf_g[None, :], gla_norm_g

    xp, xs = x_prompt.reshape(-1, d), x_sample.reshape(-1, d)
    tiles_p = xp.shape[0] // INPROJ_ROWS
    seg = min(lp, ls, INPROJ_ROWS)
    proj = _inproj(xp, xs, mod3, n1g, w_in[0].T, w_gk_f[0], w_gk_b[0], b_gk_f, b_gk_b, tm=INPROJ_ROWS, seg=seg,
                   mod_row=lambda t: jnp.where(t < tiles_p, 0, 1 + ((t - tiles_p) * INPROJ_ROWS) // ls))

    mix = functools.partial(_mixer, mod3=mod3, proj=proj, conv_w=conv_mix_w[0], gn=gn, wo=wo, seg=seg)
    x1p, new_f, new_b = mix(xp, states=None, seq=lp, heads=N_HEADS, seqs_per_step=PROMPT_SEQS_PER_STEP,
                            first_row=0, mod_row=lambda b: 0, has_state_out=True)
    y_prompt, wu, wg, wd = _ffn_staging(x1p, mod3, n2g, ffn_w_up[0], ffn_w_gate[0], cw9, cb, ffn_w_down[0], nfg,
                                        tm=FFN_ROWS, seq=lp, width=lp, mod_row=lambda i: 0)
    x1s, = mix(xs, states=(state_gla_fwd, state_gla_bwd), seq=ls, heads=N_HEADS, seqs_per_step=1,
               first_row=xp.shape[0], mod_row=lambda b: 1 + b, has_state_out=False, run_after=wd)
    y_sample = _ffn(x1s, mod3, n2g, wu, wg, cw9, cb, wd, nfg, tm=FFN_ROWS, seq=ls, width=GRID_W,
                    mod_row=lambda i: 1 + (i * FFN_ROWS) // ls)
    return y_prompt.reshape(x_prompt.shape), y_sample.reshape(x_sample.shape), new_f, new_b
```

```python
import functools

import jax
import jax.numpy as jnp
from jax import lax
from jax.experimental import pallas as pl
from jax.experimental.pallas import tpu as pltpu

D_MODEL = 1024
N_HEADS = 4
HEAD_K = 128
HEAD_V = 256
KEY_DIM = N_HEADS * HEAD_K
VAL_DIM = N_HEADS * HEAD_V
GATE_RANK = 16
GATE_NORMALIZER = 16.0
LOG2_E = 1.4426950408889634
CONV_DIM = D_MODEL
D_FF = 2816
N_MOD = 6
EPS = 1e-6
GRID_W = 64
IN_SPLITS = (KEY_DIM, KEY_DIM, VAL_DIM, VAL_DIM, GATE_RANK, GATE_RANK,
             CONV_DIM, CONV_DIM, CONV_DIM, VAL_DIM, CONV_DIM)

DIAG_BLOCK = 64
SUPER_BLOCK = 256
MXU_LANES_V7X = 128
FFN_TILE = 256
FFN_ROWS = 1024
INPROJ_ROWS = 512
PROMPT_SEQS_PER_STEP = 2
ADA_TILE = 1024
MOD_ROWS = 8
VMEM_LIMIT_BYTES = 56 * 1024 * 1024
A_SCRATCH_BYTES = 4 * 1024 * 1024
MIXER_VMEM_LIMIT_BYTES = 61 * 1024 * 1024

F32 = jnp.float32
BF16 = jnp.bfloat16


def _dot(a, b):
    return jnp.dot(a, b, preferred_element_type=F32)


def _dot_nt(a, b):
    return lax.dot_general(a, b, (((1,), (1,)), ((), ())), preferred_element_type=F32)


def _dot_tn(a, b):
    return lax.dot_general(a, b, (((0,), (0,)), ((), ())), preferred_element_type=F32)


def _sigmoid(x):
    return 1.0 / (1.0 + jnp.exp(-x))


def _rms(x, g):
    return x * lax.rsqrt(jnp.mean(x * x, axis=-1, keepdims=True) + EPS) * g


def _const_spec(shape):
    nd = len(shape)
    return pl.BlockSpec(shape, lambda *_: (0,) * nd, pipeline_mode=pl.Buffered(1))


def _params(*sem, vmem=VMEM_LIMIT_BYTES):
    return pltpu.CompilerParams(dimension_semantics=sem, vmem_limit_bytes=vmem)


def _ada_kernel(cctx_ref, c_ref, w_ref, b_ref, o_ref, rows_ref):
    rows_ref[...] = jnp.zeros_like(rows_ref)
    rows_ref[0:1, :] = cctx_ref[...]
    rows_ref[1:1 + c_ref.shape[0], :] = c_ref[...]
    c = rows_ref[...]
    s = (c * _sigmoid(c)).astype(BF16)
    o_ref[...] = _dot(s, w_ref[...].astype(BF16)) + b_ref[...]


def _ada(c_ctx, c, w_ada, b_ada):
    n = w_ada.shape[1]
    assert 1 + c.shape[0] <= MOD_ROWS
    return pl.pallas_call(
        _ada_kernel,
        out_shape=jax.ShapeDtypeStruct((MOD_ROWS, n), F32),
        grid=(n // ADA_TILE,),
        in_specs=[pl.BlockSpec(c_ctx.shape, lambda j: (0, 0)),
                  pl.BlockSpec(c.shape, lambda j: (0, 0)),
                  pl.BlockSpec((D_MODEL, ADA_TILE), lambda j: (0, j)),
                  pl.BlockSpec((1, ADA_TILE), lambda j: (0, j))],
        out_specs=pl.BlockSpec((MOD_ROWS, ADA_TILE), lambda j: (0, j)),
        scratch_shapes=[pltpu.VMEM((MOD_ROWS, D_MODEL), F32)],
        compiler_params=_params("arbitrary"),
        name="ada",
    )(c_ctx, c, w_ada, b_ada)


W_BLOCK = 512
HEAD_COLS = 2 * KEY_DIM + 2 * VAL_DIM
TAIL_OFF = HEAD_COLS + 2 * GATE_RANK
TAIL_SHIFT = TAIL_OFF % MXU_LANES_V7X
N_HEAD_BLOCKS = HEAD_COLS // W_BLOCK
N_W_BLOCKS = N_HEAD_BLOCKS + 5 * CONV_DIM // W_BLOCK


def _inproj_kernel(xa_ref, xb_ref, mod_ref, n1_ref, wa_ref, wb_ref, wgf_ref, wgb_ref, bgf_ref, bgb_ref,
                   q_ref, k_ref, v_ref, og_ref, p_ref, cbg_ref, g_ref, w_scr, wcode_scr, wgk_scr,
                   *, tiles_a, seg):
    step = pl.program_id(0)

    @pl.when(step < N_HEAD_BLOCKS)
    def _():
        w_scr[step] = wa_ref[...].T.astype(BF16)
        wgk_scr[...] = jnp.zeros_like(wgk_scr)
        wgk_scr[0:GATE_RANK, 0:KEY_DIM] = wgf_ref[...].astype(BF16)
        wgk_scr[GATE_RANK:2 * GATE_RANK, KEY_DIM:] = wgb_ref[...].astype(BF16)
        lane = lax.broadcasted_iota(jnp.int32, (D_MODEL, MXU_LANES_V7X), 1)
        wcode_scr[...] = jnp.where(lane < 2 * GATE_RANK, wb_ref[...].T, 0.0).astype(BF16)

    @pl.when(jnp.logical_and(step >= N_HEAD_BLOCKS, step < N_W_BLOCKS))
    def _():
        rows = jnp.concatenate([wa_ref[TAIL_SHIFT:, :], wb_ref[:TAIL_SHIFT, :]], axis=0)
        w_scr[step] = rows.T.astype(BF16)

    @pl.when(step >= N_W_BLOCKS)
    def _():
        first_group = step - N_W_BLOCKS < tiles_a
        mod = mod_ref[...]
        sh1 = mod[:, 0:D_MODEL]
        sc1 = mod[:, D_MODEL:2 * D_MODEL]
        x = jnp.where(first_group, xa_ref[...], xb_ref[...])
        xn = (_rms(x, n1_ref[...]) * (1.0 + sc1) + sh1).astype(BF16)
        halves = range(CONV_DIM // W_BLOCK)
        head = lambda blk: _dot(xn, w_scr[blk])
        tail = lambda grp, j: _dot(xn, w_scr[N_HEAD_BLOCKS + grp * len(halves) + j])
        cols = lambda j: slice(j * W_BLOCK, (j + 1) * W_BLOCK)

        z = (_dot(_dot(xn, wcode_scr[...]).astype(BF16), wgk_scr[...])
             + jnp.concatenate([bgf_ref[...], bgb_ref[...]], axis=1))
        g = (jnp.minimum(z, 0.0) - jnp.log(1.0 + jnp.exp(-jnp.abs(z)))) * (LOG2_E / GATE_NORMALIZER)
        g_hi = g.astype(BF16)
        g_lo = (g - g_hi.astype(F32)).astype(BF16)

        q_ref[...] = (head(0) * (HEAD_K ** -0.5)).astype(BF16)
        k_ref[...] = head(1).astype(BF16)
        for j in halves:
            v_ref[:, cols(j)] = head(2 + j).astype(BF16)

        r = lax.broadcasted_iota(jnp.int32, (seg, seg), 0)
        c = lax.broadcasted_iota(jnp.int32, (seg, seg), 1)
        lower = jnp.where(c <= r, 1.0, 0.0).astype(BF16)
        upper = jnp.where(c >= r, 1.0, 0.0).astype(BF16)
        for s0 in range(0, g.shape[0], seg):
            rows = slice(s0, s0 + seg)
            for tri, cs in ((lower, slice(0, KEY_DIM)), (upper, slice(KEY_DIM, 2 * KEY_DIM))):
                terms = jnp.concatenate([g_hi[rows, cs], g_lo[rows, cs]], axis=0)
                g_ref[rows, cs] = _dot(jnp.concatenate([tri, tri], axis=1), terms)

        for j in halves:
            g_out = head(2 + len(halves) + j)
            og_ref[:, cols(j)] = (g_out * _sigmoid(g_out) * _sigmoid(tail(3, j))).astype(BF16)
            p_ref[:, cols(j)] = (tail(1, j) * tail(2, j)).astype(BF16)
            cbg_ref[:, cols(j)] = (_sigmoid(tail(4, j)) * tail(0, j)).astype(BF16)


def _inproj(xa, xb, mod3, n1g, w_in_t, wgf, wgb, bgf, bgb, *, tm, seg, mod_row):
    tiles_a, tiles_b = xa.shape[0] // tm, xb.shape[0] // tm
    n = xa.shape[0] + xb.shape[0]
    assert seg & (seg - 1) == 0 and tm % seg == 0
    assert KEY_DIM == W_BLOCK and HEAD_COLS % W_BLOCK == 0 and W_BLOCK % MXU_LANES_V7X == 0
    tile = lambda i: jnp.maximum(i - N_W_BLOCKS, 0)
    row = lambda i: (tile(i), 0)
    lanes_per_block = W_BLOCK // MXU_LANES_V7X
    wide = lambda w, dt: (jax.ShapeDtypeStruct((n, w), dt), pl.BlockSpec((tm, w), row))
    outs = [wide(KEY_DIM, BF16), wide(KEY_DIM, BF16), wide(VAL_DIM, BF16), wide(VAL_DIM, BF16),
            wide(CONV_DIM, BF16), wide(CONV_DIM, BF16), wide(2 * KEY_DIM, F32)]
    return pl.pallas_call(
        functools.partial(_inproj_kernel, tiles_a=tiles_a, seg=seg),
        out_shape=[o[0] for o in outs],
        grid=(N_W_BLOCKS + tiles_a + tiles_b,),
        in_specs=[pl.BlockSpec((tm, D_MODEL), lambda i: (jnp.minimum(tile(i), tiles_a - 1), 0)),
                  pl.BlockSpec((tm, D_MODEL), lambda i: (jnp.maximum(tile(i) - tiles_a, 0), 0)),
                  pl.BlockSpec((None, 1, N_MOD * D_MODEL), lambda i: (mod_row(tile(i)), 0, 0)),
                  _const_spec(n1g.shape),
                  pl.BlockSpec((W_BLOCK, D_MODEL), lambda i: (jnp.minimum(i, N_W_BLOCKS - 1), 0)),
                  pl.BlockSpec((MXU_LANES_V7X, D_MODEL),
                               lambda i: (jnp.where(i < N_HEAD_BLOCKS, HEAD_COLS // MXU_LANES_V7X,
                                                    (jnp.minimum(i, N_W_BLOCKS - 1) + 1) * lanes_per_block), 0)),
                  _const_spec(wgf.shape), _const_spec(wgb.shape), _const_spec(bgf.shape), _const_spec(bgb.shape)],
        out_specs=[o[1] for o in outs],
        scratch_shapes=[pltpu.VMEM((N_W_BLOCKS, D_MODEL, W_BLOCK), BF16),
                        pltpu.VMEM((D_MODEL, MXU_LANES_V7X), BF16),
                        pltpu.VMEM((MXU_LANES_V7X, 2 * KEY_DIM), BF16)],
        compiler_params=_params("arbitrary"),
        name="in_proj",
    )(xa, xb, mod3, n1g, w_in_t, w_in_t, wgf, wgb, bgf, bgb)


def _cat(parts, axis=0):
    return parts[0] if len(parts) == 1 else jnp.concatenate(parts, axis=axis)


def _join_prefix(b, seg):
    parts = [b[:seg]]
    for c in range(1, b.shape[0] // seg):
        parts.append(b[c * seg:(c + 1) * seg] + parts[-1][seg - 1:seg, :])
    return _cat(parts)


def _join_suffix(b, seg):
    n = b.shape[0] // seg
    parts = [b[(n - 1) * seg:]]
    for c in range(n - 2, -1, -1):
        parts.insert(0, b[c * seg:(c + 1) * seg] + parts[0][0:1, :])
    return _cat(parts)


def _diag_args(b, off):
    return _cat([b[c:c + DIAG_BLOCK] - b[c + off:c + off + 1, :] for c in range(0, b.shape[0], DIAG_BLOCK)])


def _pair_args(bf, sb, s):
    lhs, rhs = [], []
    for e0 in range(0, bf.shape[0], 2 * s):
        o0 = e0 + s
        rf = bf[o0 - 1:o0, :]
        rb = sb[o0:o0 + 1, :]
        lhs += [sb[e0:o0] - rb, bf[o0:o0 + s] - rf]
        rhs += [rf - bf[e0:o0], rb - sb[o0:o0 + s]]
    return _cat(lhs), _cat(rhs)


def _mixer_kernel(*refs, seq, heads, seg, has_state_in, has_state_out, has_order_token):
    (q_ref, k_ref, v_ref, gf_ref, gb_ref, og_ref, p_ref, cbg_ref, cw_ref, gn_ref, wo_ref, x_ref,
     mod_ref) = refs[:13]
    refs = refs[13:]
    if has_state_in:
        s0f_ref, s0b_ref = refs[:2]
        refs = refs[2:]
    if has_order_token:
        refs = refs[1:]
    x1_ref = refs[0]
    refs = refs[1:]
    if has_state_out:
        sf_ref, sb_ref = refs[:2]
        refs = refs[2:]
    a_scr = refs[0]
    hsteps = N_HEADS // heads
    half = SUPER_BLOCK // 2
    n_seq = x_ref.shape[0] // seq

    rows_v = lax.broadcasted_iota(jnp.int32, (seq, HEAD_V), 0)
    r_loc = lax.broadcasted_iota(jnp.int32, (DIAG_BLOCK, half), 0)
    lane = lax.broadcasted_iota(jnp.int32, (DIAG_BLOCK, half), 1)
    ones = jnp.ones((HEAD_V, half), BF16)
    scale = lambda t, e: (t.astype(F32) * e).astype(BF16)

    mixes = [[] for _ in range(n_seq)]
    for si, h in [(si, h) for h in range(heads) for si in range(n_seq)]:
        rq = slice(si * seq, (si + 1) * seq)
        a_ref = a_scr.at[(si * heads + h) % a_scr.shape[0]]
        ks = slice(h * HEAD_K, (h + 1) * HEAD_K)
        vs = slice(h * HEAD_V, (h + 1) * HEAD_V)
        q = q_ref[rq, ks]
        k = k_ref[rq, ks]
        v = v_ref[rq, vs]
        bf = _join_prefix(gf_ref[rq, ks], seg)
        sb = _join_suffix(gb_ref[rq, ks], seg)

        e0f = jnp.exp2(_diag_args(bf, DIAG_BLOCK // 2))
        e0b = jnp.exp2(_diag_args(sb, DIAG_BLOCK // 2 - 1))
        q0f, k0f = scale(q, e0f), scale(k, 1.0 / e0f)
        q0b, k0b = scale(q, e0b), scale(k, 1.0 / e0b)

        lhs_levels, rhs_levels = [], []
        s = DIAG_BLOCK
        while s < seq:
            la, ra = _pair_args(bf, sb, s)
            lhs_levels.append(scale(q, jnp.exp2(la)))
            rhs_levels.append(scale(k, jnp.exp2(ra)))
            s *= 2

        for base in range(0, seq, SUPER_BLOCK):
            rs = slice(base, base + SUPER_BLOCK)
            m0f = _dot_nt(q0f[rs], k0f[rs])
            m0b = _dot_nt(q0b[rs], k0b[rs])
            m1 = _dot_nt(lhs_levels[0][rs], rhs_levels[0][rs])
            m2 = _dot_nt(lhs_levels[1][rs], rhs_levels[1][rs])
            for bi in range(SUPER_BLOCK // DIAG_BLOCK):
                rr = slice(bi * DIAG_BLOCK, (bi + 1) * DIAG_BLOCK)
                for lt in range(2):
                    cc = slice(lt * half, (lt + 1) * half)
                    if lt == bi // 2:
                        c_loc = lane - DIAG_BLOCK * (bi % 2)
                        in_diag = (lane >= DIAG_BLOCK) if bi % 2 else (lane < DIAG_BLOCK)
                        diag = (jnp.where(c_loc <= r_loc, m0f[rr, cc], 0.0)
                                + jnp.where(c_loc >= r_loc, m0b[rr, cc], 0.0))
                        piece = jnp.where(in_diag, diag, m1[rr, cc])
                    else:
                        piece = m2[rr, cc]
                    a_ref[base + bi * DIAG_BLOCK:base + (bi + 1) * DIAG_BLOCK,
                          base + lt * half:base + (lt + 1) * half] = piece.astype(BF16)
        s = SUPER_BLOCK
        lev = 2
        while s < seq:
            lhs, rhs = lhs_levels[lev], rhs_levels[lev]
            for e0 in range(0, seq, 2 * s):
                ev = slice(e0, e0 + s)
                od = slice(e0 + s, e0 + 2 * s)
                a_ref[od, ev] = _dot_nt(lhs[od], rhs[ev]).astype(BF16)
                a_ref[ev, od] = _dot_nt(lhs[ev], rhs[od]).astype(BF16)
            s *= 2
            lev += 1

        o = _dot(a_ref[...], v)
        if has_state_in:
            qi = jnp.concatenate([scale(q, jnp.exp2(bf)), scale(q, jnp.exp2(sb))], axis=1)
            s0 = jnp.concatenate([s0f_ref[si, h].astype(BF16), s0b_ref[si, h].astype(BF16)], axis=0)
            o = o + _dot(qi, s0)
        if has_state_out:
            sf_ref[si, h] = _dot_tn(scale(k, jnp.exp2(bf[seq - 1:seq, :] - bf)), v)
            sb_ref[si, h] = _dot_tn(scale(k, jnp.exp2(sb[0:1, :] - sb)), v)

        inv = lax.rsqrt(_dot((o * o).astype(BF16), ones) * (1.0 / HEAD_V) + EPS)
        o = jnp.concatenate([o[:, :half] * inv, o[:, half:] * inv], axis=1) * gn_ref[...]
        p = p_ref[rq, vs].astype(F32)
        conv = (jnp.where(rows_v >= 1, pltpu.roll(p, 1, axis=0), 0.0) * cw_ref[0:1, vs] + p * cw_ref[1:2, vs]
                + jnp.where(rows_v < seq - 1, pltpu.roll(p, seq - 1, axis=0), 0.0) * cw_ref[2:3, vs])
        mixes[si].append(og_ref[rq, vs] * o.astype(BF16) + cbg_ref[rq, vs] * conv.astype(BF16))

    contrib = _dot(_cat([_cat(m, axis=1) for m in mixes], axis=0), wo_ref[...])
    ga1 = mod_ref[:, 2 * D_MODEL:3 * D_MODEL]
    if hsteps == 1:
        x1_ref[...] = x_ref[...] + ga1 * contrib
    else:
        hstep = pl.program_id(1)

        @pl.when(hstep == 0)
        def _():
            x1_ref[...] = x_ref[...] + ga1 * contrib

        @pl.when(hstep != 0)
        def _():
            x1_ref[...] += ga1 * contrib


def _mixer(x, mod3, proj, conv_w, gn, wo, states, *, seq, heads, seqs_per_step, seg, first_row, mod_row,
           has_state_out, run_after=None):
    q, k, v, og, p, cbg, g = proj
    n = x.shape[0]
    rows = seqs_per_step * seq
    hsteps = N_HEADS // heads
    has_state_in = states is not None
    assert seq % SUPER_BLOCK == 0 and seq % seg == 0 and first_row % rows == 0 and n % rows == 0
    b0 = first_row // rows
    a_buffers = max(1, min(seqs_per_step * heads, A_SCRATCH_BYTES // (2 * seq * seq)))
    col = lambda b, h: (b0 + b, h)
    st_block = (seqs_per_step, None, heads, HEAD_K, HEAD_V)
    st_map = lambda b, h: (b, 0, h, 0, 0)
    in_specs = [pl.BlockSpec((rows, heads * HEAD_K), col),
                pl.BlockSpec((rows, heads * HEAD_K), col),
                pl.BlockSpec((rows, heads * HEAD_V), col),
                pl.BlockSpec((rows, heads * HEAD_K), col),
                pl.BlockSpec((rows, heads * HEAD_K), lambda b, h: (b0 + b, hsteps + h)),
                pl.BlockSpec((rows, heads * HEAD_V), col),
                pl.BlockSpec((rows, heads * HEAD_V), col),
                pl.BlockSpec((rows, heads * HEAD_V), col),
                pl.BlockSpec((3, heads * HEAD_V), lambda b, h: (0, h)),
                pl.BlockSpec((1, HEAD_V), lambda b, h: (0, 0)),
                (pl.BlockSpec((heads * HEAD_V, D_MODEL), lambda b, h: (h, 0)) if hsteps > 1
                 else _const_spec((heads * HEAD_V, D_MODEL))),
                pl.BlockSpec((rows, D_MODEL), lambda b, h: (b, 0)),
                pl.BlockSpec((None, 1, N_MOD * D_MODEL), lambda b, h: (mod_row(b), 0, 0))]
    args = [q, k, v, g, g, og, p, cbg, conv_w, gn, wo, x, mod3]
    if has_state_in:
        in_specs += [pl.BlockSpec(st_block, st_map)] * 2
        args += list(states)
    if run_after is not None:
        in_specs.append(pl.BlockSpec(memory_space=pl.ANY))
        args.append(run_after)
    out_shape = [jax.ShapeDtypeStruct((n, D_MODEL), F32)]
    out_specs = [pl.BlockSpec((rows, D_MODEL), lambda b, h: (b, 0))]
    if has_state_out:
        out_shape += [jax.ShapeDtypeStruct((n // seq, 1, N_HEADS, HEAD_K, HEAD_V), F32)] * 2
        out_specs += [pl.BlockSpec(st_block, st_map)] * 2
    return pl.pallas_call(
        functools.partial(_mixer_kernel, seq=seq, heads=heads, seg=seg, has_state_in=has_state_in,
                          has_state_out=has_state_out, has_order_token=run_after is not None),
        out_shape=out_shape,
        grid=(n // rows, hsteps),
        in_specs=in_specs,
        out_specs=out_specs,
        scratch_shapes=[pltpu.VMEM((a_buffers, seq, seq), BF16)],
        compiler_params=_params("arbitrary", "arbitrary", vmem=MIXER_VMEM_LIMIT_BYTES),
        name="mixer",
    )(*args)


N_FFN_TILES = D_FF // FFN_TILE


def _ffn_kernel(x1_ref, mod_ref, n2_ref, wu_ref, wg_ref, cw_ref, cb_ref, wd_ref, nf_ref, o_ref, acc_ref,
                *, seq, width):
    cols = lambda f: slice(f * FFN_TILE, (f + 1) * FFN_TILE)
    _ffn_tile(x1_ref, mod_ref, n2_ref, cw_ref, cb_ref, nf_ref, o_ref, acc_ref, seq, width,
              lambda f: wu_ref[:, cols(f)], lambda f: wg_ref[:, cols(f)], lambda f: wd_ref[cols(f), :])


def _ffn_staging_kernel(x1_ref, mod_ref, n2_ref, wu_ref, wg_ref, cw_ref, cb_ref, wd_ref, nf_ref,
                        o_ref, wu_out, wg_out, wd_out, wu_scr, wg_scr, wd_scr, acc_ref, *, seq, width):
    step = pl.program_id(0)

    @pl.when(step < N_FFN_TILES)
    def _():
        for src, scr, out in ((wu_ref, wu_scr, wu_out), (wg_ref, wg_scr, wg_out), (wd_ref, wd_scr, wd_out)):
            w = src[...].astype(BF16)
            scr[step] = w
            out[...] = w

    @pl.when(step >= N_FFN_TILES)
    def _():
        _ffn_tile(x1_ref, mod_ref, n2_ref, cw_ref, cb_ref, nf_ref, o_ref, acc_ref, seq, width,
                  lambda f: wu_scr[f], lambda f: wg_scr[f], lambda f: wd_scr[f])


def _ffn_tile(x1_ref, mod_ref, n2_ref, cw_ref, cb_ref, nf_ref, o_ref, acc_ref, seq, width, wu, wg, wd):
    tm = x1_ref.shape[0]
    mod = mod_ref[...]
    sh2 = mod[:, 3 * D_MODEL:4 * D_MODEL]
    sc2 = mod[:, 4 * D_MODEL:5 * D_MODEL]
    ga2 = mod[:, 5 * D_MODEL:6 * D_MODEL]
    x1 = x1_ref[...]
    xb = (_rms(x1, n2_ref[...]) * (1.0 + sc2) + sh2).astype(BF16)

    rows = lax.broadcasted_iota(jnp.int32, (tm, FFN_TILE), 0)
    col_in_row = rows & (width - 1)
    has_left = col_in_row != 0
    has_right = col_in_row != width - 1
    n_rows = seq // width
    zrow = jnp.zeros((width, FFN_TILE), F32)

    def up_rows(z):
        return jnp.concatenate([zrow, z[:tm - width]], axis=0)

    def down_rows(z):
        return jnp.concatenate([z[width:], zrow], axis=0)

    def up_and_gate(f):
        return _dot(xb, wu(f)), _dot(xb, wg(f))

    ahead = up_and_gate(0)
    for f in range(N_FFN_TILES):
        fs = slice(f * FFN_TILE, (f + 1) * FFN_TILE)
        up, gate = ahead
        if f + 1 < N_FFN_TILES:
            ahead = up_and_gate(f + 1)
        left = jnp.where(has_left, pltpu.roll(up, 1, axis=0), 0.0)
        right = jnp.where(has_right, pltpu.roll(up, tm - 1, axis=0), 0.0)
        w = lambda i, j: cw_ref[3 * i + j:3 * i + j + 1, fs]
        taps = lambda i: left * w(i, 0) + up * w(i, 1) + right * w(i, 2)
        cv = taps(1) + cb_ref[:, fs]
        if n_rows > 1:
            cv = cv + up_rows(taps(0)) + down_rows(taps(2))
        act = (cv * _sigmoid(cv) * gate).astype(BF16)
        part = _dot(act, wd(f))
        if f == 0:
            acc_ref[...] = part
        else:
            acc_ref[...] += part

    o_ref[...] = _rms(x1 + ga2 * acc_ref[...], nf_ref[...])


def _ffn_staging(x1, mod3, n2g, wu, wg, cw, cb, wd, nfg, *, tm, seq, width, mod_row):
    assert width & (width - 1) == 0 and tm % seq == 0 and (seq == width or tm == seq)
    tile = lambda i: jnp.maximum(i - N_FFN_TILES, 0)
    w_step = lambda i: jnp.minimum(i, N_FFN_TILES - 1)
    col_blk = pl.BlockSpec((D_MODEL, FFN_TILE), lambda i: (0, w_step(i)))
    row_blk = pl.BlockSpec((FFN_TILE, D_MODEL), lambda i: (w_step(i), 0))
    return pl.pallas_call(
        functools.partial(_ffn_staging_kernel, seq=seq, width=width),
        out_shape=[jax.ShapeDtypeStruct(x1.shape, F32), jax.ShapeDtypeStruct(wu.shape, BF16),
                   jax.ShapeDtypeStruct(wg.shape, BF16), jax.ShapeDtypeStruct(wd.shape, BF16)],
        grid=(N_FFN_TILES + x1.shape[0] // tm,),
        in_specs=[pl.BlockSpec((tm, D_MODEL), lambda i: (tile(i), 0)),
                  pl.BlockSpec((None, 1, N_MOD * D_MODEL), lambda i: (mod_row(tile(i)), 0, 0)),
                  _const_spec(n2g.shape), col_blk, col_blk,
                  _const_spec(cw.shape), _const_spec(cb.shape), row_blk,
                  _const_spec(nfg.shape)],
        out_specs=[pl.BlockSpec((tm, D_MODEL), lambda i: (tile(i), 0)), col_blk, col_blk, row_blk],
        scratch_shapes=[pltpu.VMEM((N_FFN_TILES, D_MODEL, FFN_TILE), BF16),
                        pltpu.VMEM((N_FFN_TILES, D_MODEL, FFN_TILE), BF16),
                        pltpu.VMEM((N_FFN_TILES, FFN_TILE, D_MODEL), BF16),
                        pltpu.VMEM((tm, D_MODEL), F32)],
        compiler_params=_params("arbitrary"),
        name="ffn_staging",
    )(x1, mod3, n2g, wu, wg, cw, cb, wd, nfg)


def _ffn(x1, mod3, n2g, wu, wg, cw, cb, wd, nfg, *, tm, seq, width, mod_row):
    assert width & (width - 1) == 0 and tm % seq == 0 and (seq == width or tm == seq)
    return pl.pallas_call(
        functools.partial(_ffn_kernel, seq=seq, width=width),
        out_shape=jax.ShapeDtypeStruct(x1.shape, F32),
        grid=(x1.shape[0] // tm,),
        in_specs=[pl.BlockSpec((tm, D_MODEL), lambda i: (i, 0)),
                  pl.BlockSpec((None, 1, N_MOD * D_MODEL), lambda i: (mod_row(i), 0, 0)),
                  _const_spec(n2g.shape), _const_spec(wu.shape), _const_spec(wg.shape),
                  _const_spec(cw.shape), _const_spec(cb.shape), _const_spec(wd.shape),
                  _const_spec(nfg.shape)],
        out_specs=pl.BlockSpec((tm, D_MODEL), lambda i: (i, 0)),
        scratch_shapes=[pltpu.VMEM((tm, D_MODEL), F32)],
        compiler_params=_params("arbitrary"),
        name="ffn",
    )(x1, mod3, n2g, wu, wg, cw, cb, wd, nfg)


def kernel(x_prompt, x_sample, c, state_gla_fwd, state_gla_bwd, c_ctx, w_ada, b_ada, norm1_g, w_in, w_gk_f,
           b_gk_f, w_gk_b, b_gk_b, gla_norm_g, conv_mix_w, w_out, norm2_g, ffn_w_up, ffn_w_gate,
           ffn_conv_w, ffn_conv_b, ffn_w_down, normf_g):
    bp, lp, d = x_prompt.shape
    bs, ls, _ = x_sample.shape
    assert w_ada.shape[0] == 1 and d == D_MODEL and bs + 1 <= MOD_ROWS

    mod3 = _ada(c_ctx[None, :], c, w_ada[0], b_ada).reshape(MOD_ROWS, 1, N_MOD * d)

    assert IN_SPLITS == (KEY_DIM, KEY_DIM, VAL_DIM, VAL_DIM, GATE_RANK, GATE_RANK) + (CONV_DIM,) * 5
    wo = w_out[0].astype(BF16)
    cw9 = ffn_conv_w[0].reshape(9, D_FF)
    cb = ffn_conv_b
    n1g, n2g, nfg, gn = norm1_g, norm2_g, normf_g[None, :], gla_norm_g

    xp, xs = x_prompt.reshape(-1, d), x_sample.reshape(-1, d)
    tiles_p = xp.shape[0] // INPROJ_ROWS
    seg = min(lp, ls, INPROJ_ROWS)
    proj = _inproj(xp, xs, mod3, n1g, w_in[0].T, w_gk_f[0], w_gk_b[0], b_gk_f, b_gk_b, tm=INPROJ_ROWS, seg=seg,
                   mod_row=lambda t: jnp.where(t < tiles_p, 0, 1 + ((t - tiles_p) * INPROJ_ROWS) // ls))

    mix = functools.partial(_mixer, mod3=mod3, proj=proj, conv_w=conv_mix_w[0], gn=gn, wo=wo, seg=seg)
    x1p, new_f, new_b = mix(xp, states=None, seq=lp, heads=N_HEADS, seqs_per_step=PROMPT_SEQS_PER_STEP,
                            first_row=0, mod_row=lambda b: 0, has_state_out=True)
    y_prompt, wu, wg, wd = _ffn_staging(x1p, mod3, n2g, ffn_w_up[0], ffn_w_gate[0], cw9, cb, ffn_w_down[0], nfg,
                                        tm=FFN_ROWS, seq=lp, width=lp, mod_row=lambda i: 0)
    x1s, = mix(xs, states=(state_gla_fwd, state_gla_bwd), seq=ls, heads=N_HEADS, seqs_per_step=1,
               first_row=xp.shape[0], mod_row=lambda b: 1 + b, has_state_out=False, run_after=wd)
    y_sample = _ffn(x1s, mod3, n2g, wu, wg, cw9, cb, wd, nfg, tm=FFN_ROWS, seq=ls, width=GRID_W,
                    mod_row=lambda i: 1 + (i * FFN_ROWS) // ls)
    return y_prompt.reshape(x_prompt.shape), y_sample.reshape(x_sample.shape), new_f, new_b
```

```python
import functools

import jax
import jax.numpy as jnp
from jax import lax
from jax.experimental import pallas as pl
from jax.experimental.pallas import tpu as pltpu

D_MODEL = 1024
N_HEADS = 4
HEAD_K = 128
HEAD_V = 256
KEY_DIM = N_HEADS * HEAD_K
VAL_DIM = N_HEADS * HEAD_V
GATE_RANK = 16
GATE_NORMALIZER = 16.0
LOG2_E = 1.4426950408889634
CONV_DIM = D_MODEL
D_FF = 2816
N_MOD = 6
EPS = 1e-6
GRID_W = 64
IN_SPLITS = (KEY_DIM, KEY_DIM, VAL_DIM, VAL_DIM, GATE_RANK, GATE_RANK,
             CONV_DIM, CONV_DIM, CONV_DIM, VAL_DIM, CONV_DIM)

DIAG_BLOCK = 64
SUPER_BLOCK = 256
MXU_LANES_V7X = 128
FFN_TILE = 256
FFN_ROWS = 1024
INPROJ_ROWS = 512
PROMPT_SEQS_PER_STEP = 2
ADA_TILE = 1024
MOD_ROWS = 8
VMEM_LIMIT_BYTES = 56 * 1024 * 1024
A_SCRATCH_BYTES = 4 * 1024 * 1024
MIXER_VMEM_LIMIT_BYTES = 61 * 1024 * 1024

F32 = jnp.float32
BF16 = jnp.bfloat16


def _dot(a, b):
    return jnp.dot(a, b, preferred_element_type=F32)


def _dot_nt(a, b):
    return lax.dot_general(a, b, (((1,), (1,)), ((), ())), preferred_element_type=F32)


def _dot_tn(a, b):
    return lax.dot_general(a, b, (((0,), (0,)), ((), ())), preferred_element_type=F32)


def _sigmoid(x):
    return 1.0 / (1.0 + jnp.exp(-x))


def _rms(x, g):
    return x * lax.rsqrt(jnp.mean(x * x, axis=-1, keepdims=True) + EPS) * g


def _const_spec(shape):
    nd = len(shape)
    return pl.BlockSpec(shape, lambda *_: (0,) * nd, pipeline_mode=pl.Buffered(1))


def _params(*sem, vmem=VMEM_LIMIT_BYTES):
    return pltpu.CompilerParams(dimension_semantics=sem, vmem_limit_bytes=vmem)


def _ada_kernel(cctx_ref, c_ref, w_ref, b_ref, o_ref, rows_ref):
    rows_ref[...] = jnp.zeros_like(rows_ref)
    rows_ref[0:1, :] = cctx_ref[...]
    rows_ref[1:1 + c_ref.shape[0], :] = c_ref[...]
    c = rows_ref[...]
    s = (c * _sigmoid(c)).astype(BF16)
    o_ref[:, 0, :] = _dot(s, w_ref[...].astype(BF16)) + b_ref[...]


def _ada(c_ctx, c, w_ada, b_ada):
    n = w_ada.shape[1]
    assert 1 + c.shape[0] <= MOD_ROWS
    return pl.pallas_call(
        _ada_kernel,
        out_shape=jax.ShapeDtypeStruct((MOD_ROWS, 1, n), F32),
        grid=(n // ADA_TILE,),
        in_specs=[pl.BlockSpec(c_ctx.shape, lambda j: (0, 0)),
                  pl.BlockSpec(c.shape, lambda j: (0, 0)),
                  pl.BlockSpec((D_MODEL, ADA_TILE), lambda j: (0, j)),
                  pl.BlockSpec((1, ADA_TILE), lambda j: (0, j))],
        out_specs=pl.BlockSpec((MOD_ROWS, 1, ADA_TILE), lambda j: (0, 0, j)),
        scratch_shapes=[pltpu.VMEM((MOD_ROWS, D_MODEL), F32)],
        compiler_params=_params("arbitrary"),
        name="ada",
    )(c_ctx, c, w_ada, b_ada)


W_BLOCK = 512
HEAD_COLS = 2 * KEY_DIM + 2 * VAL_DIM
TAIL_OFF = HEAD_COLS + 2 * GATE_RANK
TAIL_SHIFT = TAIL_OFF % MXU_LANES_V7X
N_HEAD_BLOCKS = HEAD_COLS // W_BLOCK
N_W_BLOCKS = N_HEAD_BLOCKS + 5 * CONV_DIM // W_BLOCK


def _inproj_kernel(xa_ref, xb_ref, mod_ref, n1_ref, wa_ref, wb_ref, wgf_ref, wgb_ref, bgf_ref, bgb_ref,
                   q_ref, k_ref, v_ref, og_ref, p_ref, cbg_ref, g_ref, w_scr, wcode_scr, wgk_scr,
                   *, tiles_a, seg):
    step = pl.program_id(0)

    @pl.when(step < N_HEAD_BLOCKS)
    def _():
        w_scr[step] = wa_ref[...].T.astype(BF16)
        wgk_scr[...] = jnp.zeros_like(wgk_scr)
        wgk_scr[0:GATE_RANK, 0:KEY_DIM] = wgf_ref[...].astype(BF16)
        wgk_scr[GATE_RANK:2 * GATE_RANK, KEY_DIM:] = wgb_ref[...].astype(BF16)
        lane = lax.broadcasted_iota(jnp.int32, (D_MODEL, MXU_LANES_V7X), 1)
        wcode_scr[...] = jnp.where(lane < 2 * GATE_RANK, wb_ref[...].T, 0.0).astype(BF16)

    @pl.when(jnp.logical_and(step >= N_HEAD_BLOCKS, step < N_W_BLOCKS))
    def _():
        rows = jnp.concatenate([wa_ref[TAIL_SHIFT:, :], wb_ref[:TAIL_SHIFT, :]], axis=0)
        w_scr[step] = rows.T.astype(BF16)

    @pl.when(step >= N_W_BLOCKS)
    def _():
        first_group = step - N_W_BLOCKS < tiles_a
        mod = mod_ref[...]
        sh1 = mod[:, 0:D_MODEL]
        sc1 = mod[:, D_MODEL:2 * D_MODEL]
        x = jnp.where(first_group, xa_ref[...], xb_ref[...])
        xn = (_rms(x, n1_ref[...]) * (1.0 + sc1) + sh1).astype(BF16)
        halves = range(CONV_DIM // W_BLOCK)
        head = lambda blk: _dot(xn, w_scr[blk])
        tail = lambda grp, j: _dot(xn, w_scr[N_HEAD_BLOCKS + grp * len(halves) + j])
        cols = lambda j: slice(j * W_BLOCK, (j + 1) * W_BLOCK)

        z = (_dot(_dot(xn, wcode_scr[...]).astype(BF16), wgk_scr[...])
             + jnp.concatenate([bgf_ref[...], bgb_ref[...]], axis=1))
        g = (jnp.minimum(z, 0.0) - jnp.log(1.0 + jnp.exp(-jnp.abs(z)))) * (LOG2_E / GATE_NORMALIZER)
        g_hi = g.astype(BF16)
        g_lo = (g - g_hi.astype(F32)).astype(BF16)

        q_ref[...] = (head(0) * (HEAD_K ** -0.5)).astype(BF16)
        k_ref[...] = head(1).astype(BF16)
        for j in halves:
            v_ref[:, cols(j)] = head(2 + j).astype(BF16)

        r = lax.broadcasted_iota(jnp.int32, (seg, seg), 0)
        c = lax.broadcasted_iota(jnp.int32, (seg, seg), 1)
        lower = jnp.where(c <= r, 1.0, 0.0).astype(BF16)
        upper = jnp.where(c >= r, 1.0, 0.0).astype(BF16)
        for s0 in range(0, g.shape[0], seg):
            rows = slice(s0, s0 + seg)
            for tri, cs in ((lower, slice(0, KEY_DIM)), (upper, slice(KEY_DIM, 2 * KEY_DIM))):
                terms = jnp.concatenate([g_hi[rows, cs], g_lo[rows, cs]], axis=0)
                g_ref[rows, cs] = _dot(jnp.concatenate([tri, tri], axis=1), terms)

        for j in halves:
            g_out = head(2 + len(halves) + j)
            og_ref[:, cols(j)] = (g_out * _sigmoid(g_out) * _sigmoid(tail(3, j))).astype(BF16)
            p_ref[:, cols(j)] = (tail(1, j) * tail(2, j)).astype(BF16)
            cbg_ref[:, cols(j)] = (_sigmoid(tail(4, j)) * tail(0, j)).astype(BF16)


def _inproj(xa, xb, mod3, n1g, w_in_t, wgf, wgb, bgf, bgb, *, tm, seg, mod_row):
    tiles_a, tiles_b = xa.shape[0] // tm, xb.shape[0] // tm
    n = xa.shape[0] + xb.shape[0]
    assert seg & (seg - 1) == 0 and tm % seg == 0
    assert KEY_DIM == W_BLOCK and HEAD_COLS % W_BLOCK == 0 and W_BLOCK % MXU_LANES_V7X == 0
    tile = lambda i: jnp.maximum(i - N_W_BLOCKS, 0)
    row = lambda i: (tile(i), 0)
    lanes_per_block = W_BLOCK // MXU_LANES_V7X
    wide = lambda w, dt: (jax.ShapeDtypeStruct((n, w), dt), pl.BlockSpec((tm, w), row))
    outs = [wide(KEY_DIM, BF16), wide(KEY_DIM, BF16), wide(VAL_DIM, BF16), wide(VAL_DIM, BF16),
            wide(CONV_DIM, BF16), wide(CONV_DIM, BF16), wide(2 * KEY_DIM, F32)]
    return pl.pallas_call(
        functools.partial(_inproj_kernel, tiles_a=tiles_a, seg=seg),
        out_shape=[o[0] for o in outs],
        grid=(N_W_BLOCKS + tiles_a + tiles_b,),
        in_specs=[pl.BlockSpec((tm, D_MODEL), lambda i: (jnp.minimum(tile(i), tiles_a - 1), 0)),
                  pl.BlockSpec((tm, D_MODEL), lambda i: (jnp.maximum(tile(i) - tiles_a, 0), 0)),
                  pl.BlockSpec((None, 1, N_MOD * D_MODEL), lambda i: (mod_row(tile(i)), 0, 0)),
                  _const_spec(n1g.shape),
                  pl.BlockSpec((W_BLOCK, D_MODEL), lambda i: (jnp.minimum(i, N_W_BLOCKS - 1), 0)),
                  pl.BlockSpec((MXU_LANES_V7X, D_MODEL),
                               lambda i: (jnp.where(i < N_HEAD_BLOCKS, HEAD_COLS // MXU_LANES_V7X,
                                                    (jnp.minimum(i, N_W_BLOCKS - 1) + 1) * lanes_per_block), 0)),
                  _const_spec(wgf.shape), _const_spec(wgb.shape), _const_spec(bgf.shape), _const_spec(bgb.shape)],
        out_specs=[o[1] for o in outs],
        scratch_shapes=[pltpu.VMEM((N_W_BLOCKS, D_MODEL, W_BLOCK), BF16),
                        pltpu.VMEM((D_MODEL, MXU_LANES_V7X), BF16),
                        pltpu.VMEM((MXU_LANES_V7X, 2 * KEY_DIM), BF16)],
        compiler_params=_params("arbitrary"),
        name="in_proj",
    )(xa, xb, mod3, n1g, w_in_t, w_in_t, wgf, wgb, bgf, bgb)


def _cat(parts, axis=0):
    return parts[0] if len(parts) == 1 else jnp.concatenate(parts, axis=axis)


def _join_prefix(b, seg):
    parts = [b[:seg]]
    for c in range(1, b.shape[0] // seg):
        parts.append(b[c * seg:(c + 1) * seg] + parts[-1][seg - 1:seg, :])
    return _cat(parts)


def _join_suffix(b, seg):
    n = b.shape[0] // seg
    parts = [b[(n - 1) * seg:]]
    for c in range(n - 2, -1, -1):
        parts.insert(0, b[c * seg:(c + 1) * seg] + parts[0][0:1, :])
    return _cat(parts)


def _diag_args(b, off):
    return _cat([b[c:c + DIAG_BLOCK] - b[c + off:c + off + 1, :] for c in range(0, b.shape[0], DIAG_BLOCK)])


def _pair_args(bf, sb, s):
    lhs, rhs = [], []
    for e0 in range(0, bf.shape[0], 2 * s):
        o0 = e0 + s
        rf = bf[o0 - 1:o0, :]
        rb = sb[o0:o0 + 1, :]
        lhs += [sb[e0:o0] - rb, bf[o0:o0 + s] - rf]
        rhs += [rf - bf[e0:o0], rb - sb[o0:o0 + s]]
    return _cat(lhs), _cat(rhs)


def _mixer_kernel(*refs, seq, heads, seg, has_state_in, has_state_out, has_order_token):
    (q_ref, k_ref, v_ref, gf_ref, gb_ref, og_ref, p_ref, cbg_ref, cw_ref, gn_ref, wo_ref, x_ref,
     mod_ref) = refs[:13]
    refs = refs[13:]
    if has_state_in:
        s0f_ref, s0b_ref = refs[:2]
        refs = refs[2:]
    if has_order_token:
        refs = refs[1:]
    x1_ref = refs[0]
    refs = refs[1:]
    if has_state_out:
        sf_ref, sb_ref = refs[:2]
        refs = refs[2:]
    a_scr = refs[0]
    hsteps = N_HEADS // heads
    half = SUPER_BLOCK // 2
    n_seq = x_ref.shape[0] // seq

    rows_v = lax.broadcasted_iota(jnp.int32, (seq, HEAD_V), 0)
    r_loc = lax.broadcasted_iota(jnp.int32, (DIAG_BLOCK, half), 0)
    lane = lax.broadcasted_iota(jnp.int32, (DIAG_BLOCK, half), 1)
    ones = jnp.ones((HEAD_V, half), BF16)
    scale = lambda t, e: (t.astype(F32) * e).astype(BF16)

    mixes = [[] for _ in range(n_seq)]
    for si, h in [(si, h) for h in range(heads) for si in range(n_seq)]:
        rq = slice(si * seq, (si + 1) * seq)
        a_ref = a_scr.at[(si * heads + h) % a_scr.shape[0]]
        ks = slice(h * HEAD_K, (h + 1) * HEAD_K)
        vs = slice(h * HEAD_V, (h + 1) * HEAD_V)
        q = q_ref[rq, ks]
        k = k_ref[rq, ks]
        v = v_ref[rq, vs]
        bf = _join_prefix(gf_ref[rq, ks], seg)
        sb = _join_suffix(gb_ref[rq, ks], seg)

        e0f = jnp.exp2(_diag_args(bf, DIAG_BLOCK // 2))
        e0b = jnp.exp2(_diag_args(sb, DIAG_BLOCK // 2 - 1))
        q0f, k0f = scale(q, e0f), scale(k, 1.0 / e0f)
        q0b, k0b = scale(q, e0b), scale(k, 1.0 / e0b)

        lhs_levels, rhs_levels = [], []
        s = DIAG_BLOCK
        while s < seq:
            la, ra = _pair_args(bf, sb, s)
            lhs_levels.append(scale(q, jnp.exp2(la)))
            rhs_levels.append(scale(k, jnp.exp2(ra)))
            s *= 2

        for base in range(0, seq, SUPER_BLOCK):
            rs = slice(base, base + SUPER_BLOCK)
            m0f = _dot_nt(q0f[rs], k0f[rs])
            m0b = _dot_nt(q0b[rs], k0b[rs])
            m1 = _dot_nt(lhs_levels[0][rs], rhs_levels[0][rs])
            m2 = _dot_nt(lhs_levels[1][rs], rhs_levels[1][rs])
            for bi in range(SUPER_BLOCK // DIAG_BLOCK):
                rr = slice(bi * DIAG_BLOCK, (bi + 1) * DIAG_BLOCK)
                for lt in range(2):
                    cc = slice(lt * half, (lt + 1) * half)
                    if lt == bi // 2:
                        c_loc = lane - DIAG_BLOCK * (bi % 2)
                        in_diag = (lane >= DIAG_BLOCK) if bi % 2 else (lane < DIAG_BLOCK)
                        diag = (jnp.where(c_loc <= r_loc, m0f[rr, cc], 0.0)
                                + jnp.where(c_loc >= r_loc, m0b[rr, cc], 0.0))
                        piece = jnp.where(in_diag, diag, m1[rr, cc])
                    else:
                        piece = m2[rr, cc]
                    a_ref[base + bi * DIAG_BLOCK:base + (bi + 1) * DIAG_BLOCK,
                          base + lt * half:base + (lt + 1) * half] = piece.astype(BF16)
        s = SUPER_BLOCK
        lev = 2
        while s < seq:
            lhs, rhs = lhs_levels[lev], rhs_levels[lev]
            for e0 in range(0, seq, 2 * s):
                ev = slice(e0, e0 + s)
                od = slice(e0 + s, e0 + 2 * s)
                a_ref[od, ev] = _dot_nt(lhs[od], rhs[ev]).astype(BF16)
                a_ref[ev, od] = _dot_nt(lhs[ev], rhs[od]).astype(BF16)
            s *= 2
            lev += 1

        o = _dot(a_ref[...], v)
        if has_state_in:
            qi = jnp.concatenate([scale(q, jnp.exp2(bf)), scale(q, jnp.exp2(sb))], axis=1)
            s0 = jnp.concatenate([s0f_ref[si, h].astype(BF16), s0b_ref[si, h].astype(BF16)], axis=0)
            o = o + _dot(qi, s0)
        if has_state_out:
            sf_ref[si, h] = _dot_tn(scale(k, jnp.exp2(bf[seq - 1:seq, :] - bf)), v)
            sb_ref[si, h] = _dot_tn(scale(k, jnp.exp2(sb[0:1, :] - sb)), v)

        inv = lax.rsqrt(_dot((o * o).astype(BF16), ones) * (1.0 / HEAD_V) + EPS)
        o = jnp.concatenate([o[:, :half] * inv, o[:, half:] * inv], axis=1) * gn_ref[...]
        p = p_ref[rq, vs].astype(F32)
        conv = (jnp.where(rows_v >= 1, pltpu.roll(p, 1, axis=0), 0.0) * cw_ref[0:1, vs] + p * cw_ref[1:2, vs]
                + jnp.where(rows_v < seq - 1, pltpu.roll(p, seq - 1, axis=0), 0.0) * cw_ref[2:3, vs])
        mixes[si].append(og_ref[rq, vs] * o.astype(BF16) + cbg_ref[rq, vs] * conv.astype(BF16))

    contrib = _dot(_cat([_cat(m, axis=1) for m in mixes], axis=0), wo_ref[...])
    ga1 = mod_ref[:, 2 * D_MODEL:3 * D_MODEL]
    if hsteps == 1:
        x1_ref[...] = x_ref[...] + ga1 * contrib
    else:
        hstep = pl.program_id(1)

        @pl.when(hstep == 0)
        def _():
            x1_ref[...] = x_ref[...] + ga1 * contrib

        @pl.when(hstep != 0)
        def _():
            x1_ref[...] += ga1 * contrib


def _mixer(x, mod3, proj, conv_w, gn, wo, states, *, seq, heads, seqs_per_step, seg, first_row, mod_row,
           has_state_out, run_after=None):
    q, k, v, og, p, cbg, g = proj
    n = x.shape[0]
    rows = seqs_per_step * seq
    hsteps = N_HEADS // heads
    has_state_in = states is not None
    assert seq % SUPER_BLOCK == 0 and seq % seg == 0 and first_row % rows == 0 and n % rows == 0
    b0 = first_row // rows
    a_buffers = max(1, min(seqs_per_step * heads, A_SCRATCH_BYTES // (2 * seq * seq)))
    col = lambda b, h: (b0 + b, h)
    st_block = (seqs_per_step, None, heads, HEAD_K, HEAD_V)
    st_map = lambda b, h: (b, 0, h, 0, 0)
    in_specs = [pl.BlockSpec((rows, heads * HEAD_K), col),
                pl.BlockSpec((rows, heads * HEAD_K), col),
                pl.BlockSpec((rows, heads * HEAD_V), col),
                pl.BlockSpec((rows, heads * HEAD_K), col),
                pl.BlockSpec((rows, heads * HEAD_K), lambda b, h: (b0 + b, hsteps + h)),
                pl.BlockSpec((rows, heads * HEAD_V), col),
                pl.BlockSpec((rows, heads * HEAD_V), col),
                pl.BlockSpec((rows, heads * HEAD_V), col),
                pl.BlockSpec((3, heads * HEAD_V), lambda b, h: (0, h)),
                pl.BlockSpec((1, HEAD_V), lambda b, h: (0, 0)),
                (pl.BlockSpec((heads * HEAD_V, D_MODEL), lambda b, h: (h, 0)) if hsteps > 1
                 else _const_spec((heads * HEAD_V, D_MODEL))),
                pl.BlockSpec((rows, D_MODEL), lambda b, h: (b, 0)),
                pl.BlockSpec((None, 1, N_MOD * D_MODEL), lambda b, h: (mod_row(b), 0, 0))]
    args = [q, k, v, g, g, og, p, cbg, conv_w, gn, wo, x, mod3]
    if has_state_in:
        in_specs += [pl.BlockSpec(st_block, st_map)] * 2
        args += list(states)
    if run_after is not None:
        in_specs.append(pl.BlockSpec(memory_space=pl.ANY))
        args.append(run_after)
    out_shape = [jax.ShapeDtypeStruct((n, D_MODEL), F32)]
    out_specs = [pl.BlockSpec((rows, D_MODEL), lambda b, h: (b, 0))]
    if has_state_out:
        out_shape += [jax.ShapeDtypeStruct((n // seq, 1, N_HEADS, HEAD_K, HEAD_V), F32)] * 2
        out_specs += [pl.BlockSpec(st_block, st_map)] * 2
    return pl.pallas_call(
        functools.partial(_mixer_kernel, seq=seq, heads=heads, seg=seg, has_state_in=has_state_in,
                          has_state_out=has_state_out, has_order_token=run_after is not None),
        out_shape=out_shape,
        grid=(n // rows, hsteps),
        in_specs=in_specs,
        out_specs=out_specs,
        scratch_shapes=[pltpu.VMEM((a_buffers, seq, seq), BF16)],
        compiler_params=_params("arbitrary", "arbitrary", vmem=MIXER_VMEM_LIMIT_BYTES),
        name="mixer",
    )(*args)


N_FFN_TILES = D_FF // FFN_TILE


def _ffn_kernel(x1_ref, mod_ref, n2_ref, wu_ref, wg_ref, cw_ref, cb_ref, wd_ref, nf_ref, o_ref, acc_ref,
                *, seq, width):
    cols = lambda f: slice(f * FFN_TILE, (f + 1) * FFN_TILE)
    _ffn_tile(x1_ref, mod_ref, n2_ref, cw_ref, cb_ref, nf_ref, o_ref, acc_ref, seq, width,
              lambda f: wu_ref[:, cols(f)], lambda f: wg_ref[:, cols(f)], lambda f: wd_ref[cols(f), :])


FFN_STAGE_STEPS = 16


def _ffn_staging_kernel(x1_ref, mod_ref, n2_ref, wu_ref, wg_ref, cw_ref, cb_ref, wd_ref, nf_ref,
                        o_ref, wu_out, wg_out, wd_out, wu_scr, wg_scr, wd_scr, acc_ref, *, seq, width):
    step = pl.program_id(0)

    @pl.when(step < FFN_STAGE_STEPS)
    def _():
        for src, scr, out in ((wu_ref, wu_scr, wu_out), (wg_ref, wg_scr, wg_out), (wd_ref, wd_scr, wd_out)):
            n = src.shape[0]
            w = src[...].astype(BF16)
            scr[pl.ds(pl.multiple_of(step * n, n), n), :] = w
            out[...] = w

    @pl.when(step >= FFN_STAGE_STEPS)
    def _():
        cols = lambda f: slice(f * FFN_TILE, (f + 1) * FFN_TILE)
        _ffn_tile(x1_ref, mod_ref, n2_ref, cw_ref, cb_ref, nf_ref, o_ref, acc_ref, seq, width,
                  lambda f: wu_scr[:, cols(f)], lambda f: wg_scr[:, cols(f)], lambda f: wd_scr[cols(f), :])


def _ffn_tile(x1_ref, mod_ref, n2_ref, cw_ref, cb_ref, nf_ref, o_ref, acc_ref, seq, width, wu, wg, wd):
    tm = x1_ref.shape[0]
    mod = mod_ref[...]
    sh2 = mod[:, 3 * D_MODEL:4 * D_MODEL]
    sc2 = mod[:, 4 * D_MODEL:5 * D_MODEL]
    ga2 = mod[:, 5 * D_MODEL:6 * D_MODEL]
    x1 = x1_ref[...]
    xb = (_rms(x1, n2_ref[...]) * (1.0 + sc2) + sh2).astype(BF16)

    rows = lax.broadcasted_iota(jnp.int32, (tm, FFN_TILE), 0)
    col_in_row = rows & (width - 1)
    has_left = col_in_row != 0
    has_right = col_in_row != width - 1
    n_rows = seq // width
    zrow = jnp.zeros((width, FFN_TILE), F32)

    def up_rows(z):
        return jnp.concatenate([zrow, z[:tm - width]], axis=0)

    def down_rows(z):
        return jnp.concatenate([z[width:], zrow], axis=0)

    def up_and_gate(f):
        return _dot(xb, wu(f)), _dot(xb, wg(f))

    ahead = up_and_gate(0)
    for f in range(N_FFN_TILES):
        fs = slice(f * FFN_TILE, (f + 1) * FFN_TILE)
        up, gate = ahead
        if f + 1 < N_FFN_TILES:
            ahead = up_and_gate(f + 1)
        left = jnp.where(has_left, pltpu.roll(up, 1, axis=0), 0.0)
        right = jnp.where(has_right, pltpu.roll(up, tm - 1, axis=0), 0.0)
        w = lambda i, j: cw_ref[3 * i + j:3 * i + j + 1, fs]
        taps = lambda i: left * w(i, 0) + up * w(i, 1) + right * w(i, 2)
        cv = taps(1) + cb_ref[:, fs]
        if n_rows > 1:
            cv = cv + up_rows(taps(0)) + down_rows(taps(2))
        act = (cv * _sigmoid(cv) * gate).astype(BF16)
        part = _dot(act, wd(f))
        if f == 0:
            acc_ref[...] = part
        else:
            acc_ref[...] += part

    o_ref[...] = _rms(x1 + ga2 * acc_ref[...], nf_ref[...])


def _ffn_staging(x1, mod3, n2g, wu, wg, cw, cb, wd, nfg, *, tm, seq, width, mod_row):
    assert width & (width - 1) == 0 and tm % seq == 0 and (seq == width or tm == seq)
    bf16_rows = 16
    assert D_MODEL % (FFN_STAGE_STEPS * bf16_rows) == 0 and D_FF % (FFN_STAGE_STEPS * bf16_rows) == 0
    tile = lambda i: jnp.maximum(i - FFN_STAGE_STEPS, 0)
    w_step = lambda i: (jnp.minimum(i, FFN_STAGE_STEPS - 1), 0)
    in_rows = pl.BlockSpec((D_MODEL // FFN_STAGE_STEPS, D_FF), w_step)
    hid_rows = pl.BlockSpec((D_FF // FFN_STAGE_STEPS, D_MODEL), w_step)
    return pl.pallas_call(
        functools.partial(_ffn_staging_kernel, seq=seq, width=width),
        out_shape=[jax.ShapeDtypeStruct(x1.shape, F32), jax.ShapeDtypeStruct(wu.shape, BF16),
                   jax.ShapeDtypeStruct(wg.shape, BF16), jax.ShapeDtypeStruct(wd.shape, BF16)],
        grid=(FFN_STAGE_STEPS + x1.shape[0] // tm,),
        in_specs=[pl.BlockSpec((tm, D_MODEL), lambda i: (tile(i), 0)),
                  pl.BlockSpec((None, 1, N_MOD * D_MODEL), lambda i: (mod_row(tile(i)), 0, 0)),
                  _const_spec(n2g.shape), in_rows, in_rows,
                  _const_spec(cw.shape), _const_spec(cb.shape), hid_rows,
                  _const_spec(nfg.shape)],
        out_specs=[pl.BlockSpec((tm, D_MODEL), lambda i: (tile(i), 0)), in_rows, in_rows, hid_rows],
        scratch_shapes=[pltpu.VMEM((D_MODEL, D_FF), BF16), pltpu.VMEM((D_MODEL, D_FF), BF16),
                        pltpu.VMEM((D_FF, D_MODEL), BF16), pltpu.VMEM((tm, D_MODEL), F32)],
        compiler_params=_params("arbitrary"),
        name="ffn_staging",
    )(x1, mod3, n2g, wu, wg, cw, cb, wd, nfg)


def _ffn(x1, mod3, n2g, wu, wg, cw, cb, wd, nfg, *, tm, seq, width, mod_row):
    assert width & (width - 1) == 0 and tm % seq == 0 and (seq == width or tm == seq)
    return pl.pallas_call(
        functools.partial(_ffn_kernel, seq=seq, width=width),
        out_shape=jax.ShapeDtypeStruct(x1.shape, F32),
        grid=(x1.shape[0] // tm,),
        in_specs=[pl.BlockSpec((tm, D_MODEL), lambda i: (i, 0)),
                  pl.BlockSpec((None, 1, N_MOD * D_MODEL), lambda i: (mod_row(i), 0, 0)),
                  _const_spec(n2g.shape), _const_spec(wu.shape), _const_spec(wg.shape),
                  _const_spec(cw.shape), _const_spec(cb.shape), _const_spec(wd.shape),
                  _const_spec(nfg.shape)],
        out_specs=pl.BlockSpec((tm, D_MODEL), lambda i: (i, 0)),
        scratch_shapes=[pltpu.VMEM((tm, D_MODEL), F32)],
        compiler_params=_params("arbitrary"),
        name="ffn",
    )(x1, mod3, n2g, wu, wg, cw, cb, wd, nfg)


def kernel(x_prompt, x_sample, c, state_gla_fwd, state_gla_bwd, c_ctx, w_ada, b_ada, norm1_g, w_in, w_gk_f,
           b_gk_f, w_gk_b, b_gk_b, gla_norm_g, conv_mix_w, w_out, norm2_g, ffn_w_up, ffn_w_gate,
           ffn_conv_w, ffn_conv_b, ffn_w_down, normf_g):
    bp, lp, d = x_prompt.shape
    bs, ls, _ = x_sample.shape
    assert w_ada.shape[0] == 1 and d == D_MODEL and bs + 1 <= MOD_ROWS

    mod3 = _ada(c_ctx[None, :], c, w_ada[0], b_ada)

    assert IN_SPLITS == (KEY_DIM, KEY_DIM, VAL_DIM, VAL_DIM, GATE_RANK, GATE_RANK) + (CONV_DIM,) * 5
    wo = w_out[0].astype(BF16)
    cw9 = ffn_conv_w[0].reshape(9, D_FF)
    cb = ffn_conv_b
    n1g, n2g, nfg, gn = norm1_g, norm2_g, normf_g[None, :], gla_norm_g

    xp, xs = x_prompt.reshape(-1, d), x_sample.reshape(-1, d)
    tiles_p = xp.shape[0] // INPROJ_ROWS
    seg = min(lp, ls, INPROJ_ROWS)
    proj = _inproj(xp, xs, mod3, n1g, w_in[0].T, w_gk_f[0], w_gk_b[0], b_gk_f, b_gk_b, tm=INPROJ_ROWS, seg=seg,
                   mod_row=lambda t: jnp.where(t < tiles_p, 0, 1 + ((t - tiles_p) * INPROJ_ROWS) // ls))

    mix = functools.partial(_mixer, mod3=mod3, proj=proj, conv_w=conv_mix_w[0], gn=gn, wo=wo, seg=seg)
    x1p, new_f, new_b = mix(xp, states=None, seq=lp, heads=N_HEADS, seqs_per_step=PROMPT_SEQS_PER_STEP,
                            first_row=0, mod_row=lambda b: 0, has_state_out=True)
    y_prompt, wu, wg, wd = _ffn_staging(x1p, mod3, n2g, ffn_w_up[0], ffn_w_gate[0], cw9, cb, ffn_w_down[0], nfg,
                                        tm=FFN_ROWS, seq=lp, width=lp, mod_row=lambda i: 0)
    x1s, = mix(xs, states=(state_gla_fwd, state_gla_bwd), seq=ls, heads=N_HEADS, seqs_per_step=1,
               first_row=xp.shape[0], mod_row=lambda b: 1 + b, has_state_out=False, run_after=wd)
    y_sample = _ffn(x1s, mod3, n2g, wu, wg, cw9, cb, wd, nfg, tm=FFN_ROWS, seq=ls, width=GRID_W,
                    mod_row=lambda i: 1 + (i * FFN_ROWS) // ls)
    return y_prompt.reshape(x_prompt.shape), y_sample.reshape(x_sample.shape), new_f, new_b
```

```python
import functools

import jax
import jax.numpy as jnp
from jax import lax
from jax.experimental import pallas as pl
from jax.experimental.pallas import tpu as pltpu

D_MODEL = 1024
N_HEADS = 4
HEAD_K = 128
HEAD_V = 256
KEY_DIM = N_HEADS * HEAD_K
VAL_DIM = N_HEADS * HEAD_V
GATE_RANK = 16
GATE_NORMALIZER = 16.0
LOG2_E = 1.4426950408889634
CONV_DIM = D_MODEL
D_FF = 2816
N_MOD = 6
EPS = 1e-6
GRID_W = 64
IN_SPLITS = (KEY_DIM, KEY_DIM, VAL_DIM, VAL_DIM, GATE_RANK, GATE_RANK,
             CONV_DIM, CONV_DIM, CONV_DIM, VAL_DIM, CONV_DIM)

DIAG_BLOCK = 64
SUPER_BLOCK = 256
MXU_LANES_V7X = 128
FFN_TILE = 256
FFN_ROWS = 1024
INPROJ_ROWS = 512
PROMPT_SEQS_PER_STEP = 2
ADA_TILE = 1024
MOD_ROWS = 8
VMEM_LIMIT_BYTES = 56 * 1024 * 1024
A_SCRATCH_BYTES = 4 * 1024 * 1024
MIXER_VMEM_LIMIT_BYTES = 61 * 1024 * 1024

F32 = jnp.float32
BF16 = jnp.bfloat16


def _dot(a, b):
    return jnp.dot(a, b, preferred_element_type=F32)


def _dot_nt(a, b):
    return lax.dot_general(a, b, (((1,), (1,)), ((), ())), preferred_element_type=F32)


def _dot_tn(a, b):
    return lax.dot_general(a, b, (((0,), (0,)), ((), ())), preferred_element_type=F32)


def _sigmoid(x):
    return 1.0 / (1.0 + jnp.exp(-x))


def _rms(x, g):
    return x * lax.rsqrt(jnp.mean(x * x, axis=-1, keepdims=True) + EPS) * g


def _const_spec(shape):
    nd = len(shape)
    return pl.BlockSpec(shape, lambda *_: (0,) * nd, pipeline_mode=pl.Buffered(1))


def _params(*sem, vmem=VMEM_LIMIT_BYTES):
    return pltpu.CompilerParams(dimension_semantics=sem, vmem_limit_bytes=vmem)


def _ada_kernel(cctx_ref, c_ref, w_ref, b_ref, o_ref, rows_ref):
    rows_ref[...] = jnp.zeros_like(rows_ref)
    rows_ref[0:1, :] = cctx_ref[...]
    rows_ref[1:1 + c_ref.shape[0], :] = c_ref[...]
    c = rows_ref[...]
    s = (c * _sigmoid(c)).astype(BF16)
    o_ref[:, 0, :] = _dot(s, w_ref[...].astype(BF16)) + b_ref[...]


def _ada(c_ctx, c, w_ada, b_ada):
    n = w_ada.shape[1]
    assert 1 + c.shape[0] <= MOD_ROWS
    return pl.pallas_call(
        _ada_kernel,
        out_shape=jax.ShapeDtypeStruct((MOD_ROWS, 1, n), F32),
        grid=(n // ADA_TILE,),
        in_specs=[pl.BlockSpec(c_ctx.shape, lambda j: (0, 0)),
                  pl.BlockSpec(c.shape, lambda j: (0, 0)),
                  pl.BlockSpec((D_MODEL, ADA_TILE), lambda j: (0, j)),
                  pl.BlockSpec((1, ADA_TILE), lambda j: (0, j))],
        out_specs=pl.BlockSpec((MOD_ROWS, 1, ADA_TILE), lambda j: (0, 0, j)),
        scratch_shapes=[pltpu.VMEM((MOD_ROWS, D_MODEL), F32)],
        compiler_params=_params("arbitrary"),
        name="ada",
    )(c_ctx, c, w_ada, b_ada)


W_BLOCK = 512
HEAD_COLS = 2 * KEY_DIM + 2 * VAL_DIM
TAIL_OFF = HEAD_COLS + 2 * GATE_RANK
TAIL_SHIFT = TAIL_OFF % MXU_LANES_V7X
N_HEAD_BLOCKS = HEAD_COLS // W_BLOCK
N_W_BLOCKS = N_HEAD_BLOCKS + 5 * CONV_DIM // W_BLOCK


def _inproj_kernel(xa_ref, xb_ref, mod_ref, n1_ref, wa_ref, wb_ref, wgf_ref, wgb_ref, bgf_ref, bgb_ref,
                   q_ref, k_ref, v_ref, og_ref, p_ref, cbg_ref, g_ref, w_scr, wcode_scr, wgk_scr,
                   *, tiles_a, seg):
    step = pl.program_id(0)

    @pl.when(step < N_HEAD_BLOCKS)
    def _():
        w_scr[step] = wa_ref[...].T.astype(BF16)
        wgk_scr[...] = jnp.zeros_like(wgk_scr)
        wgk_scr[0:GATE_RANK, 0:KEY_DIM] = wgf_ref[...].astype(BF16)
        wgk_scr[GATE_RANK:2 * GATE_RANK, KEY_DIM:] = wgb_ref[...].astype(BF16)
        lane = lax.broadcasted_iota(jnp.int32, (D_MODEL, MXU_LANES_V7X), 1)
        wcode_scr[...] = jnp.where(lane < 2 * GATE_RANK, wb_ref[...].T, 0.0).astype(BF16)

    @pl.when(jnp.logical_and(step >= N_HEAD_BLOCKS, step < N_W_BLOCKS))
    def _():
        rows = jnp.concatenate([wa_ref[TAIL_SHIFT:, :], wb_ref[:TAIL_SHIFT, :]], axis=0)
        w_scr[step] = rows.T.astype(BF16)

    @pl.when(step >= N_W_BLOCKS)
    def _():
        first_group = step - N_W_BLOCKS < tiles_a
        mod = mod_ref[...]
        sh1 = mod[:, 0:D_MODEL]
        sc1 = mod[:, D_MODEL:2 * D_MODEL]
        x = jnp.where(first_group, xa_ref[...], xb_ref[...])
        xn = (_rms(x, n1_ref[...]) * (1.0 + sc1) + sh1).astype(BF16)
        halves = range(CONV_DIM // W_BLOCK)
        head = lambda blk: _dot(xn, w_scr[blk])
        tail = lambda grp, j: _dot(xn, w_scr[N_HEAD_BLOCKS + grp * len(halves) + j])
        cols = lambda j: slice(j * W_BLOCK, (j + 1) * W_BLOCK)

        z = (_dot(_dot(xn, wcode_scr[...]).astype(BF16), wgk_scr[...])
             + jnp.concatenate([bgf_ref[...], bgb_ref[...]], axis=1))
        g = (jnp.minimum(z, 0.0) - jnp.log(1.0 + jnp.exp(-jnp.abs(z)))) * (LOG2_E / GATE_NORMALIZER)
        g_hi = g.astype(BF16)
        g_lo = (g - g_hi.astype(F32)).astype(BF16)

        q_ref[...] = (head(0) * (HEAD_K ** -0.5)).astype(BF16)
        k_ref[...] = head(1).astype(BF16)
        for j in halves:
            v_ref[:, cols(j)] = head(2 + j).astype(BF16)

        r = lax.broadcasted_iota(jnp.int32, (seg, seg), 0)
        c = lax.broadcasted_iota(jnp.int32, (seg, seg), 1)
        lower = jnp.where(c <= r, 1.0, 0.0).astype(BF16)
        upper = jnp.where(c >= r, 1.0, 0.0).astype(BF16)
        for s0 in range(0, g.shape[0], seg):
            rows = slice(s0, s0 + seg)
            for tri, cs in ((lower, slice(0, KEY_DIM)), (upper, slice(KEY_DIM, 2 * KEY_DIM))):
                terms = jnp.concatenate([g_hi[rows, cs], g_lo[rows, cs]], axis=0)
                g_ref[rows, cs] = _dot(jnp.concatenate([tri, tri], axis=1), terms)

        for j in halves:
            g_out = head(2 + len(halves) + j)
            og_ref[:, cols(j)] = (g_out * _sigmoid(g_out) * _sigmoid(tail(3, j))).astype(BF16)
            p_ref[:, cols(j)] = (tail(1, j) * tail(2, j)).astype(BF16)
            cbg_ref[:, cols(j)] = (_sigmoid(tail(4, j)) * tail(0, j)).astype(BF16)


def _inproj(xa, xb, mod3, n1g, w_in_t, wgf, wgb, bgf, bgb, *, tm, seg, mod_row):
    tiles_a, tiles_b = xa.shape[0] // tm, xb.shape[0] // tm
    n = xa.shape[0] + xb.shape[0]
    assert seg & (seg - 1) == 0 and tm % seg == 0
    assert KEY_DIM == W_BLOCK and HEAD_COLS % W_BLOCK == 0 and W_BLOCK % MXU_LANES_V7X == 0
    tile = lambda i: jnp.maximum(i - N_W_BLOCKS, 0)
    row = lambda i: (tile(i), 0)
    lanes_per_block = W_BLOCK // MXU_LANES_V7X
    wide = lambda w, dt: (jax.ShapeDtypeStruct((n, w), dt), pl.BlockSpec((tm, w), row))
    outs = [wide(KEY_DIM, BF16), wide(KEY_DIM, BF16), wide(VAL_DIM, BF16), wide(VAL_DIM, BF16),
            wide(CONV_DIM, BF16), wide(CONV_DIM, BF16), wide(2 * KEY_DIM, F32)]
    return pl.pallas_call(
        functools.partial(_inproj_kernel, tiles_a=tiles_a, seg=seg),
        out_shape=[o[0] for o in outs],
        grid=(N_W_BLOCKS + tiles_a + tiles_b,),
        in_specs=[pl.BlockSpec((tm, D_MODEL), lambda i: (jnp.minimum(tile(i), tiles_a - 1), 0)),
                  pl.BlockSpec((tm, D_MODEL), lambda i: (jnp.maximum(tile(i) - tiles_a, 0), 0)),
                  pl.BlockSpec((None, 1, N_MOD * D_MODEL), lambda i: (mod_row(tile(i)), 0, 0)),
                  _const_spec(n1g.shape),
                  pl.BlockSpec((W_BLOCK, D_MODEL), lambda i: (jnp.minimum(i, N_W_BLOCKS - 1), 0)),
                  pl.BlockSpec((MXU_LANES_V7X, D_MODEL),
                               lambda i: (jnp.where(i < N_HEAD_BLOCKS, HEAD_COLS // MXU_LANES_V7X,
                                                    (jnp.minimum(i, N_W_BLOCKS - 1) + 1) * lanes_per_block), 0)),
                  _const_spec(wgf.shape), _const_spec(wgb.shape), _const_spec(bgf.shape), _const_spec(bgb.shape)],
        out_specs=[o[1] for o in outs],
        scratch_shapes=[pltpu.VMEM((N_W_BLOCKS, D_MODEL, W_BLOCK), BF16),
                        pltpu.VMEM((D_MODEL, MXU_LANES_V7X), BF16),
                        pltpu.VMEM((MXU_LANES_V7X, 2 * KEY_DIM), BF16)],
        compiler_params=_params("arbitrary"),
        name="in_proj",
    )(xa, xb, mod3, n1g, w_in_t, w_in_t, wgf, wgb, bgf, bgb)


def _cat(parts, axis=0):
    return parts[0] if len(parts) == 1 else jnp.concatenate(parts, axis=axis)


def _join_prefix(b, seg):
    parts = [b[:seg]]
    for c in range(1, b.shape[0] // seg):
        parts.append(b[c * seg:(c + 1) * seg] + parts[-1][seg - 1:seg, :])
    return _cat(parts)


def _join_suffix(b, seg):
    n = b.shape[0] // seg
    parts = [b[(n - 1) * seg:]]
    for c in range(n - 2, -1, -1):
        parts.insert(0, b[c * seg:(c + 1) * seg] + parts[0][0:1, :])
    return _cat(parts)


def _diag_args(b, off):
    return _cat([b[c:c + DIAG_BLOCK] - b[c + off:c + off + 1, :] for c in range(0, b.shape[0], DIAG_BLOCK)])


def _pair_args(bf, sb, s):
    lhs, rhs = [], []
    for e0 in range(0, bf.shape[0], 2 * s):
        o0 = e0 + s
        rf = bf[o0 - 1:o0, :]
        rb = sb[o0:o0 + 1, :]
        lhs += [sb[e0:o0] - rb, bf[o0:o0 + s] - rf]
        rhs += [rf - bf[e0:o0], rb - sb[o0:o0 + s]]
    return _cat(lhs), _cat(rhs)


def _mixer_kernel(*refs, seq, heads, seg, has_state_in, has_state_out, has_order_token):
    (q_ref, k_ref, v_ref, gf_ref, gb_ref, og_ref, p_ref, cbg_ref, cw_ref, gn_ref, wo_ref, x_ref,
     mod_ref) = refs[:13]
    refs = refs[13:]
    if has_state_in:
        s0f_ref, s0b_ref = refs[:2]
        refs = refs[2:]
    if has_order_token:
        refs = refs[1:]
    x1_ref = refs[0]
    refs = refs[1:]
    if has_state_out:
        sf_ref, sb_ref = refs[:2]
        refs = refs[2:]
    a_scr = refs[0]
    hsteps = N_HEADS // heads
    half = SUPER_BLOCK // 2
    n_seq = x_ref.shape[0] // seq

    rows_v = lax.broadcasted_iota(jnp.int32, (seq, HEAD_V), 0)
    r_loc = lax.broadcasted_iota(jnp.int32, (DIAG_BLOCK, half), 0)
    lane = lax.broadcasted_iota(jnp.int32, (DIAG_BLOCK, half), 1)
    ones = jnp.ones((HEAD_V, half), BF16)
    scale = lambda t, e: (t.astype(F32) * e).astype(BF16)

    mixes = [[] for _ in range(n_seq)]
    for si, h in [(si, h) for h in range(heads) for si in range(n_seq)]:
        rq = slice(si * seq, (si + 1) * seq)
        a_ref = a_scr.at[(si * heads + h) % a_scr.shape[0]]
        ks = slice(h * HEAD_K, (h + 1) * HEAD_K)
        vs = slice(h * HEAD_V, (h + 1) * HEAD_V)
        q = q_ref[rq, ks]
        k = k_ref[rq, ks]
        v = v_ref[rq, vs]
        bf = _join_prefix(gf_ref[rq, ks], seg)
        sb = _join_suffix(gb_ref[rq, ks], seg)

        e0f = jnp.exp2(_diag_args(bf, DIAG_BLOCK // 2))
        e0b = jnp.exp2(_diag_args(sb, DIAG_BLOCK // 2 - 1))
        q0f, k0f = scale(q, e0f), scale(k, 1.0 / e0f)
        q0b, k0b = scale(q, e0b), scale(k, 1.0 / e0b)

        lhs_levels, rhs_levels = [], []
        s = DIAG_BLOCK
        while s < seq:
            la, ra = _pair_args(bf, sb, s)
            lhs_levels.append(scale(q, jnp.exp2(la)))
            rhs_levels.append(scale(k, jnp.exp2(ra)))
            s *= 2

        for base in range(0, seq, SUPER_BLOCK):
            rs = slice(base, base + SUPER_BLOCK)
            m0f = _dot_nt(q0f[rs], k0f[rs])
            m0b = _dot_nt(q0b[rs], k0b[rs])
            m1 = _dot_nt(lhs_levels[0][rs], rhs_levels[0][rs])
            m2 = _dot_nt(lhs_levels[1][rs], rhs_levels[1][rs])
            for bi in range(SUPER_BLOCK // DIAG_BLOCK):
                rr = slice(bi * DIAG_BLOCK, (bi + 1) * DIAG_BLOCK)
                for lt in range(2):
                    cc = slice(lt * half, (lt + 1) * half)
                    if lt == bi // 2:
                        c_loc = lane - DIAG_BLOCK * (bi % 2)
                        in_diag = (lane >= DIAG_BLOCK) if bi % 2 else (lane < DIAG_BLOCK)
                        diag = (jnp.where(c_loc <= r_loc, m0f[rr, cc], 0.0)
                                + jnp.where(c_loc >= r_loc, m0b[rr, cc], 0.0))
                        piece = jnp.where(in_diag, diag, m1[rr, cc])
                    else:
                        piece = m2[rr, cc]
                    a_ref[base + bi * DIAG_BLOCK:base + (bi + 1) * DIAG_BLOCK,
                          base + lt * half:base + (lt + 1) * half] = piece.astype(BF16)
        s = SUPER_BLOCK
        lev = 2
        while s < seq:
            lhs, rhs = lhs_levels[lev], rhs_levels[lev]
            for e0 in range(0, seq, 2 * s):
                ev = slice(e0, e0 + s)
                od = slice(e0 + s, e0 + 2 * s)
                a_ref[od, ev] = _dot_nt(lhs[od], rhs[ev]).astype(BF16)
                a_ref[ev, od] = _dot_nt(lhs[ev], rhs[od]).astype(BF16)
            s *= 2
            lev += 1

        o = _dot(a_ref[...], v)
        if has_state_in:
            qi = jnp.concatenate([scale(q, jnp.exp2(bf)), scale(q, jnp.exp2(sb))], axis=1)
            s0 = jnp.concatenate([s0f_ref[si, h].astype(BF16), s0b_ref[si, h].astype(BF16)], axis=0)
            o = o + _dot(qi, s0)
        if has_state_out:
            sf_ref[si, h] = _dot_tn(scale(k, jnp.exp2(bf[seq - 1:seq, :] - bf)), v)
            sb_ref[si, h] = _dot_tn(scale(k, jnp.exp2(sb[0:1, :] - sb)), v)

        inv = lax.rsqrt(_dot((o * o).astype(BF16), ones) * (1.0 / HEAD_V) + EPS)
        o = jnp.concatenate([o[:, :half] * inv, o[:, half:] * inv], axis=1) * gn_ref[...]
        p = p_ref[rq, vs].astype(F32)
        conv = (jnp.where(rows_v >= 1, pltpu.roll(p, 1, axis=0), 0.0) * cw_ref[0:1, vs] + p * cw_ref[1:2, vs]
                + jnp.where(rows_v < seq - 1, pltpu.roll(p, seq - 1, axis=0), 0.0) * cw_ref[2:3, vs])
        mixes[si].append(og_ref[rq, vs] * o.astype(BF16) + cbg_ref[rq, vs] * conv.astype(BF16))

    contrib = _dot(_cat([_cat(m, axis=1) for m in mixes], axis=0), wo_ref[...])
    ga1 = mod_ref[:, 2 * D_MODEL:3 * D_MODEL]
    if hsteps == 1:
        x1_ref[...] = x_ref[...] + ga1 * contrib
    else:
        hstep = pl.program_id(1)

        @pl.when(hstep == 0)
        def _():
            x1_ref[...] = x_ref[...] + ga1 * contrib

        @pl.when(hstep != 0)
        def _():
            x1_ref[...] += ga1 * contrib


def _mixer(x, mod3, proj, conv_w, gn, wo, states, *, seq, heads, seqs_per_step, seg, first_row, mod_row,
           has_state_out, run_after=None):
    q, k, v, og, p, cbg, g = proj
    n = x.shape[0]
    rows = seqs_per_step * seq
    hsteps = N_HEADS // heads
    has_state_in = states is not None
    assert seq % SUPER_BLOCK == 0 and seq % seg == 0 and first_row % rows == 0 and n % rows == 0
    b0 = first_row // rows
    a_buffers = max(1, min(seqs_per_step * heads, A_SCRATCH_BYTES // (2 * seq * seq)))
    col = lambda b, h: (b0 + b, h)
    st_block = (seqs_per_step, None, heads, HEAD_K, HEAD_V)
    st_map = lambda b, h: (b, 0, h, 0, 0)
    in_specs = [pl.BlockSpec((rows, heads * HEAD_K), col),
                pl.BlockSpec((rows, heads * HEAD_K), col),
                pl.BlockSpec((rows, heads * HEAD_V), col),
                pl.BlockSpec((rows, heads * HEAD_K), col),
                pl.BlockSpec((rows, heads * HEAD_K), lambda b, h: (b0 + b, hsteps + h)),
                pl.BlockSpec((rows, heads * HEAD_V), col),
                pl.BlockSpec((rows, heads * HEAD_V), col),
                pl.BlockSpec((rows, heads * HEAD_V), col),
                pl.BlockSpec((3, heads * HEAD_V), lambda b, h: (0, h)),
                pl.BlockSpec((1, HEAD_V), lambda b, h: (0, 0)),
                (pl.BlockSpec((heads * HEAD_V, D_MODEL), lambda b, h: (h, 0)) if hsteps > 1
                 else _const_spec((heads * HEAD_V, D_MODEL))),
                pl.BlockSpec((rows, D_MODEL), lambda b, h: (b, 0)),
                pl.BlockSpec((None, 1, N_MOD * D_MODEL), lambda b, h: (mod_row(b), 0, 0))]
    args = [q, k, v, g, g, og, p, cbg, conv_w, gn, wo, x, mod3]
    if has_state_in:
        in_specs += [pl.BlockSpec(st_block, st_map)] * 2
        args += list(states)
    if run_after is not None:
        in_specs.append(pl.BlockSpec(memory_space=pl.ANY))
        args.append(run_after)
    out_shape = [jax.ShapeDtypeStruct((n, D_MODEL), F32)]
    out_specs = [pl.BlockSpec((rows, D_MODEL), lambda b, h: (b, 0))]
    if has_state_out:
        out_shape += [jax.ShapeDtypeStruct((n // seq, 1, N_HEADS, HEAD_K, HEAD_V), F32)] * 2
        out_specs += [pl.BlockSpec(st_block, st_map)] * 2
    return pl.pallas_call(
        functools.partial(_mixer_kernel, seq=seq, heads=heads, seg=seg, has_state_in=has_state_in,
                          has_state_out=has_state_out, has_order_token=run_after is not None),
        out_shape=out_shape,
        grid=(n // rows, hsteps),
        in_specs=in_specs,
        out_specs=out_specs,
        scratch_shapes=[pltpu.VMEM((a_buffers, seq, seq), BF16)],
        compiler_params=_params("arbitrary", "arbitrary", vmem=MIXER_VMEM_LIMIT_BYTES),
        name="mixer",
    )(*args)


N_FFN_TILES = D_FF // FFN_TILE


def _ffn_kernel(x1_ref, mod_ref, n2_ref, wu_ref, wg_ref, cw_ref, cb_ref, wd_ref, nf_ref, o_ref, acc_ref,
                *, seq, width):
    cols = lambda f: slice(f * FFN_TILE, (f + 1) * FFN_TILE)
    _ffn_tile(x1_ref, mod_ref, n2_ref, cw_ref, cb_ref, nf_ref, o_ref, acc_ref, seq, width,
              lambda f: wu_ref[:, cols(f)], lambda f: wg_ref[:, cols(f)], lambda f: wd_ref[cols(f), :])


def _ffn_staging_kernel(x1_ref, mod_ref, n2_ref, wu_ref, wg_ref, cw_ref, cb_ref, wd_ref, nf_ref,
                        o_ref, wu_out, wg_out, wd_out, wu_scr, wg_scr, wd_scr, acc_ref, *, seq, width):
    step = pl.program_id(0)

    @pl.when(step < N_FFN_TILES)
    def _():
        for src, scr, out in ((wu_ref, wu_scr, wu_out), (wg_ref, wg_scr, wg_out), (wd_ref, wd_scr, wd_out)):
            w = src[...].astype(BF16)
            scr[step] = w
            out[...] = w

    @pl.when(step >= N_FFN_TILES)
    def _():
        _ffn_tile(x1_ref, mod_ref, n2_ref, cw_ref, cb_ref, nf_ref, o_ref, acc_ref, seq, width,
                  lambda f: wu_scr[f], lambda f: wg_scr[f], lambda f: wd_scr[f])


def _ffn_tile(x1_ref, mod_ref, n2_ref, cw_ref, cb_ref, nf_ref, o_ref, acc_ref, seq, width, wu, wg, wd):
    tm = x1_ref.shape[0]
    mod = mod_ref[...]
    sh2 = mod[:, 3 * D_MODEL:4 * D_MODEL]
    sc2 = mod[:, 4 * D_MODEL:5 * D_MODEL]
    ga2 = mod[:, 5 * D_MODEL:6 * D_MODEL]
    x1 = x1_ref[...]
    xb = (_rms(x1, n2_ref[...]) * (1.0 + sc2) + sh2).astype(BF16)

    rows = lax.broadcasted_iota(jnp.int32, (tm, FFN_TILE), 0)
    col_in_row = rows & (width - 1)
    has_left = col_in_row != 0
    has_right = col_in_row != width - 1
    n_rows = seq // width
    zrow = jnp.zeros((width, FFN_TILE), F32)

    def up_rows(z):
        return jnp.concatenate([zrow, z[:tm - width]], axis=0)

    def down_rows(z):
        return jnp.concatenate([z[width:], zrow], axis=0)

    def up_and_gate(f):
        return _dot(xb, wu(f)), _dot(xb, wg(f))

    ahead = up_and_gate(0)
    for f in range(N_FFN_TILES):
        fs = slice(f * FFN_TILE, (f + 1) * FFN_TILE)
        up, gate = ahead
        if f + 1 < N_FFN_TILES:
            ahead = up_and_gate(f + 1)
        left = jnp.where(has_left, pltpu.roll(up, 1, axis=0), 0.0)
        right = jnp.where(has_right, pltpu.roll(up, tm - 1, axis=0), 0.0)
        w = lambda i, j: cw_ref[3 * i + j:3 * i + j + 1, fs]
        taps = lambda i: left * w(i, 0) + up * w(i, 1) + right * w(i, 2)
        cv = taps(1) + cb_ref[:, fs]
        if n_rows > 1:
            cv = cv + up_rows(taps(0)) + down_rows(taps(2))
        act = (cv * _sigmoid(cv) * gate).astype(BF16)
        part = _dot(act, wd(f))
        if f == 0:
            acc_ref[...] = part
        else:
            acc_ref[...] += part

    o_ref[...] = _rms(x1 + ga2 * acc_ref[...], nf_ref[...])


def _ffn_staging(x1, mod3, n2g, wu, wg, cw, cb, wd, nfg, *, tm, seq, width, mod_row):
    assert width & (width - 1) == 0 and tm % seq == 0 and (seq == width or tm == seq)
    tile = lambda i: jnp.maximum(i - N_FFN_TILES, 0)
    w_step = lambda i: jnp.minimum(i, N_FFN_TILES - 1)
    col_blk = pl.BlockSpec((D_MODEL, FFN_TILE), lambda i: (0, w_step(i)))
    row_blk = pl.BlockSpec((FFN_TILE, D_MODEL), lambda i: (w_step(i), 0))
    return pl.pallas_call(
        functools.partial(_ffn_staging_kernel, seq=seq, width=width),
        out_shape=[jax.ShapeDtypeStruct(x1.shape, F32), jax.ShapeDtypeStruct(wu.shape, BF16),
                   jax.ShapeDtypeStruct(wg.shape, BF16), jax.ShapeDtypeStruct(wd.shape, BF16)],
        grid=(N_FFN_TILES + x1.shape[0] // tm,),
        in_specs=[pl.BlockSpec((tm, D_MODEL), lambda i: (tile(i), 0)),
                  pl.BlockSpec((None, 1, N_MOD * D_MODEL), lambda i: (mod_row(tile(i)), 0, 0)),
                  _const_spec(n2g.shape), col_blk, col_blk,
                  _const_spec(cw.shape), _const_spec(cb.shape), row_blk,
                  _const_spec(nfg.shape)],
        out_specs=[pl.BlockSpec((tm, D_MODEL), lambda i: (tile(i), 0)), col_blk, col_blk, row_blk],
        scratch_shapes=[pltpu.VMEM((N_FFN_TILES, D_MODEL, FFN_TILE), BF16),
                        pltpu.VMEM((N_FFN_TILES, D_MODEL, FFN_TILE), BF16),
                        pltpu.VMEM((N_FFN_TILES, FFN_TILE, D_MODEL), BF16),
                        pltpu.VMEM((tm, D_MODEL), F32)],
        compiler_params=_params("arbitrary"),
        name="ffn_staging",
    )(x1, mod3, n2g, wu, wg, cw, cb, wd, nfg)


def _ffn(x1, mod3, n2g, wu, wg, cw, cb, wd, nfg, *, tm, seq, width, mod_row):
    assert width & (width - 1) == 0 and tm % seq == 0 and (seq == width or tm == seq)
    return pl.pallas_call(
        functools.partial(_ffn_kernel, seq=seq, width=width),
        out_shape=jax.ShapeDtypeStruct(x1.shape, F32),
        grid=(x1.shape[0] // tm,),
        in_specs=[pl.BlockSpec((tm, D_MODEL), lambda i: (i, 0)),
                  pl.BlockSpec((None, 1, N_MOD * D_MODEL), lambda i: (mod_row(i), 0, 0)),
                  _const_spec(n2g.shape), _const_spec(wu.shape), _const_spec(wg.shape),
                  _const_spec(cw.shape), _const_spec(cb.shape), _const_spec(wd.shape),
                  _const_spec(nfg.shape)],
        out_specs=pl.BlockSpec((tm, D_MODEL), lambda i: (i, 0)),
        scratch_shapes=[pltpu.VMEM((tm, D_MODEL), F32)],
        compiler_params=_params("arbitrary"),
        name="ffn",
    )(x1, mod3, n2g, wu, wg, cw, cb, wd, nfg)


def kernel(x_prompt, x_sample, c, state_gla_fwd, state_gla_bwd, c_ctx, w_ada, b_ada, norm1_g, w_in, w_gk_f,
           b_gk_f, w_gk_b, b_gk_b, gla_norm_g, conv_mix_w, w_out, norm2_g, ffn_w_up, ffn_w_gate,
           ffn_conv_w, ffn_conv_b, ffn_w_down, normf_g):
    bp, lp, d = x_prompt.shape
    bs, ls, _ = x_sample.shape
    assert w_ada.shape[0] == 1 and d == D_MODEL and bs + 1 <= MOD_ROWS

    mod3 = _ada(c_ctx[None, :], c, w_ada[0], b_ada)

    assert IN_SPLITS == (KEY_DIM, KEY_DIM, VAL_DIM, VAL_DIM, GATE_RANK, GATE_RANK) + (CONV_DIM,) * 5
    wo = w_out[0].astype(BF16)
    cw9 = ffn_conv_w[0].reshape(9, D_FF)
    cb = ffn_conv_b
    n1g, n2g, nfg, gn = norm1_g, norm2_g, normf_g[None, :], gla_norm_g

    xp, xs = x_prompt.reshape(-1, d), x_sample.reshape(-1, d)
    tiles_p = xp.shape[0] // INPROJ_ROWS
    seg = min(lp, ls, INPROJ_ROWS)
    proj = _inproj(xp, xs, mod3, n1g, w_in[0].T, w_gk_f[0], w_gk_b[0], b_gk_f, b_gk_b, tm=INPROJ_ROWS, seg=seg,
                   mod_row=lambda t: jnp.where(t < tiles_p, 0, 1 + ((t - tiles_p) * INPROJ_ROWS) // ls))

    mix = functools.partial(_mixer, mod3=mod3, proj=proj, conv_w=conv_mix_w[0], gn=gn, wo=wo, seg=seg)
    x1p, new_f, new_b = mix(xp, states=None, seq=lp, heads=N_HEADS, seqs_per_step=PROMPT_SEQS_PER_STEP,
                            first_row=0, mod_row=lambda b: 0, has_state_out=True)
    y_prompt, wu, wg, wd = _ffn_staging(x1p, mod3, n2g, ffn_w_up[0], ffn_w_gate[0], cw9, cb, ffn_w_down[0], nfg,
                                        tm=FFN_ROWS // 2, seq=lp, width=lp, mod_row=lambda i: 0)
    x1s, = mix(xs, states=(state_gla_fwd, state_gla_bwd), seq=ls, heads=N_HEADS, seqs_per_step=1,
               first_row=xp.shape[0], mod_row=lambda b: 1 + b, has_state_out=False, run_after=wd)
    y_sample = _ffn(x1s, mod3, n2g, wu, wg, cw9, cb, wd, nfg, tm=FFN_ROWS, seq=ls, width=GRID_W,
                    mod_row=lambda i: 1 + (i * FFN_ROWS) // ls)
    return y_prompt.reshape(x_prompt.shape), y_sample.reshape(x_sample.shape), new_f, new_b
```

```python
import functools

import jax
import jax.numpy as jnp
from jax import lax
from jax.experimental import pallas as pl
from jax.experimental.pallas import tpu as pltpu

D_MODEL = 1024
N_HEADS = 4
HEAD_K = 128
HEAD_V = 256
KEY_DIM = N_HEADS * HEAD_K
VAL_DIM = N_HEADS * HEAD_V
GATE_RANK = 16
GATE_NORMALIZER = 16.0
LOG2_E = 1.4426950408889634
CONV_DIM = D_MODEL
D_FF = 2816
N_MOD = 6
EPS = 1e-6
GRID_W = 64
IN_SPLITS = (KEY_DIM, KEY_DIM, VAL_DIM, VAL_DIM, GATE_RANK, GATE_RANK,
             CONV_DIM, CONV_DIM, CONV_DIM, VAL_DIM, CONV_DIM)

DIAG_BLOCK = 64
SUPER_BLOCK = 256
MXU_LANES_V7X = 128
FFN_TILE = 256
FFN_ROWS = 1024
INPROJ_ROWS = 512
PROMPT_SEQS_PER_STEP = 2
ADA_TILE = 1024
MOD_ROWS = 8
VMEM_LIMIT_BYTES = 56 * 1024 * 1024
A_SCRATCH_BYTES = 4 * 1024 * 1024
MIXER_VMEM_LIMIT_BYTES = 61 * 1024 * 1024

F32 = jnp.float32
BF16 = jnp.bfloat16


def _dot(a, b):
    return jnp.dot(a, b, preferred_element_type=F32)


def _dot_nt(a, b):
    return lax.dot_general(a, b, (((1,), (1,)), ((), ())), preferred_element_type=F32)


def _dot_tn(a, b):
    return lax.dot_general(a, b, (((0,), (0,)), ((), ())), preferred_element_type=F32)


def _sigmoid(x):
    return 1.0 / (1.0 + jnp.exp(-x))


def _rms(x, g):
    return x * lax.rsqrt(jnp.mean(x * x, axis=-1, keepdims=True) + EPS) * g


def _const_spec(shape):
    nd = len(shape)
    return pl.BlockSpec(shape, lambda *_: (0,) * nd, pipeline_mode=pl.Buffered(1))


def _params(*sem, vmem=VMEM_LIMIT_BYTES):
    return pltpu.CompilerParams(dimension_semantics=sem, vmem_limit_bytes=vmem)


def _ada_kernel(cctx_ref, c_ref, w_ref, b_ref, o_ref, rows_ref):
    rows_ref[...] = jnp.zeros_like(rows_ref)
    rows_ref[0:1, :] = cctx_ref[...]
    rows_ref[1:1 + c_ref.shape[0], :] = c_ref[...]
    c = rows_ref[...]
    s = (c * _sigmoid(c)).astype(BF16)
    o_ref[:, 0, :] = _dot(s, w_ref[...].astype(BF16)) + b_ref[...]


def _ada(c_ctx, c, w_ada, b_ada):
    n = w_ada.shape[1]
    assert 1 + c.shape[0] <= MOD_ROWS
    return pl.pallas_call(
        _ada_kernel,
        out_shape=jax.ShapeDtypeStruct((MOD_ROWS, 1, n), F32),
        grid=(n // ADA_TILE,),
        in_specs=[pl.BlockSpec(c_ctx.shape, lambda j: (0, 0)),
                  pl.BlockSpec(c.shape, lambda j: (0, 0)),
                  pl.BlockSpec((D_MODEL, ADA_TILE), lambda j: (0, j)),
                  pl.BlockSpec((1, ADA_TILE), lambda j: (0, j))],
        out_specs=pl.BlockSpec((MOD_ROWS, 1, ADA_TILE), lambda j: (0, 0, j)),
        scratch_shapes=[pltpu.VMEM((MOD_ROWS, D_MODEL), F32)],
        compiler_params=_params("arbitrary"),
        name="ada",
    )(c_ctx, c, w_ada, b_ada)


W_BLOCK = 512
HEAD_COLS = 2 * KEY_DIM + 2 * VAL_DIM
TAIL_OFF = HEAD_COLS + 2 * GATE_RANK
TAIL_SHIFT = TAIL_OFF % MXU_LANES_V7X
N_HEAD_BLOCKS = HEAD_COLS // W_BLOCK
N_W_BLOCKS = N_HEAD_BLOCKS + 5 * CONV_DIM // W_BLOCK


def _inproj_kernel(xa_ref, xb_ref, mod_ref, n1_ref, wa_ref, wb_ref, wgf_ref, wgb_ref, bgf_ref, bgb_ref,
                   q_ref, k_ref, v_ref, og_ref, p_ref, cbg_ref, g_ref, w_scr, wcode_scr, wgk_scr,
                   *, tiles_a, seg):
    step = pl.program_id(0)

    @pl.when(step < N_HEAD_BLOCKS)
    def _():
        w_scr[step] = wa_ref[...].T.astype(BF16)
        wgk_scr[...] = jnp.zeros_like(wgk_scr)
        wgk_scr[0:GATE_RANK, 0:KEY_DIM] = wgf_ref[...].astype(BF16)
        wgk_scr[GATE_RANK:2 * GATE_RANK, KEY_DIM:] = wgb_ref[...].astype(BF16)
        lane = lax.broadcasted_iota(jnp.int32, (D_MODEL, MXU_LANES_V7X), 1)
        wcode_scr[...] = jnp.where(lane < 2 * GATE_RANK, wb_ref[...].T, 0.0).astype(BF16)

    @pl.when(jnp.logical_and(step >= N_HEAD_BLOCKS, step < N_W_BLOCKS))
    def _():
        rows = jnp.concatenate([wa_ref[TAIL_SHIFT:, :], wb_ref[:TAIL_SHIFT, :]], axis=0)
        w_scr[step] = rows.T.astype(BF16)

    @pl.when(step >= N_W_BLOCKS)
    def _():
        first_group = step - N_W_BLOCKS < tiles_a
        mod = mod_ref[...]
        sh1 = mod[:, 0:D_MODEL]
        sc1 = mod[:, D_MODEL:2 * D_MODEL]
        x = jnp.where(first_group, xa_ref[...], xb_ref[...])
        xn = (_rms(x, n1_ref[...]) * (1.0 + sc1) + sh1).astype(BF16)
        halves = range(CONV_DIM // W_BLOCK)
        head = lambda blk: _dot(xn, w_scr[blk])
        tail = lambda grp, j: _dot(xn, w_scr[N_HEAD_BLOCKS + grp * len(halves) + j])
        cols = lambda j: slice(j * W_BLOCK, (j + 1) * W_BLOCK)

        z = (_dot(_dot(xn, wcode_scr[...]).astype(BF16), wgk_scr[...])
             + jnp.concatenate([bgf_ref[...], bgb_ref[...]], axis=1))
        g = (jnp.minimum(z, 0.0) - jnp.log(1.0 + jnp.exp(-jnp.abs(z)))) * (LOG2_E / GATE_NORMALIZER)
        g_hi = g.astype(BF16)
        g_lo = (g - g_hi.astype(F32)).astype(BF16)

        q_ref[...] = (head(0) * (HEAD_K ** -0.5)).astype(BF16)
        k_ref[...] = head(1).astype(BF16)
        for j in halves:
            v_ref[:, cols(j)] = head(2 + j).astype(BF16)

        r = lax.broadcasted_iota(jnp.int32, (seg, seg), 0)
        c = lax.broadcasted_iota(jnp.int32, (seg, seg), 1)
        lower = jnp.where(c <= r, 1.0, 0.0).astype(BF16)
        upper = jnp.where(c >= r, 1.0, 0.0).astype(BF16)
        for s0 in range(0, g.shape[0], seg):
            rows = slice(s0, s0 + seg)
            for tri, cs in ((lower, slice(0, KEY_DIM)), (upper, slice(KEY_DIM, 2 * KEY_DIM))):
                terms = jnp.concatenate([g_hi[rows, cs], g_lo[rows, cs]], axis=0)
                g_ref[rows, cs] = _dot(jnp.concatenate([tri, tri], axis=1), terms)

        for j in halves:
            g_out = head(2 + len(halves) + j)
            og_ref[:, cols(j)] = (g_out * _sigmoid(g_out) * _sigmoid(tail(3, j))).astype(BF16)
            p_ref[:, cols(j)] = (tail(1, j) * tail(2, j)).astype(BF16)
            cbg_ref[:, cols(j)] = (_sigmoid(tail(4, j)) * tail(0, j)).astype(BF16)


def _inproj(xa, xb, mod3, n1g, w_in_t, wgf, wgb, bgf, bgb, *, tm, seg, mod_row):
    tiles_a, tiles_b = xa.shape[0] // tm, xb.shape[0] // tm
    n = xa.shape[0] + xb.shape[0]
    assert seg & (seg - 1) == 0 and tm % seg == 0
    assert KEY_DIM == W_BLOCK and HEAD_COLS % W_BLOCK == 0 and W_BLOCK % MXU_LANES_V7X == 0
    tile = lambda i: jnp.maximum(i - N_W_BLOCKS, 0)
    row = lambda i: (tile(i), 0)
    lanes_per_block = W_BLOCK // MXU_LANES_V7X
    wide = lambda w, dt: (jax.ShapeDtypeStruct((n, w), dt), pl.BlockSpec((tm, w), row))
    outs = [wide(KEY_DIM, BF16), wide(KEY_DIM, BF16), wide(VAL_DIM, BF16), wide(VAL_DIM, BF16),
            wide(CONV_DIM, BF16), wide(CONV_DIM, BF16), wide(2 * KEY_DIM, F32)]
    return pl.pallas_call(
        functools.partial(_inproj_kernel, tiles_a=tiles_a, seg=seg),
        out_shape=[o[0] for o in outs],
        grid=(N_W_BLOCKS + tiles_a + tiles_b,),
        in_specs=[pl.BlockSpec((tm, D_MODEL), lambda i: (jnp.minimum(tile(i), tiles_a - 1), 0)),
                  pl.BlockSpec((tm, D_MODEL), lambda i: (jnp.maximum(tile(i) - tiles_a, 0), 0)),
                  pl.BlockSpec((None, 1, N_MOD * D_MODEL), lambda i: (mod_row(tile(i)), 0, 0)),
                  _const_spec(n1g.shape),
                  pl.BlockSpec((W_BLOCK, D_MODEL), lambda i: (jnp.minimum(i, N_W_BLOCKS - 1), 0)),
                  pl.BlockSpec((MXU_LANES_V7X, D_MODEL),
                               lambda i: (jnp.where(i < N_HEAD_BLOCKS, HEAD_COLS // MXU_LANES_V7X,
                                                    (jnp.minimum(i, N_W_BLOCKS - 1) + 1) * lanes_per_block), 0)),
                  _const_spec(wgf.shape), _const_spec(wgb.shape), _const_spec(bgf.shape), _const_spec(bgb.shape)],
        out_specs=[o[1] for o in outs],
        scratch_shapes=[pltpu.VMEM((N_W_BLOCKS, D_MODEL, W_BLOCK), BF16),
                        pltpu.VMEM((D_MODEL, MXU_LANES_V7X), BF16),
                        pltpu.VMEM((MXU_LANES_V7X, 2 * KEY_DIM), BF16)],
        compiler_params=_params("arbitrary"),
        name="in_proj",
    )(xa, xb, mod3, n1g, w_in_t, w_in_t, wgf, wgb, bgf, bgb)


def _cat(parts, axis=0):
    return parts[0] if len(parts) == 1 else jnp.concatenate(parts, axis=axis)


def _join_prefix(b, seg):
    parts = [b[:seg]]
    for c in range(1, b.shape[0] // seg):
        parts.append(b[c * seg:(c + 1) * seg] + parts[-1][seg - 1:seg, :])
    return _cat(parts)


def _join_suffix(b, seg):
    n = b.shape[0] // seg
    parts = [b[(n - 1) * seg:]]
    for c in range(n - 2, -1, -1):
        parts.insert(0, b[c * seg:(c + 1) * seg] + parts[0][0:1, :])
    return _cat(parts)


def _diag_args(b, off):
    return _cat([b[c:c + DIAG_BLOCK] - b[c + off:c + off + 1, :] for c in range(0, b.shape[0], DIAG_BLOCK)])


def _pair_args(bf, sb, s):
    lhs, rhs = [], []
    for e0 in range(0, bf.shape[0], 2 * s):
        o0 = e0 + s
        rf = bf[o0 - 1:o0, :]
        rb = sb[o0:o0 + 1, :]
        lhs += [sb[e0:o0] - rb, bf[o0:o0 + s] - rf]
        rhs += [rf - bf[e0:o0], rb - sb[o0:o0 + s]]
    return _cat(lhs), _cat(rhs)


def _mixer_kernel(*refs, seq, heads, seg, has_state_in, has_state_out, has_order_token):
    (q_ref, k_ref, v_ref, gf_ref, gb_ref, og_ref, p_ref, cbg_ref, cw_ref, gn_ref, wo_ref, x_ref,
     mod_ref) = refs[:13]
    refs = refs[13:]
    if has_state_in:
        s0f_ref, s0b_ref = refs[:2]
        refs = refs[2:]
    if has_order_token:
        refs = refs[1:]
    x1_ref = refs[0]
    refs = refs[1:]
    if has_state_out:
        sf_ref, sb_ref = refs[:2]
        refs = refs[2:]
    a_scr = refs[0]
    hsteps = N_HEADS // heads
    half = SUPER_BLOCK // 2
    n_seq = x_ref.shape[0] // seq

    rows_v = lax.broadcasted_iota(jnp.int32, (seq, HEAD_V), 0)
    r_loc = lax.broadcasted_iota(jnp.int32, (DIAG_BLOCK, half), 0)
    lane = lax.broadcasted_iota(jnp.int32, (DIAG_BLOCK, half), 1)
    ones = jnp.ones((HEAD_V, half), BF16)
    scale = lambda t, e: (t.astype(F32) * e).astype(BF16)

    mixes = [[] for _ in range(n_seq)]
    for si, h in [(si, h) for h in range(heads) for si in range(n_seq)]:
        rq = slice(si * seq, (si + 1) * seq)
        a_ref = a_scr.at[(si * heads + h) % a_scr.shape[0]]
        ks = slice(h * HEAD_K, (h + 1) * HEAD_K)
        vs = slice(h * HEAD_V, (h + 1) * HEAD_V)
        q = q_ref[rq, ks]
        k = k_ref[rq, ks]
        v = v_ref[rq, vs]
        bf = _join_prefix(gf_ref[rq, ks], seg)
        sb = _join_suffix(gb_ref[rq, ks], seg)

        e0f = jnp.exp2(_diag_args(bf, DIAG_BLOCK // 2))
        e0b = jnp.exp2(_diag_args(sb, DIAG_BLOCK // 2 - 1))
        q0f, k0f = scale(q, e0f), scale(k, 1.0 / e0f)
        q0b, k0b = scale(q, e0b), scale(k, 1.0 / e0b)

        lhs_levels, rhs_levels = [], []
        s = DIAG_BLOCK
        while s < seq:
            la, ra = _pair_args(bf, sb, s)
            lhs_levels.append(scale(q, jnp.exp2(la)))
            rhs_levels.append(scale(k, jnp.exp2(ra)))
            s *= 2

        for base in range(0, seq, SUPER_BLOCK):
            rs = slice(base, base + SUPER_BLOCK)
            m0f = _dot_nt(q0f[rs], k0f[rs])
            m0b = _dot_nt(q0b[rs], k0b[rs])
            m1 = _dot_nt(lhs_levels[0][rs], rhs_levels[0][rs])
            m2 = _dot_nt(lhs_levels[1][rs], rhs_levels[1][rs])
            for bi in range(SUPER_BLOCK // DIAG_BLOCK):
                rr = slice(bi * DIAG_BLOCK, (bi + 1) * DIAG_BLOCK)
                for lt in range(2):
                    cc = slice(lt * half, (lt + 1) * half)
                    if lt == bi // 2:
                        c_loc = lane - DIAG_BLOCK * (bi % 2)
                        in_diag = (lane >= DIAG_BLOCK) if bi % 2 else (lane < DIAG_BLOCK)
                        diag = (jnp.where(c_loc <= r_loc, m0f[rr, cc], 0.0)
                                + jnp.where(c_loc >= r_loc, m0b[rr, cc], 0.0))
                        piece = jnp.where(in_diag, diag, m1[rr, cc])
                    else:
                        piece = m2[rr, cc]
                    a_ref[base + bi * DIAG_BLOCK:base + (bi + 1) * DIAG_BLOCK,
                          base + lt * half:base + (lt + 1) * half] = piece.astype(BF16)
        s = SUPER_BLOCK
        lev = 2
        while s < seq:
            lhs, rhs = lhs_levels[lev], rhs_levels[lev]
            for e0 in range(0, seq, 2 * s):
                ev = slice(e0, e0 + s)
                od = slice(e0 + s, e0 + 2 * s)
                a_ref[od, ev] = _dot_nt(lhs[od], rhs[ev]).astype(BF16)
                a_ref[ev, od] = _dot_nt(lhs[ev], rhs[od]).astype(BF16)
            s *= 2
            lev += 1

        o = _dot(a_ref[...], v)
        if has_state_in:
            qi = jnp.concatenate([scale(q, jnp.exp2(bf)), scale(q, jnp.exp2(sb))], axis=1)
            s0 = jnp.concatenate([s0f_ref[si, h].astype(BF16), s0b_ref[si, h].astype(BF16)], axis=0)
            o = o + _dot(qi, s0)
        if has_state_out:
            sf_ref[si, h] = _dot_tn(scale(k, jnp.exp2(bf[seq - 1:seq, :] - bf)), v)
            sb_ref[si, h] = _dot_tn(scale(k, jnp.exp2(sb[0:1, :] - sb)), v)

        inv = lax.rsqrt(_dot((o * o).astype(BF16), ones) * (1.0 / HEAD_V) + EPS)
        o = jnp.concatenate([o[:, :half] * inv, o[:, half:] * inv], axis=1) * gn_ref[...]
        p = p_ref[rq, vs].astype(F32)
        conv = (jnp.where(rows_v >= 1, pltpu.roll(p, 1, axis=0), 0.0) * cw_ref[0:1, vs] + p * cw_ref[1:2, vs]
                + jnp.where(rows_v < seq - 1, pltpu.roll(p, seq - 1, axis=0), 0.0) * cw_ref[2:3, vs])
        mixes[si].append(og_ref[rq, vs] * o.astype(BF16) + cbg_ref[rq, vs] * conv.astype(BF16))

    contrib = _dot(_cat([_cat(m, axis=1) for m in mixes], axis=0), wo_ref[...])
    ga1 = mod_ref[:, 2 * D_MODEL:3 * D_MODEL]
    if hsteps == 1:
        x1_ref[...] = x_ref[...] + ga1 * contrib
    else:
        hstep = pl.program_id(1)

        @pl.when(hstep == 0)
        def _():
            x1_ref[...] = x_ref[...] + ga1 * contrib

        @pl.when(hstep != 0)
        def _():
            x1_ref[...] += ga1 * contrib


def _mixer(x, mod3, proj, conv_w, gn, wo, states, *, seq, heads, seqs_per_step, seg, first_row, mod_row,
           has_state_out, run_after=None):
    q, k, v, og, p, cbg, g = proj
    n = x.shape[0]
    rows = seqs_per_step * seq
    hsteps = N_HEADS // heads
    has_state_in = states is not None
    assert seq % SUPER_BLOCK == 0 and seq % seg == 0 and first_row % rows == 0 and n % rows == 0
    b0 = first_row // rows
    a_buffers = max(1, min(seqs_per_step * heads, A_SCRATCH_BYTES // (2 * seq * seq)))
    col = lambda b, h: (b0 + b, h)
    st_block = (seqs_per_step, None, heads, HEAD_K, HEAD_V)
    st_map = lambda b, h: (b, 0, h, 0, 0)
    in_specs = [pl.BlockSpec((rows, heads * HEAD_K), col),
                pl.BlockSpec((rows, heads * HEAD_K), col),
                pl.BlockSpec((rows, heads * HEAD_V), col),
                pl.BlockSpec((rows, heads * HEAD_K), col),
                pl.BlockSpec((rows, heads * HEAD_K), lambda b, h: (b0 + b, hsteps + h)),
                pl.BlockSpec((rows, heads * HEAD_V), col),
                pl.BlockSpec((rows, heads * HEAD_V), col),
                pl.BlockSpec((rows, heads * HEAD_V), col),
                pl.BlockSpec((3, heads * HEAD_V), lambda b, h: (0, h)),
                pl.BlockSpec((1, HEAD_V), lambda b, h: (0, 0)),
                (pl.BlockSpec((heads * HEAD_V, D_MODEL), lambda b, h: (h, 0)) if hsteps > 1
                 else _const_spec((heads * HEAD_V, D_MODEL))),
                pl.BlockSpec((rows, D_MODEL), lambda b, h: (b, 0)),
                pl.BlockSpec((None, 1, N_MOD * D_MODEL), lambda b, h: (mod_row(b), 0, 0))]
    args = [q, k, v, g, g, og, p, cbg, conv_w, gn, wo, x, mod3]
    if has_state_in:
        in_specs += [pl.BlockSpec(st_block, st_map)] * 2
        args += list(states)
    if run_after is not None:
        in_specs.append(pl.BlockSpec(memory_space=pl.ANY))
        args.append(run_after)
    out_shape = [jax.ShapeDtypeStruct((n, D_MODEL), F32)]
    out_specs = [pl.BlockSpec((rows, D_MODEL), lambda b, h: (b, 0))]
    if has_state_out:
        out_shape += [jax.ShapeDtypeStruct((n // seq, 1, N_HEADS, HEAD_K, HEAD_V), F32)] * 2
        out_specs += [pl.BlockSpec(st_block, st_map)] * 2
    return pl.pallas_call(
        functools.partial(_mixer_kernel, seq=seq, heads=heads, seg=seg, has_state_in=has_state_in,
                          has_state_out=has_state_out, has_order_token=run_after is not None),
        out_shape=out_shape,
        grid=(n // rows, hsteps),
        in_specs=in_specs,
        out_specs=out_specs,
        scratch_shapes=[pltpu.VMEM((a_buffers, seq, seq), BF16)],
        compiler_params=_params("arbitrary", "arbitrary", vmem=MIXER_VMEM_LIMIT_BYTES),
        name="mixer",
    )(*args)


N_FFN_TILES = D_FF // FFN_TILE


def _ffn_kernel(x1_ref, mod_ref, n2_ref, wu_ref, wg_ref, cw_ref, cb_ref, wd_ref, nf_ref, o_ref, acc_ref,
                *, seq, width):
    cols = lambda f: slice(f * FFN_TILE, (f + 1) * FFN_TILE)
    _ffn_tile(x1_ref, mod_ref, n2_ref, cw_ref, cb_ref, nf_ref, o_ref, acc_ref, seq, width,
              lambda f: wu_ref[:, cols(f)], lambda f: wg_ref[:, cols(f)], lambda f: wd_ref[cols(f), :])


def _ffn_staging_kernel(x1_ref, mod_ref, n2_ref, wu_ref, wg_ref, cw_ref, cb_ref, wd_ref, nf_ref,
                        o_ref, wu_out, wg_out, wd_out, wu_scr, wg_scr, wd_scr, acc_ref, *, seq, width):
    step = pl.program_id(0)

    @pl.when(step < N_FFN_TILES)
    def _():
        for src, scr, out in ((wu_ref, wu_scr, wu_out), (wg_ref, wg_scr, wg_out), (wd_ref, wd_scr, wd_out)):
            w = src[...].astype(BF16)
            scr[step] = w
            out[...] = w

    @pl.when(step >= N_FFN_TILES)
    def _():
        _ffn_tile(x1_ref, mod_ref, n2_ref, cw_ref, cb_ref, nf_ref, o_ref, acc_ref, seq, width,
                  lambda f: wu_scr[f], lambda f: wg_scr[f], lambda f: wd_scr[f])


def _ffn_tile(x1_ref, mod_ref, n2_ref, cw_ref, cb_ref, nf_ref, o_ref, acc_ref, seq, width, wu, wg, wd):
    tm = x1_ref.shape[0]
    mod = mod_ref[...]
    sh2 = mod[:, 3 * D_MODEL:4 * D_MODEL]
    sc2 = mod[:, 4 * D_MODEL:5 * D_MODEL]
    ga2 = mod[:, 5 * D_MODEL:6 * D_MODEL]
    x1 = x1_ref[...]
    xb = (_rms(x1, n2_ref[...]) * (1.0 + sc2) + sh2).astype(BF16)

    rows = lax.broadcasted_iota(jnp.int32, (tm, FFN_TILE), 0)
    col_in_row = rows & (width - 1)
    has_left = col_in_row != 0
    has_right = col_in_row != width - 1
    n_rows = seq // width
    zrow = jnp.zeros((width, FFN_TILE), F32)

    def up_rows(z):
        return jnp.concatenate([zrow, z[:tm - width]], axis=0)

    def down_rows(z):
        return jnp.concatenate([z[width:], zrow], axis=0)

    def up_and_gate(f):
        return _dot(xb, wu(f)), _dot(xb, wg(f))

    ahead = up_and_gate(0)
    for f in range(N_FFN_TILES):
        fs = slice(f * FFN_TILE, (f + 1) * FFN_TILE)
        up, gate = ahead
        if f + 1 < N_FFN_TILES:
            ahead = up_and_gate(f + 1)
        left = jnp.where(has_left, pltpu.roll(up, 1, axis=0), 0.0)
        right = jnp.where(has_right, pltpu.roll(up, tm - 1, axis=0), 0.0)
        w = lambda i, j: cw_ref[3 * i + j:3 * i + j + 1, fs]
        taps = lambda i: left * w(i, 0) + up * w(i, 1) + right * w(i, 2)
        cv = taps(1) + cb_ref[:, fs]
        if n_rows > 1:
            cv = cv + up_rows(taps(0)) + down_rows(taps(2))
        act = (cv * _sigmoid(cv) * gate).astype(BF16)
        part = _dot(act, wd(f))
        if f == 0:
            acc_ref[...] = part
        else:
            acc_ref[...] += part

    o_ref[...] = _rms(x1 + ga2 * acc_ref[...], nf_ref[...])


def _ffn_staging(x1, mod3, n2g, wu, wg, cw, cb, wd, nfg, *, tm, seq, width, mod_row):
    assert width & (width - 1) == 0 and tm % seq == 0 and (seq == width or tm == seq)
    tile = lambda i: jnp.maximum(i - N_FFN_TILES, 0)
    w_step = lambda i: jnp.minimum(i, N_FFN_TILES - 1)
    col_blk = pl.BlockSpec((D_MODEL, FFN_TILE), lambda i: (0, w_step(i)))
    row_blk = pl.BlockSpec((FFN_TILE, D_MODEL), lambda i: (w_step(i), 0))
    return pl.pallas_call(
        functools.partial(_ffn_staging_kernel, seq=seq, width=width),
        out_shape=[jax.ShapeDtypeStruct(x1.shape, F32), jax.ShapeDtypeStruct(wu.shape, BF16),
                   jax.ShapeDtypeStruct(wg.shape, BF16), jax.ShapeDtypeStruct(wd.shape, BF16)],
        grid=(N_FFN_TILES + x1.shape[0] // tm,),
        in_specs=[pl.BlockSpec((tm, D_MODEL), lambda i: (tile(i), 0)),
                  pl.BlockSpec((None, 1, N_MOD * D_MODEL), lambda i: (mod_row(tile(i)), 0, 0)),
                  _const_spec(n2g.shape), col_blk, col_blk,
                  _const_spec(cw.shape), _const_spec(cb.shape), row_blk,
                  _const_spec(nfg.shape)],
        out_specs=[pl.BlockSpec((tm, D_MODEL), lambda i: (tile(i), 0)), col_blk, col_blk, row_blk],
        scratch_shapes=[pltpu.VMEM((N_FFN_TILES, D_MODEL, FFN_TILE), BF16),
                        pltpu.VMEM((N_FFN_TILES, D_MODEL, FFN_TILE), BF16),
                        pltpu.VMEM((N_FFN_TILES, FFN_TILE, D_MODEL), BF16),
                        pltpu.VMEM((tm, D_MODEL), F32)],
        compiler_params=_params("arbitrary"),
        name="ffn_staging",
    )(x1, mod3, n2g, wu, wg, cw, cb, wd, nfg)


def _ffn(x1, mod3, n2g, wu, wg, cw, cb, wd, nfg, *, tm, seq, width, mod_row):
    assert width & (width - 1) == 0 and tm % seq == 0 and (seq == width or tm == seq)
    return pl.pallas_call(
        functools.partial(_ffn_kernel, seq=seq, width=width),
        out_shape=jax.ShapeDtypeStruct(x1.shape, F32),
        grid=(x1.shape[0] // tm,),
        in_specs=[pl.BlockSpec((tm, D_MODEL), lambda i: (i, 0)),
                  pl.BlockSpec((None, 1, N_MOD * D_MODEL), lambda i: (mod_row(i), 0, 0)),
                  _const_spec(n2g.shape), _const_spec(wu.shape), _const_spec(wg.shape),
                  _const_spec(cw.shape), _const_spec(cb.shape), _const_spec(wd.shape),
                  _const_spec(nfg.shape)],
        out_specs=pl.BlockSpec((tm, D_MODEL), lambda i: (i, 0)),
        scratch_shapes=[pltpu.VMEM((tm, D_MODEL), F32)],
        compiler_params=_params("arbitrary"),
        name="ffn",
    )(x1, mod3, n2g, wu, wg, cw, cb, wd, nfg)


def kernel(x_prompt, x_sample, c, state_gla_fwd, state_gla_bwd, c_ctx, w_ada, b_ada, norm1_g, w_in, w_gk_f,
           b_gk_f, w_gk_b, b_gk_b, gla_norm_g, conv_mix_w, w_out, norm2_g, ffn_w_up, ffn_w_gate,
           ffn_conv_w, ffn_conv_b, ffn_w_down, normf_g):
    bp, lp, d = x_prompt.shape
    bs, ls, _ = x_sample.shape
    assert w_ada.shape[0] == 1 and d == D_MODEL and bs + 1 <= MOD_ROWS

    mod3 = _ada(c_ctx[None, :], c, w_ada[0], b_ada)

    assert IN_SPLITS == (KEY_DIM, KEY_DIM, VAL_DIM, VAL_DIM, GATE_RANK, GATE_RANK) + (CONV_DIM,) * 5
    wo = w_out[0].astype(BF16)
    cw9 = ffn_conv_w[0].reshape(9, D_FF)
    cb = ffn_conv_b
    n1g, n2g, nfg, gn = norm1_g, norm2_g, normf_g[None, :], gla_norm_g

    xp, xs = x_prompt.reshape(-1, d), x_sample.reshape(-1, d)
    tiles_p = xp.shape[0] // INPROJ_ROWS
    seg = min(lp, ls, INPROJ_ROWS)
    proj = _inproj(xp, xs, mod3, n1g, w_in[0].T, w_gk_f[0], w_gk_b[0], b_gk_f, b_gk_b, tm=INPROJ_ROWS, seg=seg,
                   mod_row=lambda t: jnp.where(t < tiles_p, 0, 1 + ((t - tiles_p) * INPROJ_ROWS) // ls))

    mix = functools.partial(_mixer, mod3=mod3, proj=proj, conv_w=conv_mix_w[0], gn=gn, wo=wo, seg=seg)
    x1p, new_f, new_b = mix(xp, states=None, seq=lp, heads=N_HEADS, seqs_per_step=PROMPT_SEQS_PER_STEP,
                            first_row=0, mod_row=lambda b: 0, has_state_out=True)
    y_prompt, wu, wg, wd = _ffn_staging(x1p, mod3, n2g, ffn_w_up[0], ffn_w_gate[0], cw9, cb, ffn_w_down[0], nfg,
                                        tm=FFN_ROWS // 4, seq=lp, width=lp, mod_row=lambda i: 0)
    x1s, = mix(xs, states=(state_gla_fwd, state_gla_bwd), seq=ls, heads=N_HEADS, seqs_per_step=1,
               first_row=xp.shape[0], mod_row=lambda b: 1 + b, has_state_out=False, run_after=wd)
    y_sample = _ffn(x1s, mod3, n2g, wu, wg, cw9, cb, wd, nfg, tm=FFN_ROWS, seq=ls, width=GRID_W,
                    mod_row=lambda i: 1 + (i * FFN_ROWS) // ls)
    return y_prompt.reshape(x_prompt.shape), y_sample.reshape(x_sample.shape), new_f, new_b
```

```python
import functools

import jax
import jax.numpy as jnp
from jax import lax
from jax.experimental import pallas as pl
from jax.experimental.pallas import tpu as pltpu

D_MODEL = 1024
N_HEADS = 4
HEAD_K = 128
HEAD_V = 256
KEY_DIM = N_HEADS * HEAD_K
VAL_DIM = N_HEADS * HEAD_V
GATE_RANK = 16
GATE_NORMALIZER = 16.0
LOG2_E = 1.4426950408889634
CONV_DIM = D_MODEL
D_FF = 2816
N_MOD = 6
EPS = 1e-6
GRID_W = 64
IN_SPLITS = (KEY_DIM, KEY_DIM, VAL_DIM, VAL_DIM, GATE_RANK, GATE_RANK,
             CONV_DIM, CONV_DIM, CONV_DIM, VAL_DIM, CONV_DIM)

DIAG_BLOCK = 64
SUPER_BLOCK = 256
MXU_LANES_V7X = 128
FFN_TILE = 256
FFN_ROWS = 1024
INPROJ_ROWS = 256
PROMPT_SEQS_PER_STEP = 2
ADA_TILE = 1024
MOD_ROWS = 8
VMEM_LIMIT_BYTES = 56 * 1024 * 1024
A_SCRATCH_BYTES = 4 * 1024 * 1024
MIXER_VMEM_LIMIT_BYTES = 61 * 1024 * 1024

F32 = jnp.float32
BF16 = jnp.bfloat16


def _dot(a, b):
    return jnp.dot(a, b, preferred_element_type=F32)


def _dot_nt(a, b):
    return lax.dot_general(a, b, (((1,), (1,)), ((), ())), preferred_element_type=F32)


def _dot_tn(a, b):
    return lax.dot_general(a, b, (((0,), (0,)), ((), ())), preferred_element_type=F32)


def _sigmoid(x):
    return 1.0 / (1.0 + jnp.exp(-x))


def _rms(x, g):
    return x * lax.rsqrt(jnp.mean(x * x, axis=-1, keepdims=True) + EPS) * g


def _const_spec(shape):
    nd = len(shape)
    return pl.BlockSpec(shape, lambda *_: (0,) * nd, pipeline_mode=pl.Buffered(1))


def _params(*sem, vmem=VMEM_LIMIT_BYTES):
    return pltpu.CompilerParams(dimension_semantics=sem, vmem_limit_bytes=vmem)


def _ada_kernel(cctx_ref, c_ref, w_ref, b_ref, o_ref, rows_ref):
    rows_ref[...] = jnp.zeros_like(rows_ref)
    rows_ref[0:1, :] = cctx_ref[...]
    rows_ref[1:1 + c_ref.shape[0], :] = c_ref[...]
    c = rows_ref[...]
    s = (c * _sigmoid(c)).astype(BF16)
    o_ref[:, 0, :] = _dot(s, w_ref[...].astype(BF16)) + b_ref[...]


def _ada(c_ctx, c, w_ada, b_ada):
    n = w_ada.shape[1]
    assert 1 + c.shape[0] <= MOD_ROWS
    return pl.pallas_call(
        _ada_kernel,
        out_shape=jax.ShapeDtypeStruct((MOD_ROWS, 1, n), F32),
        grid=(n // ADA_TILE,),
        in_specs=[pl.BlockSpec(c_ctx.shape, lambda j: (0, 0)),
                  pl.BlockSpec(c.shape, lambda j: (0, 0)),
                  pl.BlockSpec((D_MODEL, ADA_TILE), lambda j: (0, j)),
                  pl.BlockSpec((1, ADA_TILE), lambda j: (0, j))],
        out_specs=pl.BlockSpec((MOD_ROWS, 1, ADA_TILE), lambda j: (0, 0, j)),
        scratch_shapes=[pltpu.VMEM((MOD_ROWS, D_MODEL), F32)],
        compiler_params=_params("arbitrary"),
        name="ada",
    )(c_ctx, c, w_ada, b_ada)


W_BLOCK = 512
HEAD_COLS = 2 * KEY_DIM + 2 * VAL_DIM
TAIL_OFF = HEAD_COLS + 2 * GATE_RANK
TAIL_SHIFT = TAIL_OFF % MXU_LANES_V7X
N_HEAD_BLOCKS = HEAD_COLS // W_BLOCK
N_W_BLOCKS = N_HEAD_BLOCKS + 5 * CONV_DIM // W_BLOCK


def _inproj_kernel(xa_ref, xb_ref, mod_ref, n1_ref, wa_ref, wb_ref, wgf_ref, wgb_ref, bgf_ref, bgb_ref,
                   q_ref, k_ref, v_ref, og_ref, p_ref, cbg_ref, g_ref, w_scr, wcode_scr, wgk_scr,
                   *, tiles_a, seg):
    step = pl.program_id(0)

    @pl.when(step < N_HEAD_BLOCKS)
    def _():
        w_scr[step] = wa_ref[...].T.astype(BF16)
        wgk_scr[...] = jnp.zeros_like(wgk_scr)
        wgk_scr[0:GATE_RANK, 0:KEY_DIM] = wgf_ref[...].astype(BF16)
        wgk_scr[GATE_RANK:2 * GATE_RANK, KEY_DIM:] = wgb_ref[...].astype(BF16)
        lane = lax.broadcasted_iota(jnp.int32, (D_MODEL, MXU_LANES_V7X), 1)
        wcode_scr[...] = jnp.where(lane < 2 * GATE_RANK, wb_ref[...].T, 0.0).astype(BF16)

    @pl.when(jnp.logical_and(step >= N_HEAD_BLOCKS, step < N_W_BLOCKS))
    def _():
        rows = jnp.concatenate([wa_ref[TAIL_SHIFT:, :], wb_ref[:TAIL_SHIFT, :]], axis=0)
        w_scr[step] = rows.T.astype(BF16)

    @pl.when(step >= N_W_BLOCKS)
    def _():
        first_group = step - N_W_BLOCKS < tiles_a
        mod = mod_ref[...]
        sh1 = mod[:, 0:D_MODEL]
        sc1 = mod[:, D_MODEL:2 * D_MODEL]
        x = jnp.where(first_group, xa_ref[...], xb_ref[...])
        xn = (_rms(x, n1_ref[...]) * (1.0 + sc1) + sh1).astype(BF16)
        halves = range(CONV_DIM // W_BLOCK)
        head = lambda blk: _dot(xn, w_scr[blk])
        tail = lambda grp, j: _dot(xn, w_scr[N_HEAD_BLOCKS + grp * len(halves) + j])
        cols = lambda j: slice(j * W_BLOCK, (j + 1) * W_BLOCK)

        z = (_dot(_dot(xn, wcode_scr[...]).astype(BF16), wgk_scr[...])
             + jnp.concatenate([bgf_ref[...], bgb_ref[...]], axis=1))
        g = (jnp.minimum(z, 0.0) - jnp.log(1.0 + jnp.exp(-jnp.abs(z)))) * (LOG2_E / GATE_NORMALIZER)
        g_hi = g.astype(BF16)
        g_lo = (g - g_hi.astype(F32)).astype(BF16)

        q_ref[...] = (head(0) * (HEAD_K ** -0.5)).astype(BF16)
        k_ref[...] = head(1).astype(BF16)
        for j in halves:
            v_ref[:, cols(j)] = head(2 + j).astype(BF16)

        r = lax.broadcasted_iota(jnp.int32, (seg, seg), 0)
        c = lax.broadcasted_iota(jnp.int32, (seg, seg), 1)
        lower = jnp.where(c <= r, 1.0, 0.0).astype(BF16)
        upper = jnp.where(c >= r, 1.0, 0.0).astype(BF16)
        for s0 in range(0, g.shape[0], seg):
            rows = slice(s0, s0 + seg)
            for tri, cs in ((lower, slice(0, KEY_DIM)), (upper, slice(KEY_DIM, 2 * KEY_DIM))):
                terms = jnp.concatenate([g_hi[rows, cs], g_lo[rows, cs]], axis=0)
                g_ref[rows, cs] = _dot(jnp.concatenate([tri, tri], axis=1), terms)

        for j in halves:
            g_out = head(2 + len(halves) + j)
            og_ref[:, cols(j)] = (g_out * _sigmoid(g_out) * _sigmoid(tail(3, j))).astype(BF16)
            p_ref[:, cols(j)] = (tail(1, j) * tail(2, j)).astype(BF16)
            cbg_ref[:, cols(j)] = (_sigmoid(tail(4, j)) * tail(0, j)).astype(BF16)


def _inproj(xa, xb, mod3, n1g, w_in_t, wgf, wgb, bgf, bgb, *, tm, seg, mod_row):
    tiles_a, tiles_b = xa.shape[0] // tm, xb.shape[0] // tm
    n = xa.shape[0] + xb.shape[0]
    assert seg & (seg - 1) == 0 and tm % seg == 0
    assert KEY_DIM == W_BLOCK and HEAD_COLS % W_BLOCK == 0 and W_BLOCK % MXU_LANES_V7X == 0
    tile = lambda i: jnp.maximum(i - N_W_BLOCKS, 0)
    row = lambda i: (tile(i), 0)
    lanes_per_block = W_BLOCK // MXU_LANES_V7X
    wide = lambda w, dt: (jax.ShapeDtypeStruct((n, w), dt), pl.BlockSpec((tm, w), row))
    outs = [wide(KEY_DIM, BF16), wide(KEY_DIM, BF16), wide(VAL_DIM, BF16), wide(VAL_DIM, BF16),
            wide(CONV_DIM, BF16), wide(CONV_DIM, BF16), wide(2 * KEY_DIM, F32)]
    return pl.pallas_call(
        functools.partial(_inproj_kernel, tiles_a=tiles_a, seg=seg),
        out_shape=[o[0] for o in outs],
        grid=(N_W_BLOCKS + tiles_a + tiles_b,),
        in_specs=[pl.BlockSpec((tm, D_MODEL), lambda i: (jnp.minimum(tile(i), tiles_a - 1), 0)),
                  pl.BlockSpec((tm, D_MODEL), lambda i: (jnp.maximum(tile(i) - tiles_a, 0), 0)),
                  pl.BlockSpec((None, 1, N_MOD * D_MODEL), lambda i: (mod_row(tile(i)), 0, 0)),
                  _const_spec(n1g.shape),
                  pl.BlockSpec((W_BLOCK, D_MODEL), lambda i: (jnp.minimum(i, N_W_BLOCKS - 1), 0)),
                  pl.BlockSpec((MXU_LANES_V7X, D_MODEL),
                               lambda i: (jnp.where(i < N_HEAD_BLOCKS, HEAD_COLS // MXU_LANES_V7X,
                                                    (jnp.minimum(i, N_W_BLOCKS - 1) + 1) * lanes_per_block), 0)),
                  _const_spec(wgf.shape), _const_spec(wgb.shape), _const_spec(bgf.shape), _const_spec(bgb.shape)],
        out_specs=[o[1] for o in outs],
        scratch_shapes=[pltpu.VMEM((N_W_BLOCKS, D_MODEL, W_BLOCK), BF16),
                        pltpu.VMEM((D_MODEL, MXU_LANES_V7X), BF16),
                        pltpu.VMEM((MXU_LANES_V7X, 2 * KEY_DIM), BF16)],
        compiler_params=_params("arbitrary"),
        name="in_proj",
    )(xa, xb, mod3, n1g, w_in_t, w_in_t, wgf, wgb, bgf, bgb)


def _cat(parts, axis=0):
    return parts[0] if len(parts) == 1 else jnp.concatenate(parts, axis=axis)


def _join_prefix(b, seg):
    parts = [b[:seg]]
    for c in range(1, b.shape[0] // seg):
        parts.append(b[c * seg:(c + 1) * seg] + parts[-1][seg - 1:seg, :])
    return _cat(parts)


def _join_suffix(b, seg):
    n = b.shape[0] // seg
    parts = [b[(n - 1) * seg:]]
    for c in range(n - 2, -1, -1):
        parts.insert(0, b[c * seg:(c + 1) * seg] + parts[0][0:1, :])
    return _cat(parts)


def _diag_args(b, off):
    return _cat([b[c:c + DIAG_BLOCK] - b[c + off:c + off + 1, :] for c in range(0, b.shape[0], DIAG_BLOCK)])


def _pair_args(bf, sb, s):
    lhs, rhs = [], []
    for e0 in range(0, bf.shape[0], 2 * s):
        o0 = e0 + s
        rf = bf[o0 - 1:o0, :]
        rb = sb[o0:o0 + 1, :]
        lhs += [sb[e0:o0] - rb, bf[o0:o0 + s] - rf]
        rhs += [rf - bf[e0:o0], rb - sb[o0:o0 + s]]
    return _cat(lhs), _cat(rhs)


def _mixer_kernel(*refs, seq, heads, seg, has_state_in, has_state_out, has_order_token):
    (q_ref, k_ref, v_ref, gf_ref, gb_ref, og_ref, p_ref, cbg_ref, cw_ref, gn_ref, wo_ref, x_ref,
     mod_ref) = refs[:13]
    refs = refs[13:]
    if has_state_in:
        s0f_ref, s0b_ref = refs[:2]
        refs = refs[2:]
    if has_order_token:
        refs = refs[1:]
    x1_ref = refs[0]
    refs = refs[1:]
    if has_state_out:
        sf_ref, sb_ref = refs[:2]
        refs = refs[2:]
    a_scr = refs[0]
    hsteps = N_HEADS // heads
    half = SUPER_BLOCK // 2
    n_seq = x_ref.shape[0] // seq

    rows_v = lax.broadcasted_iota(jnp.int32, (seq, HEAD_V), 0)
    r_loc = lax.broadcasted_iota(jnp.int32, (DIAG_BLOCK, half), 0)
    lane = lax.broadcasted_iota(jnp.int32, (DIAG_BLOCK, half), 1)
    ones = jnp.ones((HEAD_V, half), BF16)
    scale = lambda t, e: (t.astype(F32) * e).astype(BF16)

    mixes = [[] for _ in range(n_seq)]
    for si, h in [(si, h) for h in range(heads) for si in range(n_seq)]:
        rq = slice(si * seq, (si + 1) * seq)
        a_ref = a_scr.at[(si * heads + h) % a_scr.shape[0]]
        ks = slice(h * HEAD_K, (h + 1) * HEAD_K)
        vs = slice(h * HEAD_V, (h + 1) * HEAD_V)
        q = q_ref[rq, ks]
        k = k_ref[rq, ks]
        v = v_ref[rq, vs]
        bf = _join_prefix(gf_ref[rq, ks], seg)
        sb = _join_suffix(gb_ref[rq, ks], seg)

        e0f = jnp.exp2(_diag_args(bf, DIAG_BLOCK // 2))
        e0b = jnp.exp2(_diag_args(sb, DIAG_BLOCK // 2 - 1))
        q0f, k0f = scale(q, e0f), scale(k, 1.0 / e0f)
        q0b, k0b = scale(q, e0b), scale(k, 1.0 / e0b)

        lhs_levels, rhs_levels = [], []
        s = DIAG_BLOCK
        while s < seq:
            la, ra = _pair_args(bf, sb, s)
            lhs_levels.append(scale(q, jnp.exp2(la)))
            rhs_levels.append(scale(k, jnp.exp2(ra)))
            s *= 2

        for base in range(0, seq, SUPER_BLOCK):
            rs = slice(base, base + SUPER_BLOCK)
            m0f = _dot_nt(q0f[rs], k0f[rs])
            m0b = _dot_nt(q0b[rs], k0b[rs])
            m1 = _dot_nt(lhs_levels[0][rs], rhs_levels[0][rs])
            m2 = _dot_nt(lhs_levels[1][rs], rhs_levels[1][rs])
            for bi in range(SUPER_BLOCK // DIAG_BLOCK):
                rr = slice(bi * DIAG_BLOCK, (bi + 1) * DIAG_BLOCK)
                for lt in range(2):
                    cc = slice(lt * half, (lt + 1) * half)
                    if lt == bi // 2:
                        c_loc = lane - DIAG_BLOCK * (bi % 2)
                        in_diag = (lane >= DIAG_BLOCK) if bi % 2 else (lane < DIAG_BLOCK)
                        diag = (jnp.where(c_loc <= r_loc, m0f[rr, cc], 0.0)
                                + jnp.where(c_loc >= r_loc, m0b[rr, cc], 0.0))
                        piece = jnp.where(in_diag, diag, m1[rr, cc])
                    else:
                        piece = m2[rr, cc]
                    a_ref[base + bi * DIAG_BLOCK:base + (bi + 1) * DIAG_BLOCK,
                          base + lt * half:base + (lt + 1) * half] = piece.astype(BF16)
        s = SUPER_BLOCK
        lev = 2
        while s < seq:
            lhs, rhs = lhs_levels[lev], rhs_levels[lev]
            for e0 in range(0, seq, 2 * s):
                ev = slice(e0, e0 + s)
                od = slice(e0 + s, e0 + 2 * s)
                a_ref[od, ev] = _dot_nt(lhs[od], rhs[ev]).astype(BF16)
                a_ref[ev, od] = _dot_nt(lhs[ev], rhs[od]).astype(BF16)
            s *= 2
            lev += 1

        o = _dot(a_ref[...], v)
        if has_state_in:
            qi = jnp.concatenate([scale(q, jnp.exp2(bf)), scale(q, jnp.exp2(sb))], axis=1)
            s0 = jnp.concatenate([s0f_ref[si, h].astype(BF16), s0b_ref[si, h].astype(BF16)], axis=0)
            o = o + _dot(qi, s0)
        if has_state_out:
            sf_ref[si, h] = _dot_tn(scale(k, jnp.exp2(bf[seq - 1:seq, :] - bf)), v)
            sb_ref[si, h] = _dot_tn(scale(k, jnp.exp2(sb[0:1, :] - sb)), v)

        inv = lax.rsqrt(_dot((o * o).astype(BF16), ones) * (1.0 / HEAD_V) + EPS)
        o = jnp.concatenate([o[:, :half] * inv, o[:, half:] * inv], axis=1) * gn_ref[...]
        p = p_ref[rq, vs].astype(F32)
        conv = (jnp.where(rows_v >= 1, pltpu.roll(p, 1, axis=0), 0.0) * cw_ref[0:1, vs] + p * cw_ref[1:2, vs]
                + jnp.where(rows_v < seq - 1, pltpu.roll(p, seq - 1, axis=0), 0.0) * cw_ref[2:3, vs])
        mixes[si].append(og_ref[rq, vs] * o.astype(BF16) + cbg_ref[rq, vs] * conv.astype(BF16))

    contrib = _dot(_cat([_cat(m, axis=1) for m in mixes], axis=0), wo_ref[...])
    ga1 = mod_ref[:, 2 * D_MODEL:3 * D_MODEL]
    if hsteps == 1:
        x1_ref[...] = x_ref[...] + ga1 * contrib
    else:
        hstep = pl.program_id(1)

        @pl.when(hstep == 0)
        def _():
            x1_ref[...] = x_ref[...] + ga1 * contrib

        @pl.when(hstep != 0)
        def _():
            x1_ref[...] += ga1 * contrib


def _mixer(x, mod3, proj, conv_w, gn, wo, states, *, seq, heads, seqs_per_step, seg, first_row, mod_row,
           has_state_out, run_after=None):
    q, k, v, og, p, cbg, g = proj
    n = x.shape[0]
    rows = seqs_per_step * seq
    hsteps = N_HEADS // heads
    has_state_in = states is not None
    assert seq % SUPER_BLOCK == 0 and seq % seg == 0 and first_row % rows == 0 and n % rows == 0
    b0 = first_row // rows
    a_buffers = max(1, min(seqs_per_step * heads, A_SCRATCH_BYTES // (2 * seq * seq)))
    col = lambda b, h: (b0 + b, h)
    st_block = (seqs_per_step, None, heads, HEAD_K, HEAD_V)
    st_map = lambda b, h: (b, 0, h, 0, 0)
    in_specs = [pl.BlockSpec((rows, heads * HEAD_K), col),
                pl.BlockSpec((rows, heads * HEAD_K), col),
                pl.BlockSpec((rows, heads * HEAD_V), col),
                pl.BlockSpec((rows, heads * HEAD_K), col),
                pl.BlockSpec((rows, heads * HEAD_K), lambda b, h: (b0 + b, hsteps + h)),
                pl.BlockSpec((rows, heads * HEAD_V), col),
                pl.BlockSpec((rows, heads * HEAD_V), col),
                pl.BlockSpec((rows, heads * HEAD_V), col),
                pl.BlockSpec((3, heads * HEAD_V), lambda b, h: (0, h)),
                pl.BlockSpec((1, HEAD_V), lambda b, h: (0, 0)),
                (pl.BlockSpec((heads * HEAD_V, D_MODEL), lambda b, h: (h, 0)) if hsteps > 1
                 else _const_spec((heads * HEAD_V, D_MODEL))),
                pl.BlockSpec((rows, D_MODEL), lambda b, h: (b, 0)),
                pl.BlockSpec((None, 1, N_MOD * D_MODEL), lambda b, h: (mod_row(b), 0, 0))]
    args = [q, k, v, g, g, og, p, cbg, conv_w, gn, wo, x, mod3]
    if has_state_in:
        in_specs += [pl.BlockSpec(st_block, st_map)] * 2
        args += list(states)
    if run_after is not None:
        in_specs.append(pl.BlockSpec(memory_space=pl.ANY))
        args.append(run_after)
    out_shape = [jax.ShapeDtypeStruct((n, D_MODEL), F32)]
    out_specs = [pl.BlockSpec((rows, D_MODEL), lambda b, h: (b, 0))]
    if has_state_out:
        out_shape += [jax.ShapeDtypeStruct((n // seq, 1, N_HEADS, HEAD_K, HEAD_V), F32)] * 2
        out_specs += [pl.BlockSpec(st_block, st_map)] * 2
    return pl.pallas_call(
        functools.partial(_mixer_kernel, seq=seq, heads=heads, seg=seg, has_state_in=has_state_in,
                          has_state_out=has_state_out, has_order_token=run_after is not None),
        out_shape=out_shape,
        grid=(n // rows, hsteps),
        in_specs=in_specs,
        out_specs=out_specs,
        scratch_shapes=[pltpu.VMEM((a_buffers, seq, seq), BF16)],
        compiler_params=_params("arbitrary", "arbitrary", vmem=MIXER_VMEM_LIMIT_BYTES),
        name="mixer",
    )(*args)


N_FFN_TILES = D_FF // FFN_TILE


def _ffn_kernel(x1_ref, mod_ref, n2_ref, wu_ref, wg_ref, cw_ref, cb_ref, wd_ref, nf_ref, o_ref, acc_ref,
                *, seq, width):
    cols = lambda f: slice(f * FFN_TILE, (f + 1) * FFN_TILE)
    _ffn_tile(x1_ref, mod_ref, n2_ref, cw_ref, cb_ref, nf_ref, o_ref, acc_ref, seq, width,
              lambda f: wu_ref[:, cols(f)], lambda f: wg_ref[:, cols(f)], lambda f: wd_ref[cols(f), :])


def _ffn_staging_kernel(x1_ref, mod_ref, n2_ref, wu_ref, wg_ref, cw_ref, cb_ref, wd_ref, nf_ref,
                        o_ref, wu_out, wg_out, wd_out, wu_scr, wg_scr, wd_scr, acc_ref, *, seq, width):
    step = pl.program_id(0)

    @pl.when(step < N_FFN_TILES)
    def _():
        for src, scr, out in ((wu_ref, wu_scr, wu_out), (wg_ref, wg_scr, wg_out), (wd_ref, wd_scr, wd_out)):
            w = src[...].astype(BF16)
            scr[step] = w
            out[...] = w

    @pl.when(step >= N_FFN_TILES)
    def _():
        _ffn_tile(x1_ref, mod_ref, n2_ref, cw_ref, cb_ref, nf_ref, o_ref, acc_ref, seq, width,
                  lambda f: wu_scr[f], lambda f: wg_scr[f], lambda f: wd_scr[f])


def _ffn_tile(x1_ref, mod_ref, n2_ref, cw_ref, cb_ref, nf_ref, o_ref, acc_ref, seq, width, wu, wg, wd):
    tm = x1_ref.shape[0]
    mod = mod_ref[...]
    sh2 = mod[:, 3 * D_MODEL:4 * D_MODEL]
    sc2 = mod[:, 4 * D_MODEL:5 * D_MODEL]
    ga2 = mod[:, 5 * D_MODEL:6 * D_MODEL]
    x1 = x1_ref[...]
    xb = (_rms(x1, n2_ref[...]) * (1.0 + sc2) + sh2).astype(BF16)

    rows = lax.broadcasted_iota(jnp.int32, (tm, FFN_TILE), 0)
    col_in_row = rows & (width - 1)
    has_left = col_in_row != 0
    has_right = col_in_row != width - 1
    n_rows = seq // width
    zrow = jnp.zeros((width, FFN_TILE), F32)

    def up_rows(z):
        return jnp.concatenate([zrow, z[:tm - width]], axis=0)

    def down_rows(z):
        return jnp.concatenate([z[width:], zrow], axis=0)

    def up_and_gate(f):
        return _dot(xb, wu(f)), _dot(xb, wg(f))

    ahead = up_and_gate(0)
    for f in range(N_FFN_TILES):
        fs = slice(f * FFN_TILE, (f + 1) * FFN_TILE)
        up, gate = ahead
        if f + 1 < N_FFN_TILES:
            ahead = up_and_gate(f + 1)
        left = jnp.where(has_left, pltpu.roll(up, 1, axis=0), 0.0)
        right = jnp.where(has_right, pltpu.roll(up, tm - 1, axis=0), 0.0)
        w = lambda i, j: cw_ref[3 * i + j:3 * i + j + 1, fs]
        taps = lambda i: left * w(i, 0) + up * w(i, 1) + right * w(i, 2)
        cv = taps(1) + cb_ref[:, fs]
        if n_rows > 1:
            cv = cv + up_rows(taps(0)) + down_rows(taps(2))
        act = (cv * _sigmoid(cv) * gate).astype(BF16)
        part = _dot(act, wd(f))
        if f == 0:
            acc_ref[...] = part
        else:
            acc_ref[...] += part

    o_ref[...] = _rms(x1 + ga2 * acc_ref[...], nf_ref[...])


def _ffn_staging(x1, mod3, n2g, wu, wg, cw, cb, wd, nfg, *, tm, seq, width, mod_row):
    assert width & (width - 1) == 0 and tm % seq == 0 and (seq == width or tm == seq)
    tile = lambda i: jnp.maximum(i - N_FFN_TILES, 0)
    w_step = lambda i: jnp.minimum(i, N_FFN_TILES - 1)
    col_blk = pl.BlockSpec((D_MODEL, FFN_TILE), lambda i: (0, w_step(i)))
    row_blk = pl.BlockSpec((FFN_TILE, D_MODEL), lambda i: (w_step(i), 0))
    return pl.pallas_call(
        functools.partial(_ffn_staging_kernel, seq=seq, width=width),
        out_shape=[jax.ShapeDtypeStruct(x1.shape, F32), jax.ShapeDtypeStruct(wu.shape, BF16),
                   jax.ShapeDtypeStruct(wg.shape, BF16), jax.ShapeDtypeStruct(wd.shape, BF16)],
        grid=(N_FFN_TILES + x1.shape[0] // tm,),
        in_specs=[pl.BlockSpec((tm, D_MODEL), lambda i: (tile(i), 0)),
                  pl.BlockSpec((None, 1, N_MOD * D_MODEL), lambda i: (mod_row(tile(i)), 0, 0)),
                  _const_spec(n2g.shape), col_blk, col_blk,
                  _const_spec(cw.shape), _const_spec(cb.shape), row_blk,
                  _const_spec(nfg.shape)],
        out_specs=[pl.BlockSpec((tm, D_MODEL), lambda i: (tile(i), 0)), col_blk, col_blk, row_blk],
        scratch_shapes=[pltpu.VMEM((N_FFN_TILES, D_MODEL, FFN_TILE), BF16),
                        pltpu.VMEM((N_FFN_TILES, D_MODEL, FFN_TILE), BF16),
                        pltpu.VMEM((N_FFN_TILES, FFN_TILE, D_MODEL), BF16),
                        pltpu.VMEM((tm, D_MODEL), F32)],
        compiler_params=_params("arbitrary"),
        name="ffn_staging",
    )(x1, mod3, n2g, wu, wg, cw, cb, wd, nfg)


def _ffn(x1, mod3, n2g, wu, wg, cw, cb, wd, nfg, *, tm, seq, width, mod_row):
    assert width & (width - 1) == 0 and tm % seq == 0 and (seq == width or tm == seq)
    return pl.pallas_call(
        functools.partial(_ffn_kernel, seq=seq, width=width),
        out_shape=jax.ShapeDtypeStruct(x1.shape, F32),
        grid=(x1.shape[0] // tm,),
        in_specs=[pl.BlockSpec((tm, D_MODEL), lambda i: (i, 0)),
                  pl.BlockSpec((None, 1, N_MOD * D_MODEL), lambda i: (mod_row(i), 0, 0)),
                  _const_spec(n2g.shape), _const_spec(wu.shape), _const_spec(wg.shape),
                  _const_spec(cw.shape), _const_spec(cb.shape), _const_spec(wd.shape),
                  _const_spec(nfg.shape)],
        out_specs=pl.BlockSpec((tm, D_MODEL), lambda i: (i, 0)),
        scratch_shapes=[pltpu.VMEM((tm, D_MODEL), F32)],
        compiler_params=_params("arbitrary"),
        name="ffn",
    )(x1, mod3, n2g, wu, wg, cw, cb, wd, nfg)


def kernel(x_prompt, x_sample, c, state_gla_fwd, state_gla_bwd, c_ctx, w_ada, b_ada, norm1_g, w_in, w_gk_f,
           b_gk_f, w_gk_b, b_gk_b, gla_norm_g, conv_mix_w, w_out, norm2_g, ffn_w_up, ffn_w_gate,
           ffn_conv_w, ffn_conv_b, ffn_w_down, normf_g):
    bp, lp, d = x_prompt.shape
    bs, ls, _ = x_sample.shape
    assert w_ada.shape[0] == 1 and d == D_MODEL and bs + 1 <= MOD_ROWS

    mod3 = _ada(c_ctx[None, :], c, w_ada[0], b_ada)

    assert IN_SPLITS == (KEY_DIM, KEY_DIM, VAL_DIM, VAL_DIM, GATE_RANK, GATE_RANK) + (CONV_DIM,) * 5
    wo = w_out[0].astype(BF16)
    cw9 = ffn_conv_w[0].reshape(9, D_FF)
    cb = ffn_conv_b
    n1g, n2g, nfg, gn = norm1_g, norm2_g, normf_g[None, :], gla_norm_g

    xp, xs = x_prompt.reshape(-1, d), x_sample.reshape(-1, d)
    tiles_p = xp.shape[0] // INPROJ_ROWS
    seg = min(lp, ls, INPROJ_ROWS)
    proj = _inproj(xp, xs, mod3, n1g, w_in[0].T, w_gk_f[0], w_gk_b[0], b_gk_f, b_gk_b, tm=INPROJ_ROWS, seg=seg,
                   mod_row=lambda t: jnp.where(t < tiles_p, 0, 1 + ((t - tiles_p) * INPROJ_ROWS) // ls))

    mix = functools.partial(_mixer, mod3=mod3, proj=proj, conv_w=conv_mix_w[0], gn=gn, wo=wo, seg=seg)
    x1p, new_f, new_b = mix(xp, states=None, seq=lp, heads=N_HEADS, seqs_per_step=PROMPT_SEQS_PER_STEP,
                            first_row=0, mod_row=lambda b: 0, has_state_out=True)
    y_prompt, wu, wg, wd = _ffn_staging(x1p, mod3, n2g, ffn_w_up[0], ffn_w_gate[0], cw9, cb, ffn_w_down[0], nfg,
                                        tm=FFN_ROWS // 4, seq=lp, width=lp, mod_row=lambda i: 0)
    x1s, = mix(xs, states=(state_gla_fwd, state_gla_bwd), seq=ls, heads=N_HEADS, seqs_per_step=1,
               first_row=xp.shape[0], mod_row=lambda b: 1 + b, has_state_out=False, run_after=wd)
    y_sample = _ffn(x1s, mod3, n2g, wu, wg, cw9, cb, wd, nfg, tm=FFN_ROWS, seq=ls, width=GRID_W,
                    mod_row=lambda i: 1 + (i * FFN_ROWS) // ls)
    return y_prompt.reshape(x_prompt.shape), y_sample.reshape(x_sample.shape), new_f, new_b
```

```python
import functools

import jax
import jax.numpy as jnp
from jax import lax
from jax.experimental import pallas as pl
from jax.experimental.pallas import tpu as pltpu

D_MODEL = 1024
N_HEADS = 4
HEAD_K = 128
HEAD_V = 256
KEY_DIM = N_HEADS * HEAD_K
VAL_DIM = N_HEADS * HEAD_V
GATE_RANK = 16
GATE_NORMALIZER = 16.0
LOG2_E = 1.4426950408889634
CONV_DIM = D_MODEL
D_FF = 2816
N_MOD = 6
EPS = 1e-6
GRID_W = 64
IN_SPLITS = (KEY_DIM, KEY_DIM, VAL_DIM, VAL_DIM, GATE_RANK, GATE_RANK,
             CONV_DIM, CONV_DIM, CONV_DIM, VAL_DIM, CONV_DIM)

DIAG_BLOCK = 64
SUPER_BLOCK = 256
MXU_LANES_V7X = 128
FFN_TILE = 256
FFN_ROWS = 1024
PROMPT_FFN_ROWS = 256
INPROJ_ROWS = 512
PROMPT_SEQS_PER_STEP = 2
ADA_TILE = 1024
MOD_ROWS = 8
VMEM_LIMIT_BYTES = 56 * 1024 * 1024
A_SCRATCH_BYTES = 4 * 1024 * 1024
MIXER_VMEM_LIMIT_BYTES = 61 * 1024 * 1024

F32 = jnp.float32
BF16 = jnp.bfloat16


def _dot(a, b):
    return jnp.dot(a, b, preferred_element_type=F32)


def _dot_nt(a, b):
    return lax.dot_general(a, b, (((1,), (1,)), ((), ())), preferred_element_type=F32)


def _dot_tn(a, b):
    return lax.dot_general(a, b, (((0,), (0,)), ((), ())), preferred_element_type=F32)


def _sigmoid(x):
    return 1.0 / (1.0 + jnp.exp(-x))


def _rms(x, g):
    return x * lax.rsqrt(jnp.mean(x * x, axis=-1, keepdims=True) + EPS) * g


def _const_spec(shape):
    nd = len(shape)
    return pl.BlockSpec(shape, lambda *_: (0,) * nd, pipeline_mode=pl.Buffered(1))


def _params(*sem, vmem=VMEM_LIMIT_BYTES):
    return pltpu.CompilerParams(dimension_semantics=sem, vmem_limit_bytes=vmem)


def _ada_kernel(cctx_ref, c_ref, w_ref, b_ref, o_ref, rows_ref):
    rows_ref[...] = jnp.zeros_like(rows_ref)
    rows_ref[0:1, :] = cctx_ref[...]
    rows_ref[1:1 + c_ref.shape[0], :] = c_ref[...]
    c = rows_ref[...]
    s = (c * _sigmoid(c)).astype(BF16)
    o_ref[:, 0, :] = _dot(s, w_ref[...].astype(BF16)) + b_ref[...]


def _ada(c_ctx, c, w_ada, b_ada):
    n = w_ada.shape[1]
    assert 1 + c.shape[0] <= MOD_ROWS
    return pl.pallas_call(
        _ada_kernel,
        out_shape=jax.ShapeDtypeStruct((MOD_ROWS, 1, n), F32),
        grid=(n // ADA_TILE,),
        in_specs=[pl.BlockSpec(c_ctx.shape, lambda j: (0, 0)),
                  pl.BlockSpec(c.shape, lambda j: (0, 0)),
                  pl.BlockSpec((D_MODEL, ADA_TILE), lambda j: (0, j)),
                  pl.BlockSpec((1, ADA_TILE), lambda j: (0, j))],
        out_specs=pl.BlockSpec((MOD_ROWS, 1, ADA_TILE), lambda j: (0, 0, j)),
        scratch_shapes=[pltpu.VMEM((MOD_ROWS, D_MODEL), F32)],
        compiler_params=_params("arbitrary"),
        name="ada",
    )(c_ctx, c, w_ada, b_ada)


W_BLOCK = 512
HEAD_COLS = 2 * KEY_DIM + 2 * VAL_DIM
TAIL_OFF = HEAD_COLS + 2 * GATE_RANK
TAIL_SHIFT = TAIL_OFF % MXU_LANES_V7X
N_HEAD_BLOCKS = HEAD_COLS // W_BLOCK
N_W_BLOCKS = N_HEAD_BLOCKS + 5 * CONV_DIM // W_BLOCK


def _inproj_kernel(xa_ref, xb_ref, mod_ref, n1_ref, wa_ref, wb_ref, wgf_ref, wgb_ref, bgf_ref, bgb_ref,
                   q_ref, k_ref, v_ref, og_ref, p_ref, cbg_ref, g_ref, w_scr, wcode_scr, wgk_scr,
                   *, tiles_a, seg):
    step = pl.program_id(0)

    @pl.when(step < N_HEAD_BLOCKS)
    def _():
        w_scr[step] = wa_ref[...].T.astype(BF16)
        wgk_scr[...] = jnp.zeros_like(wgk_scr)
        wgk_scr[0:GATE_RANK, 0:KEY_DIM] = wgf_ref[...].astype(BF16)
        wgk_scr[GATE_RANK:2 * GATE_RANK, KEY_DIM:] = wgb_ref[...].astype(BF16)
        lane = lax.broadcasted_iota(jnp.int32, (D_MODEL, MXU_LANES_V7X), 1)
        wcode_scr[...] = jnp.where(lane < 2 * GATE_RANK, wb_ref[...].T, 0.0).astype(BF16)

    @pl.when(jnp.logical_and(step >= N_HEAD_BLOCKS, step < N_W_BLOCKS))
    def _():
        rows = jnp.concatenate([wa_ref[TAIL_SHIFT:, :], wb_ref[:TAIL_SHIFT, :]], axis=0)
        w_scr[step] = rows.T.astype(BF16)

    @pl.when(step >= N_W_BLOCKS)
    def _():
        first_group = step - N_W_BLOCKS < tiles_a
        mod = mod_ref[...]
        sh1 = mod[:, 0:D_MODEL]
        sc1 = mod[:, D_MODEL:2 * D_MODEL]
        x = jnp.where(first_group, xa_ref[...], xb_ref[...])
        xn = (_rms(x, n1_ref[...]) * (1.0 + sc1) + sh1).astype(BF16)
        halves = range(CONV_DIM // W_BLOCK)
        head = lambda blk: _dot(xn, w_scr[blk])
        tail = lambda grp, j: _dot(xn, w_scr[N_HEAD_BLOCKS + grp * len(halves) + j])
        cols = lambda j: slice(j * W_BLOCK, (j + 1) * W_BLOCK)

        z = (_dot(_dot(xn, wcode_scr[...]).astype(BF16), wgk_scr[...])
             + jnp.concatenate([bgf_ref[...], bgb_ref[...]], axis=1))
        g = (jnp.minimum(z, 0.0) - jnp.log(1.0 + jnp.exp(-jnp.abs(z)))) * (LOG2_E / GATE_NORMALIZER)
        g_hi = g.astype(BF16)
        g_lo = (g - g_hi.astype(F32)).astype(BF16)

        q_ref[...] = (head(0) * (HEAD_K ** -0.5)).astype(BF16)
        k_ref[...] = head(1).astype(BF16)
        for j in halves:
            v_ref[:, cols(j)] = head(2 + j).astype(BF16)

        r = lax.broadcasted_iota(jnp.int32, (seg, seg), 0)
        c = lax.broadcasted_iota(jnp.int32, (seg, seg), 1)
        lower = jnp.where(c <= r, 1.0, 0.0).astype(BF16)
        upper = jnp.where(c >= r, 1.0, 0.0).astype(BF16)
        for s0 in range(0, g.shape[0], seg):
            rows = slice(s0, s0 + seg)
            for tri, cs in ((lower, slice(0, KEY_DIM)), (upper, slice(KEY_DIM, 2 * KEY_DIM))):
                terms = jnp.concatenate([g_hi[rows, cs], g_lo[rows, cs]], axis=0)
                g_ref[rows, cs] = _dot(jnp.concatenate([tri, tri], axis=1), terms)

        for j in halves:
            g_out = head(2 + len(halves) + j)
            og_ref[:, cols(j)] = (g_out * _sigmoid(g_out) * _sigmoid(tail(3, j))).astype(BF16)
            p_ref[:, cols(j)] = (tail(1, j) * tail(2, j)).astype(BF16)
            cbg_ref[:, cols(j)] = (_sigmoid(tail(4, j)) * tail(0, j)).astype(BF16)


def _inproj(xa, xb, mod3, n1g, w_in_t, wgf, wgb, bgf, bgb, *, tm, seg, mod_row):
    tiles_a, tiles_b = xa.shape[0] // tm, xb.shape[0] // tm
    n = xa.shape[0] + xb.shape[0]
    assert seg & (seg - 1) == 0 and tm % seg == 0
    assert KEY_DIM == W_BLOCK and HEAD_COLS % W_BLOCK == 0 and W_BLOCK % MXU_LANES_V7X == 0
    tile = lambda i: jnp.maximum(i - N_W_BLOCKS, 0)
    row = lambda i: (tile(i), 0)
    lanes_per_block = W_BLOCK // MXU_LANES_V7X
    wide = lambda w, dt: (jax.ShapeDtypeStruct((n, w), dt), pl.BlockSpec((tm, w), row))
    outs = [wide(KEY_DIM, BF16), wide(KEY_DIM, BF16), wide(VAL_DIM, BF16), wide(VAL_DIM, BF16),
            wide(CONV_DIM, BF16), wide(CONV_DIM, BF16), wide(2 * KEY_DIM, F32)]
    return pl.pallas_call(
        functools.partial(_inproj_kernel, tiles_a=tiles_a, seg=seg),
        out_shape=[o[0] for o in outs],
        grid=(N_W_BLOCKS + tiles_a + tiles_b,),
        in_specs=[pl.BlockSpec((tm, D_MODEL), lambda i: (jnp.minimum(tile(i), tiles_a - 1), 0)),
                  pl.BlockSpec((tm, D_MODEL), lambda i: (jnp.maximum(tile(i) - tiles_a, 0), 0)),
                  pl.BlockSpec((None, 1, N_MOD * D_MODEL), lambda i: (mod_row(tile(i)), 0, 0)),
                  _const_spec(n1g.shape),
                  pl.BlockSpec((W_BLOCK, D_MODEL), lambda i: (jnp.minimum(i, N_W_BLOCKS - 1), 0)),
                  pl.BlockSpec((MXU_LANES_V7X, D_MODEL),
                               lambda i: (jnp.where(i < N_HEAD_BLOCKS, HEAD_COLS // MXU_LANES_V7X,
                                                    (jnp.minimum(i, N_W_BLOCKS - 1) + 1) * lanes_per_block), 0)),
                  _const_spec(wgf.shape), _const_spec(wgb.shape), _const_spec(bgf.shape), _const_spec(bgb.shape)],
        out_specs=[o[1] for o in outs],
        scratch_shapes=[pltpu.VMEM((N_W_BLOCKS, D_MODEL, W_BLOCK), BF16),
                        pltpu.VMEM((D_MODEL, MXU_LANES_V7X), BF16),
                        pltpu.VMEM((MXU_LANES_V7X, 2 * KEY_DIM), BF16)],
        compiler_params=_params("arbitrary"),
        name="in_proj",
    )(xa, xb, mod3, n1g, w_in_t, w_in_t, wgf, wgb, bgf, bgb)


def _cat(parts, axis=0):
    return parts[0] if len(parts) == 1 else jnp.concatenate(parts, axis=axis)


def _join_prefix(b, seg):
    parts = [b[:seg]]
    for c in range(1, b.shape[0] // seg):
        parts.append(b[c * seg:(c + 1) * seg] + parts[-1][seg - 1:seg, :])
    return _cat(parts)


def _join_suffix(b, seg):
    n = b.shape[0] // seg
    parts = [b[(n - 1) * seg:]]
    for c in range(n - 2, -1, -1):
        parts.insert(0, b[c * seg:(c + 1) * seg] + parts[0][0:1, :])
    return _cat(parts)


def _diag_args(b, off):
    return _cat([b[c:c + DIAG_BLOCK] - b[c + off:c + off + 1, :] for c in range(0, b.shape[0], DIAG_BLOCK)])


def _pair_args(bf, sb, s):
    lhs, rhs = [], []
    for e0 in range(0, bf.shape[0], 2 * s):
        o0 = e0 + s
        rf = bf[o0 - 1:o0, :]
        rb = sb[o0:o0 + 1, :]
        lhs += [sb[e0:o0] - rb, bf[o0:o0 + s] - rf]
        rhs += [rf - bf[e0:o0], rb - sb[o0:o0 + s]]
    return _cat(lhs), _cat(rhs)


def _mixer_kernel(*refs, seq, heads, seg, has_state_in, has_state_out, has_order_token):
    (q_ref, k_ref, v_ref, gf_ref, gb_ref, og_ref, p_ref, cbg_ref, cw_ref, gn_ref, wo_ref, x_ref,
     mod_ref) = refs[:13]
    refs = refs[13:]
    if has_state_in:
        s0f_ref, s0b_ref = refs[:2]
        refs = refs[2:]
    if has_order_token:
        refs = refs[1:]
    x1_ref = refs[0]
    refs = refs[1:]
    if has_state_out:
        sf_ref, sb_ref = refs[:2]
        refs = refs[2:]
    a_scr = refs[0]
    hsteps = N_HEADS // heads
    half = SUPER_BLOCK // 2
    n_seq = x_ref.shape[0] // seq

    rows_v = lax.broadcasted_iota(jnp.int32, (seq, HEAD_V), 0)
    r_loc = lax.broadcasted_iota(jnp.int32, (DIAG_BLOCK, half), 0)
    lane = lax.broadcasted_iota(jnp.int32, (DIAG_BLOCK, half), 1)
    ones = jnp.ones((HEAD_V, half), BF16)
    scale = lambda t, e: (t.astype(F32) * e).astype(BF16)

    mixes = [[] for _ in range(n_seq)]
    for si, h in [(si, h) for h in range(heads) for si in range(n_seq)]:
        rq = slice(si * seq, (si + 1) * seq)
        a_ref = a_scr.at[(si * heads + h) % a_scr.shape[0]]
        ks = slice(h * HEAD_K, (h + 1) * HEAD_K)
        vs = slice(h * HEAD_V, (h + 1) * HEAD_V)
        q = q_ref[rq, ks]
        k = k_ref[rq, ks]
        v = v_ref[rq, vs]
        bf = _join_prefix(gf_ref[rq, ks], seg)
        sb = _join_suffix(gb_ref[rq, ks], seg)

        e0f = jnp.exp2(_diag_args(bf, DIAG_BLOCK // 2))
        e0b = jnp.exp2(_diag_args(sb, DIAG_BLOCK // 2 - 1))
        q0f, k0f = scale(q, e0f), scale(k, 1.0 / e0f)
        q0b, k0b = scale(q, e0b), scale(k, 1.0 / e0b)

        lhs_levels, rhs_levels = [], []
        s = DIAG_BLOCK
        while s < seq:
            la, ra = _pair_args(bf, sb, s)
            lhs_levels.append(scale(q, jnp.exp2(la)))
            rhs_levels.append(scale(k, jnp.exp2(ra)))
            s *= 2

        for base in range(0, seq, SUPER_BLOCK):
            rs = slice(base, base + SUPER_BLOCK)
            m0f = _dot_nt(q0f[rs], k0f[rs])
            m0b = _dot_nt(q0b[rs], k0b[rs])
            m1 = _dot_nt(lhs_levels[0][rs], rhs_levels[0][rs])
            m2 = _dot_nt(lhs_levels[1][rs], rhs_levels[1][rs])
            for bi in range(SUPER_BLOCK // DIAG_BLOCK):
                rr = slice(bi * DIAG_BLOCK, (bi + 1) * DIAG_BLOCK)
                for lt in range(2):
                    cc = slice(lt * half, (lt + 1) * half)
                    if lt == bi // 2:
                        c_loc = lane - DIAG_BLOCK * (bi % 2)
                        in_diag = (lane >= DIAG_BLOCK) if bi % 2 else (lane < DIAG_BLOCK)
                        diag = (jnp.where(c_loc <= r_loc, m0f[rr, cc], 0.0)
                                + jnp.where(c_loc >= r_loc, m0b[rr, cc], 0.0))
                        piece = jnp.where(in_diag, diag, m1[rr, cc])
                    else:
                        piece = m2[rr, cc]
                    a_ref[base + bi * DIAG_BLOCK:base + (bi + 1) * DIAG_BLOCK,
                          base + lt * half:base + (lt + 1) * half] = piece.astype(BF16)
        s = SUPER_BLOCK
        lev = 2
        while s < seq:
            lhs, rhs = lhs_levels[lev], rhs_levels[lev]
            for e0 in range(0, seq, 2 * s):
                ev = slice(e0, e0 + s)
                od = slice(e0 + s, e0 + 2 * s)
                a_ref[od, ev] = _dot_nt(lhs[od], rhs[ev]).astype(BF16)
                a_ref[ev, od] = _dot_nt(lhs[ev], rhs[od]).astype(BF16)
            s *= 2
            lev += 1

        o = _dot(a_ref[...], v)
        if has_state_in:
            qi = jnp.concatenate([scale(q, jnp.exp2(bf)), scale(q, jnp.exp2(sb))], axis=1)
            s0 = jnp.concatenate([s0f_ref[si, h].astype(BF16), s0b_ref[si, h].astype(BF16)], axis=0)
            o = o + _dot(qi, s0)
        if has_state_out:
            sf_ref[si, h] = _dot_tn(scale(k, jnp.exp2(bf[seq - 1:seq, :] - bf)), v)
            sb_ref[si, h] = _dot_tn(scale(k, jnp.exp2(sb[0:1, :] - sb)), v)

        inv = lax.rsqrt(_dot((o * o).astype(BF16), ones) * (1.0 / HEAD_V) + EPS)
        o = jnp.concatenate([o[:, :half] * inv, o[:, half:] * inv], axis=1) * gn_ref[...]
        p = p_ref[rq, vs].astype(F32)
        conv = (jnp.where(rows_v >= 1, pltpu.roll(p, 1, axis=0), 0.0) * cw_ref[0:1, vs] + p * cw_ref[1:2, vs]
                + jnp.where(rows_v < seq - 1, pltpu.roll(p, seq - 1, axis=0), 0.0) * cw_ref[2:3, vs])
        mixes[si].append(og_ref[rq, vs] * o.astype(BF16) + cbg_ref[rq, vs] * conv.astype(BF16))

    contrib = _dot(_cat([_cat(m, axis=1) for m in mixes], axis=0), wo_ref[...])
    ga1 = mod_ref[:, 2 * D_MODEL:3 * D_MODEL]
    if hsteps == 1:
        x1_ref[...] = x_ref[...] + ga1 * contrib
    else:
        hstep = pl.program_id(1)

        @pl.when(hstep == 0)
        def _():
            x1_ref[...] = x_ref[...] + ga1 * contrib

        @pl.when(hstep != 0)
        def _():
            x1_ref[...] += ga1 * contrib


def _mixer(x, mod3, proj, conv_w, gn, wo, states, *, seq, heads, seqs_per_step, seg, first_row, mod_row,
           has_state_out, run_after=None):
    q, k, v, og, p, cbg, g = proj
    n = x.shape[0]
    rows = seqs_per_step * seq
    hsteps = N_HEADS // heads
    has_state_in = states is not None
    assert seq % SUPER_BLOCK == 0 and seq % seg == 0 and first_row % rows == 0 and n % rows == 0
    b0 = first_row // rows
    a_buffers = max(1, min(seqs_per_step * heads, A_SCRATCH_BYTES // (2 * seq * seq)))
    col = lambda b, h: (b0 + b, h)
    st_block = (seqs_per_step, None, heads, HEAD_K, HEAD_V)
    st_map = lambda b, h: (b, 0, h, 0, 0)
    in_specs = [pl.BlockSpec((rows, heads * HEAD_K), col),
                pl.BlockSpec((rows, heads * HEAD_K), col),
                pl.BlockSpec((rows, heads * HEAD_V), col),
                pl.BlockSpec((rows, heads * HEAD_K), col),
                pl.BlockSpec((rows, heads * HEAD_K), lambda b, h: (b0 + b, hsteps + h)),
                pl.BlockSpec((rows, heads * HEAD_V), col),
                pl.BlockSpec((rows, heads * HEAD_V), col),
                pl.BlockSpec((rows, heads * HEAD_V), col),
                pl.BlockSpec((3, heads * HEAD_V), lambda b, h: (0, h)),
                pl.BlockSpec((1, HEAD_V), lambda b, h: (0, 0)),
                (pl.BlockSpec((heads * HEAD_V, D_MODEL), lambda b, h: (h, 0)) if hsteps > 1
                 else _const_spec((heads * HEAD_V, D_MODEL))),
                pl.BlockSpec((rows, D_MODEL), lambda b, h: (b, 0)),
                pl.BlockSpec((None, 1, N_MOD * D_MODEL), lambda b, h: (mod_row(b), 0, 0))]
    args = [q, k, v, g, g, og, p, cbg, conv_w, gn, wo, x, mod3]
    if has_state_in:
        in_specs += [pl.BlockSpec(st_block, st_map)] * 2
        args += list(states)
    if run_after is not None:
        in_specs.append(pl.BlockSpec(memory_space=pl.ANY))
        args.append(run_after)
    out_shape = [jax.ShapeDtypeStruct((n, D_MODEL), F32)]
    out_specs = [pl.BlockSpec((rows, D_MODEL), lambda b, h: (b, 0))]
    if has_state_out:
        out_shape += [jax.ShapeDtypeStruct((n // seq, 1, N_HEADS, HEAD_K, HEAD_V), F32)] * 2
        out_specs += [pl.BlockSpec(st_block, st_map)] * 2
    return pl.pallas_call(
        functools.partial(_mixer_kernel, seq=seq, heads=heads, seg=seg, has_state_in=has_state_in,
                          has_state_out=has_state_out, has_order_token=run_after is not None),
        out_shape=out_shape,
        grid=(n // rows, hsteps),
        in_specs=in_specs,
        out_specs=out_specs,
        scratch_shapes=[pltpu.VMEM((a_buffers, seq, seq), BF16)],
        compiler_params=_params("arbitrary", "arbitrary", vmem=MIXER_VMEM_LIMIT_BYTES),
        name="mixer",
    )(*args)


N_FFN_TILES = D_FF // FFN_TILE


def _ffn_kernel(x1_ref, mod_ref, n2_ref, wu_ref, wg_ref, cw_ref, cb_ref, wd_ref, nf_ref, o_ref, acc_ref,
                *, seq, width):
    cols = lambda f: slice(f * FFN_TILE, (f + 1) * FFN_TILE)
    _ffn_tile(x1_ref, mod_ref, n2_ref, cw_ref, cb_ref, nf_ref, o_ref, acc_ref, seq, width,
              lambda f: wu_ref[:, cols(f)], lambda f: wg_ref[:, cols(f)], lambda f: wd_ref[cols(f), :])


def _ffn_staging_kernel(x1_ref, mod_ref, n2_ref, wu_ref, wg_ref, cw_ref, cb_ref, wd_ref, nf_ref,
                        o_ref, wu_out, wg_out, wd_out, wu_scr, wg_scr, wd_scr, acc_ref, *, seq, width):
    step = pl.program_id(0)

    @pl.when(step < N_FFN_TILES)
    def _():
        for src, scr, out in ((wu_ref, wu_scr, wu_out), (wg_ref, wg_scr, wg_out), (wd_ref, wd_scr, wd_out)):
            w = src[...].astype(BF16)
            scr[step] = w
            out[...] = w

    @pl.when(step >= N_FFN_TILES)
    def _():
        _ffn_tile(x1_ref, mod_ref, n2_ref, cw_ref, cb_ref, nf_ref, o_ref, acc_ref, seq, width,
                  lambda f: wu_scr[f], lambda f: wg_scr[f], lambda f: wd_scr[f])


def _ffn_tile(x1_ref, mod_ref, n2_ref, cw_ref, cb_ref, nf_ref, o_ref, acc_ref, seq, width, wu, wg, wd):
    tm = x1_ref.shape[0]
    mod = mod_ref[...]
    sh2 = mod[:, 3 * D_MODEL:4 * D_MODEL]
    sc2 = mod[:, 4 * D_MODEL:5 * D_MODEL]
    ga2 = mod[:, 5 * D_MODEL:6 * D_MODEL]
    x1 = x1_ref[...]
    xb = (_rms(x1, n2_ref[...]) * (1.0 + sc2) + sh2).astype(BF16)

    rows = lax.broadcasted_iota(jnp.int32, (tm, FFN_TILE), 0)
    col_in_row = rows & (width - 1)
    has_left = col_in_row != 0
    has_right = col_in_row != width - 1
    n_rows = seq // width
    zrow = jnp.zeros((width, FFN_TILE), F32)

    def up_rows(z):
        return jnp.concatenate([zrow, z[:tm - width]], axis=0)

    def down_rows(z):
        return jnp.concatenate([z[width:], zrow], axis=0)

    def up_and_gate(f):
        return _dot(xb, wu(f)), _dot(xb, wg(f))

    ahead = up_and_gate(0)
    for f in range(N_FFN_TILES):
        fs = slice(f * FFN_TILE, (f + 1) * FFN_TILE)
        up, gate = ahead
        if f + 1 < N_FFN_TILES:
            ahead = up_and_gate(f + 1)
        left = jnp.where(has_left, pltpu.roll(up, 1, axis=0), 0.0)
        right = jnp.where(has_right, pltpu.roll(up, tm - 1, axis=0), 0.0)
        w = lambda i, j: cw_ref[3 * i + j:3 * i + j + 1, fs]
        taps = lambda i: left * w(i, 0) + up * w(i, 1) + right * w(i, 2)
        cv = taps(1) + cb_ref[:, fs]
        if n_rows > 1:
            cv = cv + up_rows(taps(0)) + down_rows(taps(2))
        act = (cv * _sigmoid(cv) * gate).astype(BF16)
        part = _dot(act, wd(f))
        if f == 0:
            acc_ref[...] = part
        else:
            acc_ref[...] += part

    o_ref[...] = _rms(x1 + ga2 * acc_ref[...], nf_ref[...])


def _ffn_staging(x1, mod3, n2g, wu, wg, cw, cb, wd, nfg, *, tm, seq, width, mod_row):
    assert width & (width - 1) == 0 and tm % seq == 0 and (seq == width or tm == seq)
    tile = lambda i: jnp.maximum(i - N_FFN_TILES, 0)
    w_step = lambda i: jnp.minimum(i, N_FFN_TILES - 1)
    col_blk = pl.BlockSpec((D_MODEL, FFN_TILE), lambda i: (0, w_step(i)))
    row_blk = pl.BlockSpec((FFN_TILE, D_MODEL), lambda i: (w_step(i), 0))
    return pl.pallas_call(
        functools.partial(_ffn_staging_kernel, seq=seq, width=width),
        out_shape=[jax.ShapeDtypeStruct(x1.shape, F32), jax.ShapeDtypeStruct(wu.shape, BF16),
                   jax.ShapeDtypeStruct(wg.shape, BF16), jax.ShapeDtypeStruct(wd.shape, BF16)],
        grid=(N_FFN_TILES + x1.shape[0] // tm,),
        in_specs=[pl.BlockSpec((tm, D_MODEL), lambda i: (tile(i), 0)),
                  pl.BlockSpec((None, 1, N_MOD * D_MODEL), lambda i: (mod_row(tile(i)), 0, 0)),
                  _const_spec(n2g.shape), col_blk, col_blk,
                  _const_spec(cw.shape), _const_spec(cb.shape), row_blk,
                  _const_spec(nfg.shape)],
        out_specs=[pl.BlockSpec((tm, D_MODEL), lambda i: (tile(i), 0)), col_blk, col_blk, row_blk],
        scratch_shapes=[pltpu.VMEM((N_FFN_TILES, D_MODEL, FFN_TILE), BF16),
                        pltpu.VMEM((N_FFN_TILES, D_MODEL, FFN_TILE), BF16),
                        pltpu.VMEM((N_FFN_TILES, FFN_TILE, D_MODEL), BF16),
                        pltpu.VMEM((tm, D_MODEL), F32)],
        compiler_params=_params("arbitrary"),
        name="ffn_staging",
    )(x1, mod3, n2g, wu, wg, cw, cb, wd, nfg)


def _ffn(x1, mod3, n2g, wu, wg, cw, cb, wd, nfg, *, tm, seq, width, mod_row):
    assert width & (width - 1) == 0 and tm % seq == 0 and (seq == width or tm == seq)
    return pl.pallas_call(
        functools.partial(_ffn_kernel, seq=seq, width=width),
        out_shape=jax.ShapeDtypeStruct(x1.shape, F32),
        grid=(x1.shape[0] // tm,),
        in_specs=[pl.BlockSpec((tm, D_MODEL), lambda i: (i, 0)),
                  pl.BlockSpec((None, 1, N_MOD * D_MODEL), lambda i: (mod_row(i), 0, 0)),
                  _const_spec(n2g.shape), _const_spec(wu.shape), _const_spec(wg.shape),
                  _const_spec(cw.shape), _const_spec(cb.shape), _const_spec(wd.shape),
                  _const_spec(nfg.shape)],
        out_specs=pl.BlockSpec((tm, D_MODEL), lambda i: (i, 0)),
        scratch_shapes=[pltpu.VMEM((tm, D_MODEL), F32)],
        compiler_params=_params("arbitrary"),
        name="ffn",
    )(x1, mod3, n2g, wu, wg, cw, cb, wd, nfg)


def kernel(x_prompt, x_sample, c, state_gla_fwd, state_gla_bwd, c_ctx, w_ada, b_ada, norm1_g, w_in, w_gk_f,
           b_gk_f, w_gk_b, b_gk_b, gla_norm_g, conv_mix_w, w_out, norm2_g, ffn_w_up, ffn_w_gate,
           ffn_conv_w, ffn_conv_b, ffn_w_down, normf_g):
    bp, lp, d = x_prompt.shape
    bs, ls, _ = x_sample.shape
    assert w_ada.shape[0] == 1 and d == D_MODEL and bs + 1 <= MOD_ROWS

    mod3 = _ada(c_ctx[None, :], c, w_ada[0], b_ada)

    assert IN_SPLITS == (KEY_DIM, KEY_DIM, VAL_DIM, VAL_DIM, GATE_RANK, GATE_RANK) + (CONV_DIM,) * 5
    wo = w_out[0].astype(BF16)
    cw9 = ffn_conv_w[0].reshape(9, D_FF)
    cb = ffn_conv_b
    n1g, n2g, nfg, gn = norm1_g, norm2_g, normf_g[None, :], gla_norm_g

    xp, xs = x_prompt.reshape(-1, d), x_sample.reshape(-1, d)
    tiles_p = xp.shape[0] // INPROJ_ROWS
    seg = min(lp, ls, INPROJ_ROWS)
    proj = _inproj(xp, xs, mod3, n1g, w_in[0].T, w_gk_f[0], w_gk_b[0], b_gk_f, b_gk_b, tm=INPROJ_ROWS, seg=seg,
                   mod_row=lambda t: jnp.where(t < tiles_p, 0, 1 + ((t - tiles_p) * INPROJ_ROWS) // ls))

    mix = functools.partial(_mixer, mod3=mod3, proj=proj, conv_w=conv_mix_w[0], gn=gn, wo=wo, seg=seg)
    x1p, new_f, new_b = mix(xp, states=None, seq=lp, heads=N_HEADS, seqs_per_step=PROMPT_SEQS_PER_STEP,
                            first_row=0, mod_row=lambda b: 0, has_state_out=True)
    y_prompt, wu, wg, wd = _ffn_staging(x1p, mod3, n2g, ffn_w_up[0], ffn_w_gate[0], cw9, cb, ffn_w_down[0], nfg,
                                        tm=PROMPT_FFN_ROWS, seq=lp, width=lp, mod_row=lambda i: 0)
    x1s, = mix(xs, states=(state_gla_fwd, state_gla_bwd), seq=ls, heads=N_HEADS, seqs_per_step=1,
               first_row=xp.shape[0], mod_row=lambda b: 1 + b, has_state_out=False, run_after=wd)
    y_sample = _ffn(x1s, mod3, n2g, wu, wg, cw9, cb, wd, nfg, tm=FFN_ROWS, seq=ls, width=GRID_W,
                    mod_row=lambda i: 1 + (i * FFN_ROWS) // ls)
    return y_prompt.reshape(x_prompt.shape), y_sample.reshape(x_sample.shape), new_f, new_b
```
